```python
import functools
import jax, jax.numpy as jnp
from jax import lax
import numpy as np

D_MODEL = 1024
BATCH = 4
SEQ = 4096
DEPTH = 2
DEC_BATCH = 128
DEC_SEQ = 8
PAST_LEN = 2048
PAGE_SIZE = 128

N_META = 16
EPS = 1e-6
HD_A = 64
C_A = D_MODEL // 2
H_A = C_A // HD_A
Q_BLOCK = 128
FOX_BIAS = 4.0
HD_B = 64
C_B = D_MODEL // 4
H_B = C_B // HD_B
LORA_W = 64
LORA_A = 64
LORA_G = 128
GN_EPS = 64e-5
N_RWKV = 3 * C_B + LORA_W + LORA_A + LORA_G
GRP = 16
C_C = D_MODEL // 4
G_C = C_C // GRP
P_C = 64
D_FF = ((8 * D_MODEL // 3 + 127) // 128) * 128
CONV_W = 3
N_IN = 3 * C_A + H_A + N_RWKV + C_C + 3 * D_MODEL

kernel_name = 'hybrid_fox_rwkv7_s5_decode_step'


def _split_cols(x, sizes):
    return jnp.split(x, np.cumsum(sizes)[:-1].tolist(), axis=-1)


def rmsnorm(x, g):
    xf = x.astype(jnp.float32)
    y = xf * lax.rsqrt(jnp.mean(xf * xf, axis=-1, keepdims=True) + EPS)
    return (y * g.astype(jnp.float32)).astype(x.dtype)


def fox_attend(q, cq, qpos, k, v, ck):
    s = jnp.einsum('bqhd,bkhd->bhqk', q, k).astype(jnp.float32) * (HD_A ** -0.5)
    s = s + jnp.transpose(cq, (0, 2, 1))[..., None] - jnp.transpose(ck, (0, 2, 1))[:, :, None, :]
    mask = jnp.arange(k.shape[1])[None, :] <= qpos[:, None]
    s = jnp.where(mask, s, -jnp.inf)
    p = jax.nn.softmax(s, axis=-1).astype(v.dtype)
    return jnp.einsum('bhqk,bkhd->bqhd', p, v)


def fox_prompt_attn(q, k, v, logf):
    B, L = q.shape[0], q.shape[1]
    T = L - N_META
    nb = T // Q_BLOCK
    c = jnp.cumsum(logf, axis=1)
    out_meta = fox_attend(q[:, :N_META], c[:, :N_META], jnp.arange(N_META), k, v, c)
    qb = jnp.transpose(q[:, N_META:].reshape(B, nb, Q_BLOCK, H_A, HD_A), (1, 0, 2, 3, 4))
    cb = jnp.transpose(c[:, N_META:].reshape(B, nb, Q_BLOCK, H_A), (1, 0, 2, 3))
    pos = N_META + jnp.arange(T).reshape(nb, Q_BLOCK)
    out = lax.map(lambda a: fox_attend(a[0], a[1], a[2], k, v, c), (qb, cb, pos))
    out = jnp.transpose(out, (1, 0, 2, 3, 4)).reshape(B, T, H_A, HD_A)
    return jnp.concatenate([out_meta, out], axis=1)


def _gather_pages(pool, page_table):
    g = pool[page_table]
    return g.reshape((g.shape[0], g.shape[1] * g.shape[2]) + g.shape[3:])


def fox_sample_attn(ck_pool, cv_pool, cf_pool, page_table, q, k, v, logf):
    T = q.shape[1]
    kp = _gather_pages(ck_pool, page_table)
    vp = _gather_pages(cv_pool, page_table)
    fp = _gather_pages(cf_pool, page_table)
    P = kp.shape[1]
    c = jnp.cumsum(jnp.concatenate([fp.astype(jnp.float32), logf], axis=1), axis=1)
    cq = c[:, P:]
    s = jnp.concatenate([jnp.einsum('bqhd,bkhd->bhqk', q, kp),
                         jnp.einsum('bqhd,bkhd->bhqk', q, k)], axis=-1).astype(jnp.float32)
    s = s * (HD_A ** -0.5) + jnp.transpose(cq, (0, 2, 1))[..., None] - jnp.transpose(c, (0, 2, 1))[:, :, None, :]
    mask = jnp.arange(P + T)[None, :] <= (P + jnp.arange(T))[:, None]
    s = jnp.where(mask, s, -jnp.inf)
    p = jax.nn.softmax(s, axis=-1).astype(v.dtype)
    return (jnp.einsum('bhqk,bkhd->bqhd', p[..., :P], vp)
            + jnp.einsum('bhqk,bkhd->bqhd', p[..., P:], v))


def wkv7_scan(r, decay, k, v, kk, a, S0):
    xs = (jnp.swapaxes(r, 0, 1), jnp.swapaxes(decay, 0, 1), jnp.swapaxes(k, 0, 1),
          jnp.swapaxes(v, 0, 1), jnp.swapaxes(kk, 0, 1), jnp.swapaxes(a, 0, 1))

    def step(S, inp):
        r_t, w_t, k_t, v_t, kk_t, a_t = inp
        sa = jnp.einsum('bhvk,bhk->bhv', S, -kk_t)
        S = (S * w_t[:, :, None, :] + sa[..., None] * (kk_t * a_t)[:, :, None, :]
             + v_t[..., None] * k_t[:, :, None, :])
        return S, jnp.einsum('bhvk,bhk->bhv', S, r_t)

    S, ys = lax.scan(step, S0, xs)
    return jnp.swapaxes(ys, 0, 1), S


def rwkv7_branch(z, prev, S0, lw):
    B, T = z.shape[0], z.shape[1]
    f32 = jnp.float32
    zf = z.astype(f32)
    zprev = jnp.concatenate([prev.astype(f32)[:, None], zf[:, :-1]], axis=1)
    zm = zf + (zprev - zf) * lw['rwkv_mu'].astype(f32)
    r, k, v, xw, xa, xg = _split_cols(zm, (C_B, C_B, C_B, LORA_W, LORA_A, LORA_G))
    w = -jax.nn.softplus(-(lw['rwkv_w0'].astype(f32) + jnp.tanh(xw) @ lw['rwkv_w2'].astype(f32))) - 0.5
    decay = jnp.exp(-jnp.exp(w))
    a = jax.nn.sigmoid(lw['rwkv_a0'].astype(f32) + xa @ lw['rwkv_a2'].astype(f32))
    g = jax.nn.sigmoid(xg) @ lw['rwkv_g2'].astype(f32)
    heads = lambda t: t.reshape(B, T, H_B, HD_B)
    kk = heads(k * lw['rwkv_kk'].astype(f32))
    kk = kk / jnp.maximum(jnp.sqrt(jnp.sum(kk * kk, axis=-1, keepdims=True)), 1e-12)
    k = k * (1.0 + (a - 1.0) * lw['rwkv_ka'].astype(f32))
    r, k, v, a, decay = heads(r), heads(k), heads(v), heads(a), heads(decay)
    y, S = wkv7_scan(r, decay, k, v, kk, a, S0.astype(f32))
    mu = jnp.mean(y, axis=-1, keepdims=True)
    var = jnp.mean(jnp.square(y - mu), axis=-1, keepdims=True)
    yn = ((y - mu) * lax.rsqrt(var + GN_EPS)).reshape(B, T, C_B)
    yn = yn * lw['rwkv_lnx_g'].astype(f32) + lw['rwkv_lnx_b'].astype(f32)
    bonus = (jnp.sum(r * k * lw['rwkv_rk'].astype(f32), axis=-1, keepdims=True) * v).reshape(B, T, C_B)
    return (yn + bonus) * g, z[:, -1], S


def _cmul_combine(e1, e2):
    a1r, a1i, b1r, b1i = e1
    a2r, a2i, b2r, b2i = e2
    return (a2r * a1r - a2i * a1i, a2r * a1i + a2i * a1r,
            a2r * b1r - a2i * b1i + b2r, a2r * b1i + a2i * b1r + b2i)


def s5_branch(u, re0, im0, lw):
    B, T = u.shape[0], u.shape[1]
    f32 = jnp.float32
    a_re = lw['ssm_a_re'].astype(f32)
    a_im = lw['ssm_a_im'].astype(f32)
    step = jnp.exp(lw['ssm_log_dt'].astype(f32))[:, None]
    mag = jnp.exp(step * a_re)
    ab_re = mag * jnp.cos(step * a_im)
    ab_im = mag * jnp.sin(step * a_im)
    den = a_re * a_re + a_im * a_im
    cf_re = ((ab_re - 1.0) * a_re + ab_im * a_im) / den
    cf_im = (ab_im * a_re - (ab_re - 1.0) * a_im) / den
    b_re = lw['ssm_b_re'].astype(f32)
    b_im = lw['ssm_b_im'].astype(f32)
    bb_re = cf_re[..., None] * b_re - cf_im[..., None] * b_im
    bb_im = cf_re[..., None] * b_im + cf_im[..., None] * b_re
    uf = u.astype(f32)
    ug = uf.reshape(B, T, G_C, GRP)
    bu_re = jnp.einsum('gpc,btgc->btgp', bb_re, ug)
    bu_im = jnp.einsum('gpc,btgc->btgp', bb_im, ug)
    re0 = re0.astype(f32)
    im0 = im0.astype(f32)
    bu_re = bu_re.at[:, 0].add(ab_re * re0 - ab_im * im0)
    bu_im = bu_im.at[:, 0].add(ab_re * im0 + ab_im * re0)
    el = (jnp.broadcast_to(ab_re, bu_re.shape), jnp.broadcast_to(ab_im, bu_im.shape), bu_re, bu_im)
    _, _, x_re, x_im = lax.associative_scan(_cmul_combine, el, axis=1)
    y = (jnp.einsum('gcp,btgp->btgc', lw['ssm_c_re'].astype(f32), x_re)
         - jnp.einsum('gcp,btgp->btgc', lw['ssm_c_im'].astype(f32), x_im))
    y = y.reshape(B, T, C_C) + lw['ssm_d'].astype(f32) * uf
    hg = jax.nn.gelu(y)
    out = hg * jax.nn.sigmoid(hg @ lw['ssm_glu_w'].astype(f32) + lw['ssm_glu_b'].astype(f32))
    return out, x_re[:, -1], x_im[:, -1]


def conv_ffn(x, prev, lw):
    h = x @ lw['ffn_up']
    T = h.shape[1]
    hp = jnp.concatenate([prev.astype(h.dtype), h], axis=1)
    w = lw['ffn_conv_w']
    hc = sum(w[j] * hp[:, j:j + T] for j in range(CONV_W)) + lw['ffn_conv_b']
    a, b = jnp.split(hc, 2, axis=-1)
    return (jax.nn.silu(a) * b) @ lw['ffn_down'], hp[:, -(CONV_W - 1):]


def trunk_layer(h, lw, attn_fn, shift_prev, wkv_prev, re_prev, im_prev, conv_prev):
    B, T = h.shape[0], h.shape[1]
    dt = h.dtype
    xn = rmsnorm(h, lw['norm1_g'])
    proj = xn @ lw['w_in']
    q, k, v, f_in, z_rwkv, u_ssm, g_a, g_b, g_c = _split_cols(
        proj, (C_A, C_A, C_A, H_A, N_RWKV, C_C, D_MODEL, D_MODEL, D_MODEL))
    q = q.reshape(B, T, H_A, HD_A)
    k = k.reshape(B, T, H_A, HD_A)
    v = v.reshape(B, T, H_A, HD_A)
    logf = jax.nn.log_sigmoid(f_in.astype(jnp.float32) + lw['fox_bf'].astype(jnp.float32))
    o_a = attn_fn(q, k, v, logf).reshape(B, T, C_A)
    o_b, shift_new, wkv_new = rwkv7_branch(z_rwkv, shift_prev, wkv_prev, lw)
    o_c, re_new, im_new = s5_branch(u_ssm, re_prev, im_prev, lw)
    merged = (jax.nn.sigmoid(g_a) * (o_a.astype(dt) @ lw['proj_a'])
              + jax.nn.sigmoid(g_b) * (o_b.astype(dt) @ lw['proj_b'])
              + jax.nn.sigmoid(g_c) * (o_c.astype(dt) @ lw['proj_c']))
    h = h + merged @ lw['w_out']
    f_out, conv_new = conv_ffn(rmsnorm(h, lw['norm2_g']), conv_prev, lw)
    h = h + f_out
    return h, (k, v, logf, wkv_new, shift_new, re_new, im_new, conv_new)


def setup_inputs(seed: int = 0) -> dict:
    key = jax.random.key(seed)
    ks = iter(jax.random.split(key, 64))
    f32 = jnp.float32
    nrm = lambda shape, scale=1.0: scale * jax.random.normal(next(ks), shape, f32)
    uni = lambda shape, lo, hi: jax.random.uniform(next(ks), shape, f32, lo, hi)
    n_pages = PAST_LEN // PAGE_SIZE
    n_pool = (5 * DEC_BATCH * n_pages + 3) // 4
    page_table = jax.random.permutation(next(ks), n_pool)[:DEC_BATCH * n_pages]
    page_table = page_table.reshape(DEC_BATCH, n_pages).astype(jnp.int32)
    a_im_base = jnp.broadcast_to(jnp.pi * jnp.arange(P_C, dtype=f32), (DEPTH, G_C, P_C))
    return {
        'x_prompt': nrm((BATCH, SEQ, D_MODEL)),
        'x_sample': nrm((DEC_BATCH, DEC_SEQ, D_MODEL)),
        'cache_k': nrm((DEPTH, n_pool, PAGE_SIZE, H_A, HD_A)),
        'cache_v': nrm((DEPTH, n_pool, PAGE_SIZE, H_A, HD_A)),
        'cache_logf': jax.nn.log_sigmoid(FOX_BIAS + nrm((DEPTH, n_pool, PAGE_SIZE, H_A))),
        'page_table': page_table,
        'state_rwkv_wkv': nrm((DEPTH, DEC_BATCH, H_B, HD_B, HD_B), 0.5),
        'state_rwkv_shift': nrm((DEPTH, DEC_BATCH, N_RWKV)),
        'state_ssm_re': nrm((DEPTH, DEC_BATCH, G_C, P_C), 0.5),
        'state_ssm_im': nrm((DEPTH, DEC_BATCH, G_C, P_C), 0.5),
        'state_ffn_conv': nrm((DEPTH, DEC_BATCH, CONV_W - 1, 2 * D_FF)),
        'meta_tokens': nrm((N_META, D_MODEL)),
        'norm1_g': 1.0 + nrm((DEPTH, D_MODEL), 0.02),
        'w_in': nrm((DEPTH, D_MODEL, N_IN), D_MODEL ** -0.5),
        'fox_bf': FOX_BIAS + nrm((DEPTH, H_A), 0.1),
        'rwkv_mu': uni((DEPTH, N_RWKV), 0.0, 1.0),
        'rwkv_w0': uni((DEPTH, C_B), -6.0, -1.0),
        'rwkv_w2': nrm((DEPTH, LORA_W, C_B), LORA_W ** -0.5),
        'rwkv_a0': nrm((DEPTH, C_B), 0.1),
        'rwkv_a2': nrm((DEPTH, LORA_A, C_B), LORA_A ** -0.5),
        'rwkv_g2': nrm((DEPTH, LORA_G, C_B), LORA_G ** -0.5),
        'rwkv_kk': 0.85 + nrm((DEPTH, C_B), 0.02),
        'rwkv_ka': 1.0 + nrm((DEPTH, C_B), 0.02),
        'rwkv_rk': nrm((DEPTH, H_B, HD_B), 0.1),
        'rwkv_lnx_g': 1.0 + nrm((DEPTH, C_B), 0.02),
        'rwkv_lnx_b': nrm((DEPTH, C_B), 0.02),
        'ssm_a_re': -0.5 + nrm((DEPTH, G_C, P_C), 0.01),
        'ssm_a_im': a_im_base + nrm((DEPTH, G_C, P_C), 0.01),
        'ssm_log_dt': uni((DEPTH, G_C), float(np.log(1e-3)), float(np.log(1e-1))),
        'ssm_b_re': nrm((DEPTH, G_C, P_C, GRP), (2 * GRP) ** -0.5),
        'ssm_b_im': nrm((DEPTH, G_C, P_C, GRP), (2 * GRP) ** -0.5),
        'ssm_c_re': nrm((DEPTH, G_C, GRP, P_C), (2 * P_C) ** -0.5),
        'ssm_c_im': nrm((DEPTH, G_C, GRP, P_C), (2 * P_C) ** -0.5),
        'ssm_d': nrm((DEPTH, C_C)),
        'ssm_glu_w': nrm((DEPTH, C_C, C_C), C_C ** -0.5),
        'ssm_glu_b': nrm((DEPTH, C_C), 0.02),
        'proj_a': nrm((DEPTH, C_A, D_MODEL), C_A ** -0.5),
        'proj_b': nrm((DEPTH, C_B, D_MODEL), C_B ** -0.5),
        'proj_c': nrm((DEPTH, C_C, D_MODEL), C_C ** -0.5),
        'w_out': nrm((DEPTH, D_MODEL, D_MODEL), D_MODEL ** -0.5),
        'norm2_g': 1.0 + nrm((DEPTH, D_MODEL), 0.02),
        'ffn_up': nrm((DEPTH, D_MODEL, 2 * D_FF), D_MODEL ** -0.5),
        'ffn_conv_w': nrm((DEPTH, CONV_W, 2 * D_FF), 0.5),
        'ffn_conv_b': nrm((DEPTH, 2 * D_FF), 0.02),
        'ffn_down': nrm((DEPTH, D_FF, D_MODEL), D_FF ** -0.5),
        'final_norm_g': 1.0 + nrm((D_MODEL,), 0.02),
    }


def reference(x_prompt, x_sample, cache_k, cache_v, cache_logf, page_table,
              state_rwkv_wkv, state_rwkv_shift, state_ssm_re, state_ssm_im, state_ffn_conv,
              meta_tokens, norm1_g, w_in, fox_bf, rwkv_mu, rwkv_w0, rwkv_w2, rwkv_a0, rwkv_a2,
              rwkv_g2, rwkv_kk, rwkv_ka, rwkv_rk, rwkv_lnx_g, rwkv_lnx_b,
              ssm_a_re, ssm_a_im, ssm_log_dt, ssm_b_re, ssm_b_im, ssm_c_re, ssm_c_im, ssm_d,
              ssm_glu_w, ssm_glu_b, proj_a, proj_b, proj_c, w_out, norm2_g,
              ffn_up, ffn_conv_w, ffn_conv_b, ffn_down, final_norm_g):
    f32 = jnp.float32
    dt = x_prompt.dtype
    B = x_prompt.shape[0]
    meta = jnp.broadcast_to(meta_tokens.astype(dt)[None], (B, N_META, D_MODEL))
    h_p = jnp.concatenate([meta, x_prompt], axis=1)
    h_s = x_sample
    zeros_shift = jnp.zeros((B, N_RWKV), dt)
    zeros_wkv = jnp.zeros((B, H_B, HD_B, HD_B), f32)
    zeros_ssm = jnp.zeros((B, G_C, P_C), f32)
    zeros_conv = jnp.zeros((B, CONV_W - 1, 2 * D_FF), dt)
    new_p = []
    new_s = []
    for l in range(DEPTH):
        lw = dict(norm1_g=norm1_g[l], w_in=w_in[l], fox_bf=fox_bf[l], rwkv_mu=rwkv_mu[l],
                  rwkv_w0=rwkv_w0[l], rwkv_w2=rwkv_w2[l], rwkv_a0=rwkv_a0[l], rwkv_a2=rwkv_a2[l],
                  rwkv_g2=rwkv_g2[l], rwkv_kk=rwkv_kk[l], rwkv_ka=rwkv_ka[l], rwkv_rk=rwkv_rk[l],
                  rwkv_lnx_g=rwkv_lnx_g[l], rwkv_lnx_b=rwkv_lnx_b[l],
                  ssm_a_re=ssm_a_re[l], ssm_a_im=ssm_a_im[l], ssm_log_dt=ssm_log_dt[l],
                  ssm_b_re=ssm_b_re[l], ssm_b_im=ssm_b_im[l], ssm_c_re=ssm_c_re[l],
                  ssm_c_im=ssm_c_im[l], ssm_d=ssm_d[l], ssm_glu_w=ssm_glu_w[l], ssm_glu_b=ssm_glu_b[l],
                  proj_a=proj_a[l], proj_b=proj_b[l], proj_c=proj_c[l], w_out=w_out[l],
                  norm2_g=norm2_g[l], ffn_up=ffn_up[l], ffn_conv_w=ffn_conv_w[l],
                  ffn_conv_b=ffn_conv_b[l], ffn_down=ffn_down[l])
        h_p, st_p = trunk_layer(h_p, lw, fox_prompt_attn, zeros_shift, zeros_wkv,
                                zeros_ssm, zeros_ssm, zeros_conv)
        attn_s = functools.partial(fox_sample_attn, cache_k[l], cache_v[l], cache_logf[l], page_table)
        h_s, st_s = trunk_layer(h_s, lw, attn_s, state_rwkv_shift[l], state_rwkv_wkv[l],
                                state_ssm_re[l], state_ssm_im[l], state_ffn_conv[l])
        new_p.append(st_p)
        new_s.append(st_s)
    y_prompt = rmsnorm(h_p, final_norm_g)[:, N_META:]
    y_sample = rmsnorm(h_s, final_norm_g)
    k_p, v_p, logf_p, wkv_p, shift_p, re_p, im_p, conv_p = [jnp.stack(t) for t in zip(*new_p)]
    k_s, v_s, logf_s, wkv_s, shift_s, re_s, im_s, conv_s = [jnp.stack(t) for t in zip(*new_s)]
    return (y_prompt, y_sample, k_p, v_p, logf_p, wkv_p, shift_p, re_p, im_p, conv_p,
            k_s, v_s, logf_s, wkv_s, shift_s, re_s, im_s, conv_s)
```

```python
import functools

import jax
import jax.numpy as jnp
from jax import lax
from jax.experimental import pallas as pl
from jax.experimental.pallas import tpu as pltpu

F32 = jnp.float32
BF16 = jnp.bfloat16

N_META_TOK = 16
RMS_EPS = 1e-6
HEAD_A = 64
HEAD_B = 64
LORA_W_DIM = 64
LORA_A_DIM = 64
LORA_G_DIM = 128
GROUPNORM_EPS = 64e-5
SSM_GRP = 16
SSM_P = 64
CONV_TAPS = 3
LANES = 128
SUBLANES = 8
VMEM_LIMIT = 52 * 1024 * 1024


def _cparams(*sem):
    return pltpu.CompilerParams(dimension_semantics=sem, vmem_limit_bytes=VMEM_LIMIT)


def _dot(a, b):
    return jnp.dot(a, b, preferred_element_type=F32)


def _dot_nt(a, b):
    return lax.dot_general(a, b, (((1,), (1,)), ((), ())), preferred_element_type=F32)


def _split2(x):
    hi = x.astype(BF16)
    lo = (x - hi.astype(F32)).astype(BF16)
    return hi, lo


def _split3(x):
    hi = x.astype(BF16)
    r = x - hi.astype(F32)
    mid = r.astype(BF16)
    lo = (r - mid.astype(F32)).astype(BF16)
    return hi, mid, lo


def _dot_l2(x, w_bf):
    hi, lo = _split2(x)
    return _dot(hi, w_bf) + _dot(lo, w_bf)


def _dot_l3(x, w_bf):
    hi, mid, lo = _split3(x)
    return _dot(hi, w_bf) + _dot(mid, w_bf) + _dot(lo, w_bf)


def _dot_r3(w_bf, x):
    hi, mid, lo = _split3(x)
    return _dot(w_bf, hi) + _dot(w_bf, mid) + _dot(w_bf, lo)


def _dot_x3(a, b):
    ah, al = _split2(a)
    bh, bl = _split2(b)
    return _dot(ah, bh) + _dot(al, bh) + _dot(ah, bl)


def _sigmoid(x):
    return 1.0 / (1.0 + jnp.exp(-x))


def _log_sigmoid(x):
    return jnp.minimum(x, 0.0) - jnp.log1p(jnp.exp(-jnp.abs(x)))


def _rms(x, g):
    ms = jnp.mean(x * x, axis=-1, keepdims=True)
    return x * lax.rsqrt(ms + RMS_EPS) * g


def _iota(shape, dim):
    return lax.broadcasted_iota(jnp.int32, shape, dim)


def _block_ones(n, blk):
    return jnp.where(_iota((n, n), 0) // blk == _iota((n, n), 1) // blk, 1.0, 0.0).astype(BF16)


def _row_tile(rows, cap):
    t = cap
    while rows % t:
        t //= 2
    return t


def _inproj_body(h_ref, g_ref, w_ref, bf_ref, qb_ref, k_ref, v_ref, kb_ref, vb_ref, lf_ref, z_ref,
                 u_ref, *, c_a, n_rwkv, c_c):
    xn = _rms(h_ref[...], g_ref[...]).astype(BF16)
    q = _dot(xn, w_ref[:, 0:c_a])
    qb_ref[...] = (q * (HEAD_A ** -0.5)).astype(BF16)
    k = _dot(xn, w_ref[:, c_a:2 * c_a])
    k_ref[...] = k
    kb_ref[...] = k.astype(BF16)
    v = _dot(xn, w_ref[:, 2 * c_a:3 * c_a])
    v_ref[...] = v
    vb_ref[...] = v.astype(BF16)
    o = 3 * c_a
    f = _dot(xn, w_ref[:, o:o + LANES]) + bf_ref[...]
    lf_ref[...] = _log_sigmoid(f)
    o += LANES
    z_ref[...] = _dot(xn, w_ref[:, o:o + n_rwkv])
    o += n_rwkv
    u_ref[...] = _dot(xn, w_ref[:, o:o + c_c])


def _inproj(h2, g1, wp, bfp, c_a, n_rwkv, c_c):
    rows, d = h2.shape
    tm = _row_tile(rows, 512)
    npj = wp.shape[1]
    row = lambda n: pl.BlockSpec((tm, n), lambda i: (i, 0))
    full = lambda a: pl.BlockSpec(a.shape, lambda i: (0,) * a.ndim)
    outs = [(c_a, BF16), (c_a, F32), (c_a, F32), (c_a, BF16), (c_a, BF16), (LANES, F32),
            (n_rwkv, F32), (c_c, F32)]
    return pl.pallas_call(
        functools.partial(_inproj_body, c_a=c_a, n_rwkv=n_rwkv, c_c=c_c),
        grid=(rows // tm,),
        in_specs=[row(d), full(g1), full(wp), full(bfp)],
        out_specs=[row(n) for n, _ in outs],
        out_shape=[jax.ShapeDtypeStruct((rows, n), dt) for n, dt in outs],
        compiler_params=_cparams("parallel"),
        name="inproj",
    )(h2, g1, wp, bfp)


def _cumsum_body(lf_ref, c_ref, ct_ref, carry_ref, *, tc):
    @pl.when(pl.program_id(1) == 0)
    def _():
        carry_ref[...] = jnp.zeros_like(carry_ref)

    tri = jnp.where(_iota((tc, tc), 0) >= _iota((tc, tc), 1), 1.0, 0.0).astype(BF16)
    cs = _dot_r3(tri, lf_ref[0]) + carry_ref[...]
    c_ref[0] = cs
    cst = cs.T
    for p in range(ct_ref.shape[1]):
        ct_ref[0, p] = cst[2 * p:2 * p + 2]
    carry_ref[...] = cs[tc - 1:tc]


def _cumsum(lf3, n_pairs):
    b, lp, _ = lf3.shape
    tc = 256
    return pl.pallas_call(
        functools.partial(_cumsum_body, tc=tc),
        grid=(b, lp // tc),
        in_specs=[pl.BlockSpec((1, tc, LANES), lambda i, j: (i, j, 0))],
        out_specs=[pl.BlockSpec((1, tc, LANES), lambda i, j: (i, j, 0)),
                   pl.BlockSpec((1, n_pairs, 2, tc), lambda i, j: (i, 0, 0, j))],
        out_shape=[jax.ShapeDtypeStruct((b, lp, LANES), F32),
                   jax.ShapeDtypeStruct((b, n_pairs, 2, lp), F32)],
        scratch_shapes=[pltpu.VMEM((1, LANES), F32)],
        compiler_params=_cparams("parallel", "arbitrary"),
        name="fox_cumsum",
    )(lf3)


def _attn_body(q_ref, k_ref, v_ref, c_ref, ct_ref, o_ref, *, tq):
    pr = pl.program_id(1)
    i = pl.program_id(2)
    q = q_ref[0]
    lane = _iota((tq, LANES), 1)
    chi, cmid, clo = _split3(c_ref[0])
    rowi = _iota((LANES, LANES), 0)
    qh, cq = [], []
    for hh in range(2):
        qh.append(jnp.where((lane >= HEAD_A * hh) & (lane < HEAD_A * (hh + 1)), q, jnp.zeros_like(q)))
        sel = jnp.where(rowi == 2 * pr + hh, 1.0, 0.0).astype(BF16)
        c1 = _dot(chi, sel) + _dot(cmid, sel) + _dot(clo, sel)
        cq.append(jnp.concatenate([c1] * (tq // LANES), axis=1))
    causal = _iota((tq, tq), 1) <= _iota((tq, tq), 0)

    def block(j, carry, masked):
        start = pl.multiple_of(j * tq, tq)
        kb = k_ref[0, pl.ds(start, tq), :]
        vb = v_ref[0, pl.ds(start, tq), :]
        new = []
        for hh in range(2):
            m, l, acc = carry[hh]
            s = _dot_nt(qh[hh], kb)
            s = s + cq[hh] - ct_ref[0, 0, hh:hh + 1, pl.ds(start, tq)]
            if masked:
                s = jnp.where(causal, s, -jnp.inf)
            m_new = jnp.maximum(m, jnp.max(s, axis=1, keepdims=True))
            alpha = jnp.exp(m - m_new)
            p = jnp.exp(s - m_new)
            l = alpha * l + jnp.sum(p, axis=1, keepdims=True)
            acc = alpha * acc + _dot(p.astype(BF16), vb)
            new.append((m_new, l, acc))
        return tuple(new)

    init = tuple((jnp.full((tq, 1), -jnp.inf, F32), jnp.zeros((tq, 1), F32),
                  jnp.zeros((tq, LANES), F32)) for _ in range(2))
    carry = lax.fori_loop(0, i, lambda j, c: block(j, c, False), init)
    carry = block(i, carry, True)
    o0 = carry[0][2] / carry[0][1]
    o1 = carry[1][2] / carry[1][1]
    o_ref[0] = jnp.where(lane < HEAD_A, o0, o1).astype(BF16)


def _attn_prompt(qb, kb, vb, c, ct):
    b, lp, c_a = qb.shape
    n_pairs = c_a // LANES
    tq = 256
    return pl.pallas_call(
        functools.partial(_attn_body, tq=tq),
        grid=(b, n_pairs, lp // tq),
        in_specs=[pl.BlockSpec((1, tq, LANES), lambda bi, p, i: (bi, i, p)),
                  pl.BlockSpec((1, lp, LANES), lambda bi, p, i: (bi, 0, p)),
                  pl.BlockSpec((1, lp, LANES), lambda bi, p, i: (bi, 0, p)),
                  pl.BlockSpec((1, tq, LANES), lambda bi, p, i: (bi, i, 0)),
                  pl.BlockSpec((1, 1, 2, lp), lambda bi, p, i: (bi, p, 0, 0))],
        out_specs=pl.BlockSpec((1, tq, LANES), lambda bi, p, i: (bi, i, p)),
        out_shape=jax.ShapeDtypeStruct((b, lp, c_a), BF16),
        compiler_params=_cparams("parallel", "parallel", "arbitrary"),
        name="fox_prompt_attn",
    )(qb, kb, vb, c, ct)


def _pool_suffix_body(x_ref, suf_ref, tot_ref):
    upper = jnp.where(_iota((LANES, LANES), 0) > _iota((LANES, LANES), 1), 1.0, 0.0).astype(BF16)
    ones = jnp.ones((LANES, LANES), BF16)
    hi, mid, lo = _split3(x_ref[...])
    suf_ref[...] = _dot(hi, upper) + _dot(mid, upper) + _dot(lo, upper)
    tot_ref[...] = _dot(hi, ones) + _dot(mid, ones) + _dot(lo, ones)


def _pool_suffix(ft2):
    rows = ft2.shape[0]
    tm = _row_tile(rows, 1024)
    spec = pl.BlockSpec((tm, LANES), lambda i: (i, 0))
    return pl.pallas_call(
        _pool_suffix_body,
        grid=(rows // tm,),
        in_specs=[spec],
        out_specs=[spec, spec],
        out_shape=[jax.ShapeDtypeStruct(ft2.shape, F32)] * 2,
        compiler_params=_cparams("parallel"),
        name="fox_pool_suffix",
    )(ft2)


def _attn_sample_body(pt_ref, qbd_ref, kn_ref, vn_ref, lfn_ref, *rest, n_pages, t_new):
    kp = rest[0:n_pages]
    vp = rest[n_pages:2 * n_pages]
    sufp = rest[2 * n_pages:3 * n_pages]
    totp = rest[3 * n_pages:4 * n_pages]
    o_ref = rest[4 * n_pages]
    nr = qbd_ref.shape[1]
    n_h = nr // t_new
    qbd = qbd_ref[0]
    row = _iota((nr, LANES), 0)
    lane = _iota((nr, LANES), 1)

    lfn_pad = jnp.concatenate([lfn_ref[0], jnp.zeros((LANES - t_new, LANES), F32)], axis=0)
    tri = jnp.where(_iota((LANES, LANES), 0) >= _iota((LANES, LANES), 1), 1.0, 0.0).astype(BF16)
    cs = _dot_r3(tri, lfn_pad)
    rep = jnp.where(_iota((nr, LANES), 1) == row // n_h, 1.0, 0.0).astype(BF16)
    cs_rows = _dot_r3(rep, cs)
    cq = jnp.sum(jnp.where(lane == row % n_h, cs_rows, 0.0), axis=1, keepdims=True)
    cst = cs.T[0:n_h]

    tiles = [None] * (n_pages + 1)
    carry = jnp.zeros((n_h, LANES), F32)
    for j in reversed(range(n_pages)):
        s = _dot_nt(qbd, kp[j][...].astype(BF16))
        bias = sufp[j][...] + carry
        tiles[j] = s + cq + jnp.concatenate([bias] * t_new, axis=0)
        carry = carry + totp[j][...]
    s = _dot_nt(qbd, kn_ref[0]) + cq - jnp.concatenate([cst] * t_new, axis=0)
    tiles[n_pages] = jnp.where((lane <= row // n_h) & (lane < t_new), s, -jnp.inf)

    m = tiles[0]
    for tl in tiles[1:]:
        m = jnp.maximum(m, tl)
    m = jnp.max(m, axis=1, keepdims=True)
    lsum = jnp.zeros((nr, LANES), F32)
    acc = jnp.zeros((nr, qbd.shape[1]), F32)
    for j in range(n_pages + 1):
        p = jnp.exp(tiles[j] - m)
        lsum = lsum + p
        vj = vn_ref[0] if j == n_pages else vp[j][...].astype(BF16)
        acc = acc + _dot(p.astype(BF16), vj)
    o_full = acc / jnp.sum(lsum, axis=1, keepdims=True)
    c_a = o_full.shape[1]
    keep = _iota((nr, c_a), 1) // HEAD_A == _iota((nr, c_a), 0) % n_h
    picked = jnp.where(keep, o_full, 0.0).astype(BF16)
    gather = jnp.where((_iota((nr, nr), 1) // n_h == _iota((nr, nr), 0)), 1.0, 0.0).astype(BF16)
    o_ref[0] = _dot(gather, picked)[0:t_new].astype(BF16)


def _attn_sample(page_table, layer, qbd, kn_pad, vn_pad, lfn, cache_k4, cache_v4, suf, tot):
    db, n_pages = page_table.shape
    nr, c_a = qbd.shape[1], qbd.shape[2]
    t_new = lfn.shape[1]
    n_h = nr // t_new
    page = cache_k4.shape[2]

    def seq_spec(a):
        return pl.BlockSpec((1,) + a.shape[1:], lambda b, pt: (b,) + (0,) * (a.ndim - 1))

    def pool_spec(j):
        return pl.BlockSpec((None, None, page, c_a), lambda b, pt, j=j: (layer, pt[b, j], 0, 0))

    def small_spec(j):
        return pl.BlockSpec((None, n_h, LANES), lambda b, pt, j=j: (pt[b, j], 0, 0))

    in_specs = ([seq_spec(qbd), seq_spec(kn_pad), seq_spec(vn_pad), seq_spec(lfn)]
                + [pool_spec(j) for j in range(n_pages)] * 2
                + [small_spec(j) for j in range(n_pages)] * 2)
    grid_spec = pltpu.PrefetchScalarGridSpec(
        num_scalar_prefetch=1,
        grid=(db,),
        in_specs=in_specs,
        out_specs=pl.BlockSpec((1, t_new, c_a), lambda b, pt: (b, 0, 0)),
    )
    args = ([qbd, kn_pad, vn_pad, lfn] + [cache_k4] * n_pages + [cache_v4] * n_pages
            + [suf] * n_pages + [tot] * n_pages)
    return pl.pallas_call(
        functools.partial(_attn_sample_body, n_pages=n_pages, t_new=t_new),
        grid_spec=grid_spec,
        out_shape=jax.ShapeDtypeStruct((db, t_new, c_a), BF16),
        compiler_params=_cparams("parallel"),
        name="fox_paged_attn",
    )(page_table, *args)


def _rwkv_body(z_ref, sp_ref, s0_ref, mu_ref, w0_ref, w2_ref, a0_ref, a2_ref, g2_ref, kkw_ref, ka_ref,
               rk_ref, lg_ref, lb_ref, o_ref, sout_ref,
               s_scr, carry_scr, nkk_s, w_s, be_s, k_s, wr_s, v_s, br_s, kr_s, g_s, bo_s, y_s,
               *, nb, tc, seq_len, c_b):
    c = pl.program_id(1)
    n_pairs = c_b // LANES
    n_chains = nb * n_pairs

    @pl.when(c == 0)
    def _():
        s_scr[...] = s0_ref[...]
        carry_scr[...] = sp_ref[...]

    q_heads = _block_ones(c_b, HEAD_B)
    rowid = _iota((tc, z_ref.shape[2]), 0)
    lora0 = 3 * c_b
    for b in range(nb):
        z = z_ref[b]
        zprev = jnp.where(rowid == 0, carry_scr[b], pltpu.roll(z, 1, 0))
        carry_scr[b] = z[tc - 1:tc]
        zm = z + (zprev - z) * mu_ref[...]
        r = zm[:, 0:c_b]
        k = zm[:, c_b:2 * c_b]
        v = zm[:, 2 * c_b:3 * c_b]
        x_lora = zm[:, lora0:]
        w = _log_sigmoid(w0_ref[...] + _dot_x3(jnp.tanh(x_lora), w2_ref[...])) - 0.5
        dec = jnp.exp(-jnp.exp(w))
        a = _sigmoid(a0_ref[...] + _dot_x3(x_lora, a2_ref[...]))
        g = _dot_x3(_sigmoid(x_lora), g2_ref[...])
        kk = k * kkw_ref[...]
        kk = kk / jnp.maximum(jnp.sqrt(_dot_l2(kk * kk, q_heads)), 1e-12)
        k2 = k * (1.0 + (a - 1.0) * ka_ref[...])
        beta = kk * a
        g_s[b] = g
        bo_s[b] = _dot_l2(r * k2 * rk_ref[...], q_heads) * v
        per_step = ((nkk_s, -kk), (w_s, dec), (be_s, beta), (k_s, k2), (wr_s, dec * r), (v_s, v),
                    (br_s, _dot_l2(beta * r, q_heads)), (kr_s, _dot_l2(k2 * r, q_heads)))
        for ref, val in per_step:
            for p in range(n_pairs):
                ref[pl.ds(b * n_pairs + p, tc, stride=n_chains), :] = val[:, p * LANES:(p + 1) * LANES]

    diag = jnp.where(_iota((HEAD_B, LANES), 0) == _iota((HEAD_B, LANES), 1) % HEAD_B, 1.0, 0.0)
    q_pair = _block_ones(LANES, HEAD_B)
    n_steps = jnp.clip(seq_len - c * tc, 0, tc)
    chains = [(b, p) for b in range(nb) for p in range(n_pairs)]

    @pl.when(n_steps < tc)
    def _():
        y_s[...] = jnp.zeros_like(y_s)

    def step(t, carry):
        rows = pl.ds(pl.multiple_of(t * n_chains, n_chains), n_chains)
        nkk_t, wr_t, v_t = nkk_s[rows, :], wr_s[rows, :], v_s[rows, :]
        parts = []
        for ci, (b, p) in enumerate(chains):
            st = s_scr[b, p]
            m1h, m1l = _split2(st * nkk_t[ci:ci + 1])
            parts += [m1h, m1l, (st * wr_t[ci:ci + 1]).astype(BF16), (diag * v_t[ci:ci + 1]).astype(BF16)]
        res = _dot(jnp.concatenate(parts, axis=0), q_pair)
        w_t, be_t, k_t, br_t, kr_t = w_s[rows, :], be_s[rows, :], k_s[rows, :], br_s[rows, :], kr_s[rows, :]
        y_rows = []
        for ci, (b, p) in enumerate(chains):
            o = ci * 4 * HEAD_B
            sa = res[o:o + HEAD_B] + res[o + HEAD_B:o + 2 * HEAD_B]
            y1 = res[o + 2 * HEAD_B:o + 3 * HEAD_B]
            vb = res[o + 3 * HEAD_B:o + 4 * HEAD_B]
            st = s_scr[b, p]
            yb = y1 + sa * br_t[ci:ci + 1] + vb * kr_t[ci:ci + 1]
            s_scr[b, p] = st * w_t[ci:ci + 1] + sa * be_t[ci:ci + 1] + vb * k_t[ci:ci + 1]
            y_rows.append(jnp.sum(diag * yb, axis=0, keepdims=True))
        y_s[rows, :] = jnp.concatenate(y_rows, axis=0)
        return carry

    lax.fori_loop(0, n_steps, step, 0)

    inv_n = 1.0 / HEAD_B
    for b in range(nb):
        y = jnp.concatenate([y_s[pl.ds(b * n_pairs + p, tc, stride=n_chains), :] for p in range(n_pairs)], axis=1)
        mu = _dot_l2(y, q_heads) * inv_n
        d = y - mu
        var = _dot_l2(d * d, q_heads) * inv_n
        yn = d * lax.rsqrt(var + GROUPNORM_EPS) * lg_ref[...] + lb_ref[...]
        o_ref[b] = ((yn + bo_s[b]) * g_s[b]).astype(BF16)
    sout_ref[...] = s_scr[...]


def _rwkv(z3, shift_prev, s0p, prm, seq_len, tc):
    bt, lp, n_rwkv = z3.shape
    nb = 4
    c_b = prm["w0"].shape[1]
    n_pairs = c_b // LANES
    full = lambda a: pl.BlockSpec(a.shape, lambda g, c: (0,) * a.ndim)
    names = ["mu", "w0", "w2", "a0", "a2", "g2", "kkw", "ka", "rk", "lg", "lb"]
    plist = [prm[n] for n in names]
    assert nb * n_pairs == SUBLANES, "one time-major tile row per (sequence, head pair) chain"
    chunk = lambda: pltpu.VMEM((nb, tc, c_b), F32)
    tiles = lambda: pltpu.VMEM((tc * SUBLANES, LANES), F32)
    return pl.pallas_call(
        functools.partial(_rwkv_body, nb=nb, tc=tc, seq_len=seq_len, c_b=c_b),
        grid=(bt // nb, lp // tc),
        in_specs=[pl.BlockSpec((nb, tc, n_rwkv), lambda g, c: (g, c, 0)),
                  pl.BlockSpec((nb, 1, n_rwkv), lambda g, c: (g, 0, 0)),
                  pl.BlockSpec((nb, n_pairs, HEAD_B, LANES), lambda g, c: (g, 0, 0, 0))]
                 + [full(a) for a in plist],
        out_specs=[pl.BlockSpec((nb, tc, c_b), lambda g, c: (g, c, 0)),
                   pl.BlockSpec((nb, n_pairs, HEAD_B, LANES), lambda g, c: (g, 0, 0, 0))],
        out_shape=[jax.ShapeDtypeStruct((bt, lp, c_b), BF16),
                   jax.ShapeDtypeStruct((bt, n_pairs, HEAD_B, LANES), F32)],
        scratch_shapes=[pltpu.VMEM((nb, n_pairs, HEAD_B, LANES), F32),
                        pltpu.VMEM((nb, 1, n_rwkv), F32)] + [tiles() for _ in range(8)]
                       + [chunk(), chunk(), tiles()],
        compiler_params=_cparams("parallel", "arbitrary"),
        name="rwkv7",
    )(z3, shift_prev, s0p, *plist)


def _s5_prep_body(are_ref, aim_ref, ldt_ref, bre_ref, bim_ref, abre_ref, abim_ref, bbre_ref, bbim_ref):
    a_re = are_ref[...]
    a_im = aim_ref[...]
    step = jnp.exp(ldt_ref[...])
    mag = jnp.exp(step * a_re)
    ab_re = mag * jnp.cos(step * a_im)
    ab_im = mag * jnp.sin(step * a_im)
    den = a_re * a_re + a_im * a_im
    cf_re = ((ab_re - 1.0) * a_re + ab_im * a_im) / den
    cf_im = (ab_im * a_re - (ab_re - 1.0) * a_im) / den
    abre_ref[...] = ab_re
    abim_ref[...] = ab_im
    bbre_ref[...] = cf_re * bre_ref[...] - cf_im * bim_ref[...]
    bbim_ref[...] = cf_re * bim_ref[...] + cf_im * bre_ref[...]


def _s5_prep(a_re, a_im, log_dt, b_re, b_im):
    g, p = a_re.shape
    grp = b_re.shape[2]
    r3 = lambda x: x.reshape(g, 1, p)
    ldt = jnp.broadcast_to(log_dt[:, None, None], (g, 1, p))
    bt = lambda x: jnp.transpose(x, (0, 2, 1))
    vec = jax.ShapeDtypeStruct((g, 1, p), F32)
    mat = jax.ShapeDtypeStruct((g, grp, p), F32)
    return pl.pallas_call(_s5_prep_body, out_shape=[vec, vec, mat, mat], name="s5_prep")(
        r3(a_re), r3(a_im), ldt, bt(b_re), bt(b_im))


def _gelu_tanh(x):
    return 0.5 * x * (1.0 + jnp.tanh(0.7978845608028654 * (x + 0.044715 * (x * x * x))))


def _s5_body(u_ref, re0_ref, im0_ref, bb_ref, ab_ref, cre_ref, cim_ref, d_ref, gw_ref, gb_ref,
             o_ref, reo_ref, imo_ref, xr_scr, xi_scr, u_s, xr_s, xi_s, o_s,
             *, nb, tc, seq_len, precise):
    c = pl.program_id(1)
    n_state = ab_ref.shape[1]

    @pl.when(c == 0)
    def _():
        xr_scr[...] = re0_ref[0]
        xi_scr[...] = im0_ref[0]
        if nb < SUBLANES:
            u_s[...] = jnp.zeros_like(u_s)

    n_lane_tiles = u_s.shape[0]
    for b in range(nb):
        ub = u_ref[b]
        for j in range(n_lane_tiles):
            u_s[j, pl.ds(b, tc, stride=SUBLANES), :] = ub[:, j * LANES:(j + 1) * LANES]
    u = jnp.concatenate([u_s[j] for j in range(n_lane_tiles)], axis=1)
    for half, dst in enumerate((xr_s, xi_s)):
        cols = slice(half * n_state, (half + 1) * n_state)
        if precise:
            uh, ul = _split2(u)
            dst[...] = _dot(uh, bb_ref[0, :, cols]) + _dot(ul, bb_ref[0, :, cols]) + _dot(uh, bb_ref[1, :, cols])
        else:
            dst[...] = _dot(u.astype(BF16), bb_ref[0, :, cols])
    ar = ab_ref[0:1, :]
    ai = ab_ref[1:2, :]

    def step(t, carry):
        xr, xi = carry
        rows = pl.ds(pl.multiple_of(t * SUBLANES, SUBLANES), SUBLANES)
        nr = ar * xr - ai * xi + xr_s[rows, :]
        ni = ar * xi + ai * xr + xi_s[rows, :]
        xr_s[rows, :] = nr
        xi_s[rows, :] = ni
        return nr, ni

    n_steps = jnp.clip(seq_len - c * tc, 0, tc)
    xr, xi = lax.fori_loop(0, n_steps, step, (xr_scr[...], xi_scr[...]))
    xr_scr[...] = xr
    xi_scr[...] = xi
    reo_ref[0] = xr
    imo_ref[0] = xi
    y = (_dot(xr_s[...].astype(BF16), cre_ref[...]) - _dot(xi_s[...].astype(BF16), cim_ref[...])
         + d_ref[...] * u)
    hg = _gelu_tanh(y)
    out = hg * _sigmoid(_dot_x3(hg, gw_ref[...]) + gb_ref[...])
    for j in range(n_lane_tiles):
        o_s[j] = out[:, j * LANES:(j + 1) * LANES]
    for b in range(nb):
        o_ref[b] = jnp.concatenate([o_s[j, pl.ds(b, tc, stride=SUBLANES), :] for j in range(n_lane_tiles)],
                                   axis=1).astype(BF16)


def _s5(u3, re0, im0, prm, seq_len, nb, tc, precise):
    bt, lp, c_c = u3.shape
    n_state = re0.shape[2]
    full = lambda a: pl.BlockSpec(a.shape, lambda g, c: (0,) * a.ndim)
    plist = [prm[n] for n in ("bb", "ab", "cre", "cim", "d", "gw", "gb")]
    st = pl.BlockSpec((1, SUBLANES, n_state), lambda g, c: (g, 0, 0))
    st_shape = jax.ShapeDtypeStruct((bt // nb, SUBLANES, n_state), F32)
    rows = tc * SUBLANES
    return pl.pallas_call(
        functools.partial(_s5_body, nb=nb, tc=tc, seq_len=seq_len, precise=precise),
        grid=(bt // nb, lp // tc),
        in_specs=[pl.BlockSpec((nb, tc, c_c), lambda g, c: (g, c, 0)), st, st] + [full(a) for a in plist],
        out_specs=[pl.BlockSpec((nb, tc, c_c), lambda g, c: (g, c, 0)), st, st],
        out_shape=[jax.ShapeDtypeStruct((bt, lp, c_c), BF16), st_shape, st_shape],
        scratch_shapes=[pltpu.VMEM((SUBLANES, n_state), F32), pltpu.VMEM((SUBLANES, n_state), F32),
                        pltpu.VMEM((c_c // LANES, rows, LANES), F32), pltpu.VMEM((rows, n_state), F32),
                        pltpu.VMEM((rows, n_state), F32), pltpu.VMEM((c_c // LANES, rows, LANES), F32)],
        compiler_params=_cparams("parallel", "arbitrary"),
        name="s5",
    )(u3, re0, im0, *plist)


def _merge_body(h_ref, oa_ref, ob_ref, oc_ref, g_ref, wg_ref, pa_ref, pb_ref, pc_ref, wo_ref, o_ref, *, d):
    h = h_ref[...]
    xn = _rms(h, g_ref[...]).astype(BF16)
    merged = None
    for n, (br_ref, pj_ref) in enumerate(((oa_ref, pa_ref), (ob_ref, pb_ref), (oc_ref, pc_ref))):
        gate = _sigmoid(_dot(xn, wg_ref[:, n * d:(n + 1) * d]))
        term = gate * _dot(br_ref[...], pj_ref[...])
        merged = term if merged is None else merged + term
    o_ref[...] = h + _dot(merged.astype(BF16), wo_ref[...])


def _merge(h2, oa, ob, oc, g1, wg, pa, pb, pc, wo):
    rows, d = h2.shape
    tm = _row_tile(rows, 512)
    row = lambda a: pl.BlockSpec((tm, a.shape[1]), lambda i: (i, 0))
    full = lambda a: pl.BlockSpec(a.shape, lambda i: (0,) * a.ndim)
    return pl.pallas_call(
        functools.partial(_merge_body, d=d),
        grid=(rows // tm,),
        in_specs=[row(h2), row(oa), row(ob), row(oc)] + [full(a) for a in (g1, wg, pa, pb, pc, wo)],
        out_specs=row(h2),
        out_shape=jax.ShapeDtypeStruct(h2.shape, F32),
        compiler_params=_cparams("parallel"),
        name="merge_outproj",
    )(h2, oa, ob, oc, g1, wg, pa, pb, pc, wo)


PREV_ROWS = 16


def _ffn_body(*refs, tm, seq_rows, decode):
    if decode:
        (x_ref, pa_ref, pb_ref, g_ref, ua_ref, ub_ref, cwa_ref, cwb_ref, cba_ref, cbb_ref, dn_ref,
         o_ref, xn_scr) = refs
    else:
        (x_ref, xp_ref, g_ref, ua_ref, ub_ref, cwa_ref, cwb_ref, cba_ref, cbb_ref, dn_ref,
         o_ref, xn_scr, xnp_scr) = refs
    i = pl.program_id(0)
    j = pl.program_id(1)

    @pl.when(j == 0)
    def _():
        xn_scr[...] = _rms(x_ref[...], g_ref[...]).astype(BF16)
        if not decode:
            xnp_scr[...] = _rms(xp_ref[...], g_ref[...]).astype(BF16)

    xn = xn_scr[...]
    nc = ua_ref.shape[1]
    rowid = _iota((tm, nc), 0)

    def half(up_ref, cw_ref, cb_ref, prev_ref):
        hu = _dot(xn, up_ref[...])
        r1 = pltpu.roll(hu, 1, 0)
        r2 = pltpu.roll(hu, 2, 0)
        if decode:
            prev = prev_ref[...]
            pos = rowid % seq_rows
            sh1 = jnp.where(pos == 0, pltpu.roll(prev, tm - 1, 0), r1)
            sh2 = jnp.where(pos < 2, prev, r2)
        else:
            hp = _dot(xnp_scr[...], up_ref[...])
            last = hp[PREV_ROWS - 1:PREV_ROWS]
            last2 = hp[PREV_ROWS - 2:PREV_ROWS - 1]
            pos = (i * tm + rowid) % seq_rows
            r1 = jnp.where(rowid == 0, last, r1)
            r2 = jnp.where(rowid == 0, last2, jnp.where(rowid == 1, last, r2))
            sh1 = jnp.where(pos == 0, 0.0, r1)
            sh2 = jnp.where(pos < 2, 0.0, r2)
        return cw_ref[0:1, :] * sh2 + cw_ref[1:2, :] * sh1 + cw_ref[2:3, :] * hu + cb_ref[...]

    a = half(ua_ref, cwa_ref, cba_ref, pa_ref if decode else None)
    bgate = half(ub_ref, cwb_ref, cbb_ref, pb_ref if decode else None)
    act = (a * _sigmoid(a)) * bgate
    contrib = _dot(act.astype(BF16), dn_ref[...])

    @pl.when(j == 0)
    def _():
        o_ref[...] = x_ref[...] + contrib

    @pl.when(j > 0)
    def _():
        o_ref[...] += contrib


def _ffn(h2, g2, ua, ub, cwa, cwb, cba, cbb, dn, seq_rows, prev_pad=None):
    rows, d = h2.shape
    d_ff = ua.shape[1]
    nc = 256
    n_chunks = d_ff // nc
    decode = prev_pad is not None
    tm = _row_tile(rows, 1024)
    xrow = pl.BlockSpec((tm, d), lambda i, j: (i, 0))
    col = lambda r: pl.BlockSpec((r, nc), lambda i, j: (0, j))
    w_specs = [pl.BlockSpec((1, d), lambda i, j: (0, 0)), col(d), col(d), col(CONV_TAPS), col(CONV_TAPS),
               col(1), col(1), pl.BlockSpec((nc, d), lambda i, j: (j, 0))]
    scratch = [pltpu.VMEM((tm, d), BF16)]
    if decode:
        in_specs = [xrow, pl.BlockSpec((tm, nc), lambda i, j: (i, j)),
                    pl.BlockSpec((tm, nc), lambda i, j: (i, n_chunks + j))] + w_specs
        args = (h2, prev_pad, prev_pad, g2, ua, ub, cwa, cwb, cba, cbb, dn)
    else:
        per = tm // PREV_ROWS
        in_specs = [xrow, pl.BlockSpec((PREV_ROWS, d), lambda i, j: (jnp.maximum(i * per - 1, 0), 0))] + w_specs
        args = (h2, h2, g2, ua, ub, cwa, cwb, cba, cbb, dn)
        scratch.append(pltpu.VMEM((PREV_ROWS, d), BF16))
    return pl.pallas_call(
        functools.partial(_ffn_body, tm=tm, seq_rows=seq_rows, decode=decode),
        grid=(rows // tm, n_chunks),
        in_specs=in_specs,
        out_specs=xrow,
        out_shape=jax.ShapeDtypeStruct(h2.shape, F32),
        scratch_shapes=scratch,
        compiler_params=_cparams("parallel", "arbitrary"),
        name="conv_ffn",
    )(*args)


def _up_rows_body(x_ref, g_ref, up_ref, o_ref):
    o_ref[...] = _dot(_rms(x_ref[...], g_ref[...]).astype(BF16), up_ref[...])


def _up_rows(x2, g2, up):
    rows, d = x2.shape
    n = up.shape[1]
    nc = 512
    return pl.pallas_call(
        _up_rows_body,
        grid=(n // nc,),
        in_specs=[pl.BlockSpec((rows, d), lambda j: (0, 0)), pl.BlockSpec((1, d), lambda j: (0, 0)),
                  pl.BlockSpec((d, nc), lambda j: (0, j))],
        out_specs=pl.BlockSpec((rows, nc), lambda j: (0, j)),
        out_shape=jax.ShapeDtypeStruct((rows, n), F32),
        compiler_params=_cparams("parallel"),
        name="conv_state_rows",
    )(x2, g2, up)


def _final_norm_body(x_ref, g_ref, o_ref):
    o_ref[...] = _rms(x_ref[...], g_ref[...])


def _final_norm(h2, g):
    rows, d = h2.shape
    tm = _row_tile(rows, 1024)
    spec = pl.BlockSpec((tm, d), lambda i: (i, 0))
    return pl.pallas_call(
        _final_norm_body,
        grid=(rows // tm,),
        in_specs=[spec, pl.BlockSpec((1, d), lambda i: (0, 0))],
        out_specs=spec,
        out_shape=jax.ShapeDtypeStruct(h2.shape, F32),
        compiler_params=_cparams("parallel"),
        name="final_norm",
    )(h2, g)


def _pack_state(s):
    b, h, v, k = s.shape
    return s.reshape(b, h // 2, 2, v, k).transpose(0, 1, 3, 2, 4).reshape(b, h // 2, v, 2 * k)


def _unpack_state(s):
    b, hp, v, k2 = s.shape
    return s.reshape(b, hp, v, 2, k2 // 2).transpose(0, 1, 3, 2, 4).reshape(b, 2 * hp, v, k2 // 2)


def _block_diag(x):
    g, a, c = x.shape
    eye = jnp.eye(g, dtype=bool)[:, None, :, None]
    return jnp.where(eye, x[:, :, None, :], 0.0).reshape(g * a, g * c)


def _pad_rows(x, n):
    return jnp.concatenate([x, jnp.zeros((n - x.shape[0],) + x.shape[1:], x.dtype)], axis=0)


def kernel(x_prompt, x_sample, cache_k, cache_v, cache_logf, page_table, state_rwkv_wkv, state_rwkv_shift, state_ssm_re, state_ssm_im, state_ffn_conv, meta_tokens, norm1_g, w_in, fox_bf, rwkv_mu, rwkv_w0, rwkv_w2, rwkv_a0, rwkv_a2, rwkv_g2, rwkv_kk, rwkv_ka, rwkv_rk, rwkv_lnx_g, rwkv_lnx_b, ssm_a_re, ssm_a_im, ssm_log_dt, ssm_b_re, ssm_b_im, ssm_c_re, ssm_c_im, ssm_d, ssm_glu_w, ssm_glu_b, proj_a, proj_b, proj_c, w_out, norm2_g, ffn_up, ffn_conv_w, ffn_conv_b, ffn_down, final_norm_g):
    bsz, seq, d = x_prompt.shape
    db, t_new, _ = x_sample.shape
    depth = norm1_g.shape[0]
    n_ha = fox_bf.shape[1]
    c_a = n_ha * HEAD_A
    n_rwkv = rwkv_mu.shape[1]
    c_b = rwkv_w0.shape[1]
    n_hb = c_b // HEAD_B
    g_c, p_c = ssm_a_re.shape[1], ssm_a_re.shape[2]
    c_c = g_c * SSM_GRP
    n_state = g_c * p_c
    d_ff = ffn_down.shape[1]
    n_pool, page = cache_k.shape[1], cache_k.shape[2]
    seq_len = N_META_TOK + seq
    lp = -(-seq_len // 256) * 256
    seq_tile = 256

    meta = jnp.broadcast_to(meta_tokens[None], (bsz, N_META_TOK, d))
    h_p = jnp.concatenate([meta, x_prompt, jnp.zeros((bsz, lp - seq_len, d), F32)], axis=1)
    h_p = h_p.reshape(bsz * lp, d)
    h_s = x_sample.reshape(db * t_new, d)

    cache_k4 = cache_k.reshape(depth, n_pool, page, c_a)
    cache_v4 = cache_v.reshape(depth, n_pool, page, c_a)
    eye_h = jnp.eye(n_ha, dtype=bool)

    zeros_shift = jnp.zeros((bsz, 1, n_rwkv), F32)
    zeros_wkv = jnp.zeros((bsz, n_hb // 2, HEAD_B, 2 * HEAD_B), F32)
    assert bsz <= SUBLANES and db % SUBLANES == 0
    zeros_ssm = jnp.zeros((1, SUBLANES, n_state), F32)

    outs_p, outs_s = [], []
    for l in range(depth):
        w = w_in[l]
        o_f = 3 * c_a
        o_z = o_f + n_ha
        o_g = o_z + n_rwkv + c_c
        wp = jnp.concatenate([w[:, :o_f], w[:, o_f:o_z], jnp.zeros((d, LANES - n_ha), F32),
                              w[:, o_z:o_g]], axis=1).astype(BF16)
        wg = w[:, o_g:].astype(BF16)
        bfp = jnp.concatenate([fox_bf[l], jnp.zeros((LANES - n_ha,), F32)])[None]
        g1 = norm1_g[l][None]
        g2 = norm2_g[l][None]
        row1 = lambda x: x.reshape(1, -1)
        lora_pad = lambda x, o: jnp.zeros((n_rwkv - 3 * c_b, c_b), F32).at[o:o + x.shape[0]].set(x)
        rw = dict(mu=row1(rwkv_mu[l]), w0=row1(rwkv_w0[l]), w2=lora_pad(rwkv_w2[l], 0),
                  a0=row1(rwkv_a0[l]), a2=lora_pad(rwkv_a2[l], LORA_W_DIM),
                  g2=lora_pad(rwkv_g2[l], LORA_W_DIM + LORA_A_DIM), kkw=row1(rwkv_kk[l]),
                  ka=row1(rwkv_ka[l]), rk=row1(rwkv_rk[l]), lg=row1(rwkv_lnx_g[l]), lb=row1(rwkv_lnx_b[l]))
        ab_re, ab_im, bbt_re, bbt_im = _s5_prep(ssm_a_re[l], ssm_a_im[l], ssm_log_dt[l], ssm_b_re[l], ssm_b_im[l])
        bb = jnp.concatenate([_block_diag(bbt_re), _block_diag(bbt_im)], axis=1)
        bb_hi = bb.astype(BF16)
        bb_lo = (bb - bb_hi.astype(F32)).astype(BF16)
        s5p = dict(bb=jnp.stack([bb_hi, bb_lo]),
                   ab=jnp.concatenate([ab_re.reshape(1, n_state), ab_im.reshape(1, n_state)], axis=0),
                   cre=_block_diag(jnp.transpose(ssm_c_re[l], (0, 2, 1))).astype(BF16),
                   cim=_block_diag(jnp.transpose(ssm_c_im[l], (0, 2, 1))).astype(BF16),
                   d=row1(ssm_d[l]), gw=ssm_glu_w[l], gb=row1(ssm_glu_b[l]))
        pa, pb, pc, wo = (x[l].astype(BF16) for x in (proj_a, proj_b, proj_c, w_out))
        up = ffn_up[l].astype(BF16)
        ua, ub = up[:, :d_ff], up[:, d_ff:]
        cwa, cwb = ffn_conv_w[l][:, :d_ff], ffn_conv_w[l][:, d_ff:]
        cba, cbb = ffn_conv_b[l][None, :d_ff], ffn_conv_b[l][None, d_ff:]
        dn = ffn_down[l].astype(BF16)

        qb, k, v, kb, vb, lf, z, u = _inproj(h_p, g1, wp, bfp, c_a, n_rwkv, c_c)
        r3 = lambda x: x.reshape(bsz, lp, x.shape[1])
        c, ct = _cumsum(r3(lf), n_ha // 2)
        o_a = _attn_prompt(r3(qb), r3(kb), r3(vb), c, ct)
        o_b, wkv = _rwkv(r3(z), zeros_shift, zeros_wkv, rw, seq_len, seq_tile)
        o_c, re_n, im_n = _s5(r3(u), zeros_ssm, zeros_ssm, s5p, seq_len, bsz, seq_tile // 2, False)
        flat = lambda x: x.reshape(bsz * lp, x.shape[2])
        h_mid = _merge(h_p, flat(o_a), flat(o_b), flat(o_c), g1, wg, pa, pb, pc, wo)
        h_p = _ffn(h_mid, g2, ua, ub, cwa, cwb, cba, cbb, dn, lp)
        tail = r3(h_mid)[:, seq_len - 2:seq_len].reshape(bsz * 2, d)
        conv = _up_rows(_pad_rows(tail, 16), g2, up)[:bsz * 2].reshape(bsz, 2, 2 * d_ff)
        outs_p.append((r3(k)[:, :seq_len].reshape(bsz, seq_len, n_ha, HEAD_A),
                       r3(v)[:, :seq_len].reshape(bsz, seq_len, n_ha, HEAD_A),
                       r3(lf)[:, :seq_len, :n_ha], _unpack_state(wkv), r3(z)[:, seq_len - 1],
                       re_n[0, :bsz].reshape(bsz, g_c, p_c), im_n[0, :bsz].reshape(bsz, g_c, p_c), conv))

        qb, k, v, kb, vb, lf, z, u = _inproj(h_s, g1, wp, bfp, c_a, n_rwkv, c_c)
        s3 = lambda x: x.reshape(db, t_new, x.shape[1])
        q4 = s3(qb).reshape(db, t_new, n_ha, HEAD_A)
        qbd = jnp.where(eye_h[None, None, :, :, None], q4[:, :, :, None, :], jnp.zeros((), BF16))
        qbd = qbd.reshape(db, t_new * n_ha, c_a)
        pad_new = lambda x: jnp.concatenate([s3(x), jnp.zeros((db, LANES - t_new, c_a), BF16)], axis=1)
        ft = jnp.transpose(cache_logf[l], (0, 2, 1)).reshape(n_pool * n_ha, page)
        suf, tot = _pool_suffix(ft)
        o_a = _attn_sample(page_table, l, qbd, pad_new(kb), pad_new(vb), s3(lf), cache_k4, cache_v4,
                           suf.reshape(n_pool, n_ha, page), tot.reshape(n_pool, n_ha, page))
        o_b, wkv = _rwkv(s3(z), state_rwkv_shift[l][:, None], _pack_state(state_rwkv_wkv[l]), rw, t_new, t_new)
        o_c, re_n, im_n = _s5(s3(u), state_ssm_re[l].reshape(db // SUBLANES, SUBLANES, n_state),
                              state_ssm_im[l].reshape(db // SUBLANES, SUBLANES, n_state), s5p, t_new,
                              SUBLANES, t_new, True)
        flat = lambda x: x.reshape(db * t_new, x.shape[2])
        h_mid = _merge(h_s, flat(o_a), flat(o_b), flat(o_c), g1, wg, pa, pb, pc, wo)
        prev = state_ffn_conv[l]
        prev_pad = jnp.concatenate([prev, jnp.zeros((db, t_new - prev.shape[1], prev.shape[2]), F32)], axis=1)
        h_s = _ffn(h_mid, g2, ua, ub, cwa, cwb, cba, cbb, dn, t_new, prev_pad.reshape(db * t_new, 2 * d_ff))
        tail = s3(h_mid)[:, t_new - 2:].reshape(db * 2, d)
        conv = _up_rows(tail, g2, up).reshape(db, 2, 2 * d_ff)
        outs_s.append((s3(k).reshape(db, t_new, n_ha, HEAD_A), s3(v).reshape(db, t_new, n_ha, HEAD_A),
                       s3(lf)[:, :, :n_ha], _unpack_state(wkv), s3(z)[:, t_new - 1],
                       re_n.reshape(db, g_c, p_c), im_n.reshape(db, g_c, p_c), conv))

    fg = final_norm_g[None]
    y_p = _final_norm(h_p, fg).reshape(bsz, lp, d)[:, N_META_TOK:seq_len]
    y_s = _final_norm(h_s, fg).reshape(db, t_new, d)
    stack = lambda outs: [jnp.stack(t) for t in zip(*outs)]
    return tuple([y_p, y_s] + stack(outs_p) + stack(outs_s))
```

```python
import functools

import jax
import jax.numpy as jnp
from jax import lax
from jax.experimental import pallas as pl
from jax.experimental.pallas import tpu as pltpu

F32 = jnp.float32
BF16 = jnp.bfloat16

N_META_TOK = 16
RMS_EPS = 1e-6
HEAD_A = 64
HEAD_B = 64
LORA_W_DIM = 64
LORA_A_DIM = 64
LORA_G_DIM = 128
GROUPNORM_EPS = 64e-5
SSM_GRP = 16
SSM_P = 64
CONV_TAPS = 3
LANES = 128
SUBLANES = 8
VMEM_LIMIT = 52 * 1024 * 1024


def _cparams(*sem):
    return pltpu.CompilerParams(dimension_semantics=sem, vmem_limit_bytes=VMEM_LIMIT)


def _dot(a, b):
    return jnp.dot(a, b, preferred_element_type=F32)


def _dot_nt(a, b):
    return lax.dot_general(a, b, (((1,), (1,)), ((), ())), preferred_element_type=F32)


def _split2(x):
    hi = x.astype(BF16)
    lo = (x - hi.astype(F32)).astype(BF16)
    return hi, lo


def _split3(x):
    hi = x.astype(BF16)
    r = x - hi.astype(F32)
    mid = r.astype(BF16)
    lo = (r - mid.astype(F32)).astype(BF16)
    return hi, mid, lo


def _dot_l2(x, w_bf):
    hi, lo = _split2(x)
    return _dot(hi, w_bf) + _dot(lo, w_bf)


def _dot_l3(x, w_bf):
    hi, mid, lo = _split3(x)
    return _dot(hi, w_bf) + _dot(mid, w_bf) + _dot(lo, w_bf)


def _dot_r3(w_bf, x):
    hi, mid, lo = _split3(x)
    return _dot(w_bf, hi) + _dot(w_bf, mid) + _dot(w_bf, lo)


def _dot_x3(a, b):
    ah, al = _split2(a)
    bh, bl = _split2(b)
    return _dot(ah, bh) + _dot(al, bh) + _dot(ah, bl)


def _sigmoid(x):
    return 1.0 / (1.0 + jnp.exp(-x))


def _log_sigmoid(x):
    return jnp.minimum(x, 0.0) - jnp.log1p(jnp.exp(-jnp.abs(x)))


def _rms(x, g):
    ms = jnp.mean(x * x, axis=-1, keepdims=True)
    return x * lax.rsqrt(ms + RMS_EPS) * g


def _iota(shape, dim):
    return lax.broadcasted_iota(jnp.int32, shape, dim)


def _block_ones(n, blk):
    return jnp.where(_iota((n, n), 0) // blk == _iota((n, n), 1) // blk, 1.0, 0.0).astype(BF16)


def _row_tile(rows, cap):
    t = cap
    while rows % t:
        t //= 2
    return t


def _inproj_body(h_ref, g_ref, w_ref, bf_ref, qb_ref, k_ref, v_ref, kb_ref, vb_ref, lf_ref, z_ref,
                 u_ref, *, c_a, n_rwkv, c_c):
    xn = _rms(h_ref[...], g_ref[...]).astype(BF16)
    q = _dot(xn, w_ref[:, 0:c_a])
    qb_ref[...] = (q * (HEAD_A ** -0.5)).astype(BF16)
    k = _dot(xn, w_ref[:, c_a:2 * c_a])
    k_ref[...] = k
    kb_ref[...] = k.astype(BF16)
    v = _dot(xn, w_ref[:, 2 * c_a:3 * c_a])
    v_ref[...] = v
    vb_ref[...] = v.astype(BF16)
    o = 3 * c_a
    f = _dot(xn, w_ref[:, o:o + LANES]) + bf_ref[...]
    lf_ref[...] = _log_sigmoid(f)
    o += LANES
    z_ref[...] = _dot(xn, w_ref[:, o:o + n_rwkv])
    o += n_rwkv
    u_ref[...] = _dot(xn, w_ref[:, o:o + c_c])


def _inproj(h2, g1, wp, bfp, c_a, n_rwkv, c_c):
    rows, d = h2.shape
    tm = _row_tile(rows, 512)
    npj = wp.shape[1]
    row = lambda n: pl.BlockSpec((tm, n), lambda i: (i, 0))
    full = lambda a: pl.BlockSpec(a.shape, lambda i: (0,) * a.ndim)
    outs = [(c_a, BF16), (c_a, F32), (c_a, F32), (c_a, BF16), (c_a, BF16), (LANES, F32),
            (n_rwkv, F32), (c_c, F32)]
    return pl.pallas_call(
        functools.partial(_inproj_body, c_a=c_a, n_rwkv=n_rwkv, c_c=c_c),
        grid=(rows // tm,),
        in_specs=[row(d), full(g1), full(wp), full(bfp)],
        out_specs=[row(n) for n, _ in outs],
        out_shape=[jax.ShapeDtypeStruct((rows, n), dt) for n, dt in outs],
        compiler_params=_cparams("parallel"),
        name="inproj",
    )(h2, g1, wp, bfp)


N_BIAS = 3


def _fox_prep_body(lf_ref, q_ref, k_ref, v_ref, qx_ref, kx_ref, vx_ref, carry_ref, *, tc, n_heads, n_valid):
    @pl.when(pl.program_id(1) == 0)
    def _():
        carry_ref[...] = jnp.zeros_like(carry_ref)

    @pl.when(pl.program_id(1) >= n_valid)
    def _():
        for ref in (qx_ref, kx_ref, vx_ref):
            ref[...] = jnp.zeros_like(ref)

    @pl.when(pl.program_id(1) < n_valid)
    def _():
        _fox_prep_tile(lf_ref, q_ref, k_ref, v_ref, qx_ref, kx_ref, vx_ref, carry_ref, tc, n_heads)


def _fox_prep_tile(lf_ref, q_ref, k_ref, v_ref, qx_ref, kx_ref, vx_ref, carry_ref, tc, n_heads):
    tri = jnp.where(_iota((tc, tc), 0) >= _iota((tc, tc), 1), 1.0, 0.0).astype(BF16)
    cs = _dot_r3(tri, lf_ref[0]) + carry_ref[...]
    carry_ref[...] = cs[tc - 1:tc]
    pieces = jnp.concatenate(_split3(cs), axis=1)

    nl = n_heads * LANES
    src = _iota((N_BIAS * LANES, nl), 0)
    dst = _iota((N_BIAS * LANES, nl), 1)
    base = jnp.where((dst // LANES) % 2 == 0, HEAD_A, 0)
    same_head = src % LANES == dst // LANES
    off = dst % LANES - base
    place_q = jnp.where(same_head & (off == src // LANES), 1.0, 0.0).astype(BF16)
    place_k = jnp.where(same_head & (off == N_BIAS + src // LANES), -1.0, 0.0).astype(BF16)
    off1 = off[0:1]
    qe = _dot(pieces, place_q) + jnp.where((off1 >= N_BIAS) & (off1 < 2 * N_BIAS), 1.0, 0.0)
    ke = _dot(pieces, place_k) + jnp.where((off1 >= 0) & (off1 < N_BIAS), 1.0, 0.0)
    lane_half = _iota((tc, LANES), 1) // HEAD_A
    ones = jnp.ones((tc, LANES), BF16)
    for h in range(n_heads):
        pair = slice((h // 2) * LANES, (h // 2 + 1) * LANES)
        tile = slice(h * LANES, (h + 1) * LANES)
        own = lane_half == h % 2
        qx_ref[0, h] = jnp.where(own, q_ref[0][:, pair], qe[:, tile].astype(BF16))
        kx_ref[0, h] = jnp.where(own, k_ref[0][:, pair], ke[:, tile].astype(BF16))
        vx_ref[0, h] = jnp.where(own, v_ref[0][:, pair], ones)


ATTN_TQ = 256
ATTN_TK = 1024


def _fox_prep(lf3, qb, kb, vb):
    b, lp, c_a = qb.shape
    n_heads = c_a // HEAD_A
    tc = 256
    n_valid = lp // tc
    lkv = -(-lp // ATTN_TK) * ATTN_TK
    row = lambda n: pl.BlockSpec((1, tc, n), lambda i, j: (i, jnp.minimum(j, n_valid - 1), 0))
    out = pl.BlockSpec((1, n_heads, tc, LANES), lambda i, j: (i, 0, j, 0))
    return pl.pallas_call(
        functools.partial(_fox_prep_body, tc=tc, n_heads=n_heads, n_valid=n_valid),
        grid=(b, lkv // tc),
        in_specs=[row(LANES), row(c_a), row(c_a), row(c_a)],
        out_specs=[out, out, out],
        out_shape=[jax.ShapeDtypeStruct((b, n_heads, lkv, LANES), BF16)] * 3,
        scratch_shapes=[pltpu.VMEM((1, LANES), F32)],
        compiler_params=_cparams("parallel", "arbitrary"),
        name="fox_prep",
    )(lf3, qb, kb, vb)


def _attn_body(qx_ref, kx_ref, vx_ref, o_ref, s_scr, *, tq, tk):
    row0 = pl.program_id(2) * tq
    n_full = row0 // tk
    visible = _iota((tq, tk), 1) <= _iota((tq, tk), 0) + (row0 - n_full * tk)
    q = [qx_ref[0, hh] for hh in range(2)]

    def lane_tile_max(s):
        m = s[:, 0:LANES]
        for t in range(1, tk // LANES):
            m = jnp.maximum(m, s[:, t * LANES:(t + 1) * LANES])
        return m

    def pass_max(j, ms, masked):
        start = pl.multiple_of(j * tk, tk)
        out = []
        for hh in range(2):
            s = _dot_nt(q[hh], kx_ref[0, hh, pl.ds(start, tk), :])
            if masked:
                s = jnp.where(visible, s, -jnp.inf)
            s_scr[hh, :, pl.ds(start, tk)] = s
            out.append(jnp.maximum(ms[hh], lane_tile_max(s)))
        return tuple(out)

    ms = tuple(jnp.full((tq, LANES), -jnp.inf, F32) for _ in range(2))
    ms = lax.fori_loop(0, n_full, lambda j, c: pass_max(j, c, False), ms)
    ms = pass_max(n_full, ms, True)
    mb = []
    for hh in range(2):
        m1 = jnp.broadcast_to(jnp.max(ms[hh], axis=1, keepdims=True), (tq, LANES))
        mb.append(jnp.concatenate([m1] * (tk // LANES), axis=1))

    def pass_acc(j, accs):
        start = pl.multiple_of(j * tk, tk)
        out = []
        for hh in range(2):
            p = jnp.exp(s_scr[hh, :, pl.ds(start, tk)] - mb[hh]).astype(BF16)
            out.append(accs[hh] + _dot(p, vx_ref[0, hh, pl.ds(start, tk), :]))
        return tuple(out)

    accs = tuple(jnp.zeros((tq, LANES), F32) for _ in range(2))
    accs = lax.fori_loop(0, n_full + 1, pass_acc, accs)
    outs = [acc / pltpu.roll(acc, HEAD_A, 1) for acc in accs]
    o_ref[0] = jnp.where(_iota((tq, LANES), 1) < HEAD_A, outs[0], outs[1]).astype(BF16)


def _attn_prompt(qx, kx, vx, lp):
    b, n_heads, lkv, _ = qx.shape
    tq, tk = ATTN_TQ, ATTN_TK
    kv = pl.BlockSpec((1, 2, lkv, LANES), lambda bi, p, i: (bi, p, 0, 0))
    return pl.pallas_call(
        functools.partial(_attn_body, tq=tq, tk=tk),
        grid=(b, n_heads // 2, lp // tq),
        in_specs=[pl.BlockSpec((1, 2, tq, LANES), lambda bi, p, i: (bi, p, i, 0)), kv, kv],
        out_specs=pl.BlockSpec((1, tq, LANES), lambda bi, p, i: (bi, i, p)),
        out_shape=jax.ShapeDtypeStruct((b, lp, n_heads * HEAD_A), BF16),
        scratch_shapes=[pltpu.VMEM((2, tq, lkv), F32)],
        compiler_params=_cparams("parallel", "parallel", "arbitrary"),
        name="fox_prompt_attn",
    )(qx, kx, vx)


def _pool_suffix_body(x_ref, suf_ref, tot_ref, *, n_h):
    n = x_ref.shape[1]
    src = _iota((n, n), 0)
    dst = _iota((n, n), 1)
    same_head = src % n_h == dst % n_h
    later = jnp.where(same_head & (src // n_h > dst // n_h), 1.0, 0.0).astype(BF16)
    whole = jnp.where(same_head, 1.0, 0.0).astype(BF16)
    hi, mid, lo = _split3(x_ref[...])
    suf_ref[...] = _dot(hi, later) + _dot(mid, later) + _dot(lo, later)
    tot_ref[...] = _dot(hi, whole) + _dot(mid, whole) + _dot(lo, whole)


def _pool_suffix(f2, n_h):
    rows, n = f2.shape
    tm = _row_tile(rows, 512)
    spec = pl.BlockSpec((tm, n), lambda i: (i, 0))
    return pl.pallas_call(
        functools.partial(_pool_suffix_body, n_h=n_h),
        grid=(rows // tm,),
        in_specs=[spec],
        out_specs=[spec, spec],
        out_shape=[jax.ShapeDtypeStruct(f2.shape, F32)] * 2,
        compiler_params=_cparams("parallel"),
        name="fox_pool_suffix",
    )(f2)


def _attn_sample_body(pt_ref, q_ref, kn_ref, vn_ref, lfc_ref, lfr_ref, *rest, n_pages, n_h):
    kp = rest[0:n_pages]
    vp = rest[n_pages:2 * n_pages]
    sufp = rest[2 * n_pages:3 * n_pages]
    totp = rest[3 * n_pages:4 * n_pages]
    o_ref = rest[4 * n_pages]
    q = q_ref[0]
    nr, hd = q.shape
    t_new = nr // n_h
    n_keys = kp[0].shape[0] * n_h

    lfc = lfc_ref[0]
    parts = [lfc[0:n_h]]
    for t in range(1, t_new):
        parts.append(parts[-1] + lfc[t * n_h:(t + 1) * n_h])
    cq = jnp.concatenate(parts, axis=0)
    cs_row = lfr_ref[0]
    lane_r = _iota(cs_row.shape, 1)
    shift = n_h
    while shift < nr:
        cs_row = cs_row + jnp.where(lane_r >= shift, pltpu.roll(cs_row, shift, 1), 0.0)
        shift *= 2
    cs_row = cs_row[0:1]

    own_head = _iota((nr, n_keys), 1) % n_h == _iota((nr, n_keys), 0) % n_h
    cq_wide = jnp.concatenate([cq] * (n_keys // LANES), axis=1)
    tiles = [None] * n_pages
    carry = jnp.zeros((1, n_keys), F32)
    for j in reversed(range(n_pages)):
        k2 = kp[j][...].reshape(n_keys, hd).astype(BF16)
        s = _dot_nt(q, k2) + cq_wide + (sufp[j][...] + carry)
        tiles[j] = jnp.where(own_head, s, -jnp.inf)
        carry = carry + totp[j][...]
    row_n = _iota((nr, nr), 0)
    lane_n = _iota((nr, nr), 1)
    visible = (lane_n % n_h == row_n % n_h) & (lane_n // n_h <= row_n // n_h)
    s_new = _dot_nt(q, kn_ref[0]) + cq[:, 0:nr] - cs_row[:, 0:nr]
    s_new = jnp.where(visible, s_new, -jnp.inf)

    m = tiles[0]
    for tl in tiles[1:]:
        m = jnp.maximum(m, tl)
    m = jnp.maximum(jnp.max(m, axis=1, keepdims=True), jnp.max(s_new, axis=1, keepdims=True))
    p_new = jnp.exp(s_new - m)
    acc = _dot(p_new.astype(BF16), vn_ref[0])
    lsum = jnp.zeros((nr, n_keys), F32)
    for j in range(n_pages):
        p = jnp.exp(tiles[j] - m)
        lsum = lsum + p
        acc = acc + _dot(p.astype(BF16), vp[j][...].reshape(n_keys, hd).astype(BF16))
    l = jnp.sum(lsum, axis=1, keepdims=True) + jnp.sum(p_new, axis=1, keepdims=True)
    o_ref[0] = (acc / l).astype(BF16)


def _attn_sample(page_table, layer, q2, kn2, vn2, lf_col, lf_row, cache_k, cache_v, suf, tot):
    db, n_pages = page_table.shape
    nr, hd = q2.shape[1], q2.shape[2]
    page, n_h = cache_k.shape[2], cache_k.shape[3]

    def seq_spec(a):
        return pl.BlockSpec((1,) + a.shape[1:], lambda b, pt: (b,) + (0,) * (a.ndim - 1))

    def pool_spec(j):
        return pl.BlockSpec((None, None, page, n_h, hd), lambda b, pt, j=j: (layer, pt[b, j], 0, 0, 0))

    def small_spec(j):
        return pl.BlockSpec((None, 1, page * n_h), lambda b, pt, j=j: (pt[b, j], 0, 0))

    in_specs = ([seq_spec(a) for a in (q2, kn2, vn2, lf_col, lf_row)]
                + [pool_spec(j) for j in range(n_pages)] * 2
                + [small_spec(j) for j in range(n_pages)] * 2)
    grid_spec = pltpu.PrefetchScalarGridSpec(
        num_scalar_prefetch=1,
        grid=(db,),
        in_specs=in_specs,
        out_specs=pl.BlockSpec((1, nr, hd), lambda b, pt: (b, 0, 0)),
    )
    args = ([q2, kn2, vn2, lf_col, lf_row] + [cache_k] * n_pages + [cache_v] * n_pages
            + [suf] * n_pages + [tot] * n_pages)
    return pl.pallas_call(
        functools.partial(_attn_sample_body, n_pages=n_pages, n_h=n_h),
        grid_spec=grid_spec,
        out_shape=jax.ShapeDtypeStruct((db, nr, hd), BF16),
        compiler_params=_cparams("parallel"),
        name="fox_paged_attn",
    )(page_table, *args)


def _rwkv_body(z_ref, sp_ref, s0_ref, mu_ref, w0_ref, w2_ref, a0_ref, a2_ref, g2_ref, kkw_ref, ka_ref,
               rk_ref, lg_ref, lb_ref, o_ref, sout_ref,
               s_scr, carry_scr, nkk_s, w_s, be_s, k_s, wr_s, v_s, br_s, kr_s, g_s, bo_s, y_s,
               *, nb, tc, seq_len, c_b):
    c = pl.program_id(1)
    n_pairs = c_b // LANES
    n_chains = nb * n_pairs

    @pl.when(c == 0)
    def _():
        s_scr[...] = s0_ref[...]
        carry_scr[...] = sp_ref[...]

    q_heads = _block_ones(c_b, HEAD_B)
    rowid = _iota((tc, z_ref.shape[2]), 0)
    lora0 = 3 * c_b
    for b in range(nb):
        z = z_ref[b]
        zprev = jnp.where(rowid == 0, carry_scr[b], pltpu.roll(z, 1, 0))
        carry_scr[b] = z[tc - 1:tc]
        zm = z + (zprev - z) * mu_ref[...]
        r = zm[:, 0:c_b]
        k = zm[:, c_b:2 * c_b]
        v = zm[:, 2 * c_b:3 * c_b]
        x_lora = zm[:, lora0:]
        w = _log_sigmoid(w0_ref[...] + _dot_x3(jnp.tanh(x_lora), w2_ref[...])) - 0.5
        dec = jnp.exp(-jnp.exp(w))
        a = _sigmoid(a0_ref[...] + _dot_x3(x_lora, a2_ref[...]))
        g = _dot_x3(_sigmoid(x_lora), g2_ref[...])
        kk = k * kkw_ref[...]
        kk = kk / jnp.maximum(jnp.sqrt(_dot_l2(kk * kk, q_heads)), 1e-12)
        k2 = k * (1.0 + (a - 1.0) * ka_ref[...])
        beta = kk * a
        g_s[b] = g
        bo_s[b] = _dot_l2(r * k2 * rk_ref[...], q_heads) * v
        per_step = ((nkk_s, -kk), (w_s, dec), (be_s, beta), (k_s, k2), (wr_s, dec * r), (v_s, v),
                    (br_s, _dot_l2(beta * r, q_heads)), (kr_s, _dot_l2(k2 * r, q_heads)))
        for ref, val in per_step:
            for p in range(n_pairs):
                ref[pl.ds(b * n_pairs + p, tc, stride=n_chains), :] = val[:, p * LANES:(p + 1) * LANES]

    diag = jnp.where(_iota((HEAD_B, LANES), 0) == _iota((HEAD_B, LANES), 1) % HEAD_B, 1.0, 0.0)
    q_pair = _block_ones(LANES, HEAD_B)
    n_steps = jnp.clip(seq_len - c * tc, 0, tc)
    chains = [(b, p) for b in range(nb) for p in range(n_pairs)]

    @pl.when(n_steps < tc)
    def _():
        y_s[...] = jnp.zeros_like(y_s)

    def step(t, carry):
        rows = pl.ds(pl.multiple_of(t * n_chains, n_chains), n_chains)
        nkk_t, wr_t, v_t = nkk_s[rows, :], wr_s[rows, :], v_s[rows, :]
        parts = []
        for ci, (b, p) in enumerate(chains):
            st = s_scr[b, p]
            m1h, m1l = _split2(st * nkk_t[ci:ci + 1])
            parts += [m1h, m1l, (st * wr_t[ci:ci + 1]).astype(BF16), (diag * v_t[ci:ci + 1]).astype(BF16)]
        res = _dot(jnp.concatenate(parts, axis=0), q_pair)
        w_t, be_t, k_t, br_t, kr_t = w_s[rows, :], be_s[rows, :], k_s[rows, :], br_s[rows, :], kr_s[rows, :]
        y_rows = []
        for ci, (b, p) in enumerate(chains):
            o = ci * 4 * HEAD_B
            sa = res[o:o + HEAD_B] + res[o + HEAD_B:o + 2 * HEAD_B]
            y1 = res[o + 2 * HEAD_B:o + 3 * HEAD_B]
            vb = res[o + 3 * HEAD_B:o + 4 * HEAD_B]
            st = s_scr[b, p]
            yb = y1 + sa * br_t[ci:ci + 1] + vb * kr_t[ci:ci + 1]
            s_scr[b, p] = st * w_t[ci:ci + 1] + sa * be_t[ci:ci + 1] + vb * k_t[ci:ci + 1]
            y_rows.append(jnp.sum(diag * yb, axis=0, keepdims=True))
        y_s[rows, :] = jnp.concatenate(y_rows, axis=0)
        return carry

    lax.fori_loop(0, n_steps, step, 0)

    inv_n = 1.0 / HEAD_B
    for b in range(nb):
        y = jnp.concatenate([y_s[pl.ds(b * n_pairs + p, tc, stride=n_chains), :] for p in range(n_pairs)], axis=1)
        mu = _dot_l2(y, q_heads) * inv_n
        d = y - mu
        var = _dot_l2(d * d, q_heads) * inv_n
        yn = d * lax.rsqrt(var + GROUPNORM_EPS) * lg_ref[...] + lb_ref[...]
        o_ref[b] = ((yn + bo_s[b]) * g_s[b]).astype(BF16)
    sout_ref[...] = s_scr[...]


def _rwkv(z3, shift_prev, s0p, prm, seq_len, tc):
    bt, lp, n_rwkv = z3.shape
    nb = 4
    c_b = prm["w0"].shape[1]
    n_pairs = c_b // LANES
    full = lambda a: pl.BlockSpec(a.shape, lambda g, c: (0,) * a.ndim)
    names = ["mu", "w0", "w2", "a0", "a2", "g2", "kkw", "ka", "rk", "lg", "lb"]
    plist = [prm[n] for n in names]
    assert nb * n_pairs == SUBLANES, "one time-major tile row per (sequence, head pair) chain"
    chunk = lambda: pltpu.VMEM((nb, tc, c_b), F32)
    tiles = lambda: pltpu.VMEM((tc * SUBLANES, LANES), F32)
    return pl.pallas_call(
        functools.partial(_rwkv_body, nb=nb, tc=tc, seq_len=seq_len, c_b=c_b),
        grid=(bt // nb, lp // tc),
        in_specs=[pl.BlockSpec((nb, tc, n_rwkv), lambda g, c: (g, c, 0)),
                  pl.BlockSpec((nb, 1, n_rwkv), lambda g, c: (g, 0, 0)),
                  pl.BlockSpec((nb, n_pairs, HEAD_B, LANES), lambda g, c: (g, 0, 0, 0))]
                 + [full(a) for a in plist],
        out_specs=[pl.BlockSpec((nb, tc, c_b), lambda g, c: (g, c, 0)),
                   pl.BlockSpec((nb, n_pairs, HEAD_B, LANES), lambda g, c: (g, 0, 0, 0))],
        out_shape=[jax.ShapeDtypeStruct((bt, lp, c_b), BF16),
                   jax.ShapeDtypeStruct((bt, n_pairs, HEAD_B, LANES), F32)],
        scratch_shapes=[pltpu.VMEM((nb, n_pairs, HEAD_B, LANES), F32),
                        pltpu.VMEM((nb, 1, n_rwkv), F32)] + [tiles() for _ in range(8)]
                       + [chunk(), chunk(), tiles()],
        compiler_params=_cparams("parallel", "arbitrary"),
        name="rwkv7",
    )(z3, shift_prev, s0p, *plist)


def _s5_prep_body(are_ref, aim_ref, ldt_ref, bre_ref, bim_ref, abre_ref, abim_ref, bbre_ref, bbim_ref):
    a_re = are_ref[...]
    a_im = aim_ref[...]
    step = jnp.exp(ldt_ref[...])
    mag = jnp.exp(step * a_re)
    ab_re = mag * jnp.cos(step * a_im)
    ab_im = mag * jnp.sin(step * a_im)
    den = a_re * a_re + a_im * a_im
    cf_re = ((ab_re - 1.0) * a_re + ab_im * a_im) / den
    cf_im = (ab_im * a_re - (ab_re - 1.0) * a_im) / den
    abre_ref[...] = ab_re
    abim_ref[...] = ab_im
    bbre_ref[...] = cf_re * bre_ref[...] - cf_im * bim_ref[...]
    bbim_ref[...] = cf_re * bim_ref[...] + cf_im * bre_ref[...]


def _s5_prep(a_re, a_im, log_dt, b_re, b_im):
    g, p = a_re.shape
    grp = b_re.shape[2]
    r3 = lambda x: x.reshape(g, 1, p)
    ldt = jnp.broadcast_to(log_dt[:, None, None], (g, 1, p))
    bt = lambda x: jnp.transpose(x, (0, 2, 1))
    vec = jax.ShapeDtypeStruct((g, 1, p), F32)
    mat = jax.ShapeDtypeStruct((g, grp, p), F32)
    return pl.pallas_call(_s5_prep_body, out_shape=[vec, vec, mat, mat], name="s5_prep")(
        r3(a_re), r3(a_im), ldt, bt(b_re), bt(b_im))


def _gelu_tanh(x):
    return 0.5 * x * (1.0 + jnp.tanh(0.7978845608028654 * (x + 0.044715 * (x * x * x))))


def _s5_body(u_ref, re0_ref, im0_ref, bb_ref, ab_ref, cre_ref, cim_ref, d_ref, gw_ref, gb_ref,
             o_ref, reo_ref, imo_ref, xr_scr, xi_scr, u_s, xr_s, xi_s, o_s,
             *, nb, tc, seq_len, precise):
    c = pl.program_id(1)
    n_state = ab_ref.shape[1]

    @pl.when(c == 0)
    def _():
        xr_scr[...] = re0_ref[0]
        xi_scr[...] = im0_ref[0]
        if nb < SUBLANES:
            u_s[...] = jnp.zeros_like(u_s)

    n_lane_tiles = u_s.shape[0]
    for b in range(nb):
        ub = u_ref[b]
        for j in range(n_lane_tiles):
            u_s[j, pl.ds(b, tc, stride=SUBLANES), :] = ub[:, j * LANES:(j + 1) * LANES]
    u = jnp.concatenate([u_s[j] for j in range(n_lane_tiles)], axis=1)
    for half, dst in enumerate((xr_s, xi_s)):
        cols = slice(half * n_state, (half + 1) * n_state)
        if precise:
            uh, ul = _split2(u)
            dst[...] = _dot(uh, bb_ref[0, :, cols]) + _dot(ul, bb_ref[0, :, cols]) + _dot(uh, bb_ref[1, :, cols])
        else:
            dst[...] = _dot(u.astype(BF16), bb_ref[0, :, cols])
    ar = ab_ref[0:1, :]
    ai = ab_ref[1:2, :]

    def step(t, carry):
        xr, xi = carry
        rows = pl.ds(pl.multiple_of(t * SUBLANES, SUBLANES), SUBLANES)
        nr = ar * xr - ai * xi + xr_s[rows, :]
        ni = ar * xi + ai * xr + xi_s[rows, :]
        xr_s[rows, :] = nr
        xi_s[rows, :] = ni
        return nr, ni

    n_steps = jnp.clip(seq_len - c * tc, 0, tc)
    xr, xi = lax.fori_loop(0, n_steps, step, (xr_scr[...], xi_scr[...]))
    xr_scr[...] = xr
    xi_scr[...] = xi
    reo_ref[0] = xr
    imo_ref[0] = xi
    y = (_dot(xr_s[...].astype(BF16), cre_ref[...]) - _dot(xi_s[...].astype(BF16), cim_ref[...])
         + d_ref[...] * u)
    hg = _gelu_tanh(y)
    out = hg * _sigmoid(_dot_x3(hg, gw_ref[...]) + gb_ref[...])
    for j in range(n_lane_tiles):
        o_s[j] = out[:, j * LANES:(j + 1) * LANES]
    for b in range(nb):
        o_ref[b] = jnp.concatenate([o_s[j, pl.ds(b, tc, stride=SUBLANES), :] for j in range(n_lane_tiles)],
                                   axis=1).astype(BF16)


def _s5(u3, re0, im0, prm, seq_len, nb, tc, precise):
    bt, lp, c_c = u3.shape
    n_state = re0.shape[2]
    full = lambda a: pl.BlockSpec(a.shape, lambda g, c: (0,) * a.ndim)
    plist = [prm[n] for n in ("bb", "ab", "cre", "cim", "d", "gw", "gb")]
    st = pl.BlockSpec((1, SUBLANES, n_state), lambda g, c: (g, 0, 0))
    st_shape = jax.ShapeDtypeStruct((bt // nb, SUBLANES, n_state), F32)
    rows = tc * SUBLANES
    return pl.pallas_call(
        functools.partial(_s5_body, nb=nb, tc=tc, seq_len=seq_len, precise=precise),
        grid=(bt // nb, lp // tc),
        in_specs=[pl.BlockSpec((nb, tc, c_c), lambda g, c: (g, c, 0)), st, st] + [full(a) for a in plist],
        out_specs=[pl.BlockSpec((nb, tc, c_c), lambda g, c: (g, c, 0)), st, st],
        out_shape=[jax.ShapeDtypeStruct((bt, lp, c_c), BF16), st_shape, st_shape],
        scratch_shapes=[pltpu.VMEM((SUBLANES, n_state), F32), pltpu.VMEM((SUBLANES, n_state), F32),
                        pltpu.VMEM((c_c // LANES, rows, LANES), F32), pltpu.VMEM((rows, n_state), F32),
                        pltpu.VMEM((rows, n_state), F32), pltpu.VMEM((c_c // LANES, rows, LANES), F32)],
        compiler_params=_cparams("parallel", "arbitrary"),
        name="s5",
    )(u3, re0, im0, *plist)


def _merge_body(h_ref, oa_ref, ob_ref, oc_ref, g_ref, wg_ref, pa_ref, pb_ref, pc_ref, wo_ref, o_ref, *, d):
    h = h_ref[...]
    xn = _rms(h, g_ref[...]).astype(BF16)
    merged = None
    for n, (br_ref, pj_ref) in enumerate(((oa_ref, pa_ref), (ob_ref, pb_ref), (oc_ref, pc_ref))):
        gate = _sigmoid(_dot(xn, wg_ref[:, n * d:(n + 1) * d]))
        term = gate * _dot(br_ref[...], pj_ref[...])
        merged = term if merged is None else merged + term
    o_ref[...] = h + _dot(merged.astype(BF16), wo_ref[...])


def _merge(h2, oa, ob, oc, g1, wg, pa, pb, pc, wo):
    rows, d = h2.shape
    tm = _row_tile(rows, 512)
    row = lambda a: pl.BlockSpec((tm, a.shape[1]), lambda i: (i, 0))
    full = lambda a: pl.BlockSpec(a.shape, lambda i: (0,) * a.ndim)
    return pl.pallas_call(
        functools.partial(_merge_body, d=d),
        grid=(rows // tm,),
        in_specs=[row(h2), row(oa), row(ob), row(oc)] + [full(a) for a in (g1, wg, pa, pb, pc, wo)],
        out_specs=row(h2),
        out_shape=jax.ShapeDtypeStruct(h2.shape, F32),
        compiler_params=_cparams("parallel"),
        name="merge_outproj",
    )(h2, oa, ob, oc, g1, wg, pa, pb, pc, wo)


PREV_ROWS = 16


def _ffn_body(*refs, tm, seq_rows, seq_valid, decode):
    if decode:
        (x_ref, pa_ref, pb_ref, g_ref, ua_ref, ub_ref, cwa_ref, cwb_ref, cba_ref, cbb_ref, dn_ref,
         o_ref, xn_scr) = refs
    else:
        (x_ref, xp_ref, g_ref, ua_ref, ub_ref, cwa_ref, cwb_ref, cba_ref, cbb_ref, dn_ref,
         o_ref, xn_scr) = refs
    i = pl.program_id(0)
    j = pl.program_id(1)

    @pl.when(j == 0)
    def _():
        o_ref[...] = x_ref[...]
        if decode:
            xn_scr[...] = _rms(x_ref[...], g_ref[...]).astype(BF16)
        else:
            d = x_ref.shape[1]
            pos_p = (i * tm - PREV_ROWS + _iota((PREV_ROWS, d), 0)) % seq_rows
            keep = (pos_p < seq_valid) & (i > 0)
            xn_scr[0:PREV_ROWS] = jnp.where(keep, _rms(xp_ref[...], g_ref[...]), 0.0).astype(BF16)
            pos = (i * tm + _iota((tm, d), 0)) % seq_rows
            xn_scr[PREV_ROWS:] = jnp.where(pos < seq_valid, _rms(x_ref[...], g_ref[...]), 0.0).astype(BF16)

    nc = ua_ref.shape[1]
    xn = xn_scr[...]

    def conv(up_ref, cw_ref, cb_ref, prev_ref):
        h_ext = _dot(xn, up_ref[...])
        if decode:
            hu = h_ext
            prev = prev_ref[...]
            pos = _iota((tm, nc), 0) % seq_rows
            sh1 = jnp.where(pos == 0, pltpu.roll(prev, tm - 1, 0), pltpu.roll(hu, 1, 0))
            sh2 = jnp.where(pos < 2, prev, pltpu.roll(hu, 2, 0))
        else:
            hu = h_ext[PREV_ROWS:]
            last = h_ext[PREV_ROWS - 1:PREV_ROWS]
            last2 = h_ext[PREV_ROWS - 2:PREV_ROWS - 1]
            r1 = pltpu.roll(hu, 1, 0)
            r2 = pltpu.roll(hu, 2, 0)
            row8 = _iota((SUBLANES, nc), 0)
            head1 = jnp.where(row8 == 0, last, r1[0:SUBLANES])
            head2 = jnp.where(row8 == 0, last2, jnp.where(row8 == 1, last, r2[0:SUBLANES]))
            sh1 = jnp.concatenate([head1, r1[SUBLANES:]], axis=0)
            sh2 = jnp.concatenate([head2, r2[SUBLANES:]], axis=0)
        return cw_ref[0:1, :] * sh2 + cw_ref[1:2, :] * sh1 + cw_ref[2:3, :] * hu + cb_ref[...]

    a = conv(ua_ref, cwa_ref, cba_ref, pa_ref if decode else None)
    bgate = conv(ub_ref, cwb_ref, cbb_ref, pb_ref if decode else None)
    act = (a * _sigmoid(a)) * bgate
    o_ref[...] += _dot(act.astype(BF16), dn_ref[...])


def _ffn(h2, g2, ua, ub, cwa, cwb, cba, cbb, dn, seq_rows, seq_valid, prev_pad=None):
    assert prev_pad is not None or seq_rows - seq_valid >= CONV_TAPS - 1
    rows, d = h2.shape
    d_ff = ua.shape[1]
    nc = 256
    n_chunks = d_ff // nc
    decode = prev_pad is not None
    tm = _row_tile(rows, 1024)
    xrow = pl.BlockSpec((tm, d), lambda i, j: (i, 0))
    col = lambda r: pl.BlockSpec((r, nc), lambda i, j: (0, j))
    w_specs = [pl.BlockSpec((1, d), lambda i, j: (0, 0)), col(d), col(d), col(CONV_TAPS), col(CONV_TAPS),
               col(1), col(1), pl.BlockSpec((nc, d), lambda i, j: (j, 0))]
    scratch = [pltpu.VMEM((tm if decode else tm + PREV_ROWS, d), BF16)]
    if decode:
        in_specs = [xrow, pl.BlockSpec((tm, nc), lambda i, j: (i, j)),
                    pl.BlockSpec((tm, nc), lambda i, j: (i, n_chunks + j))] + w_specs
        args = (h2, prev_pad, prev_pad, g2, ua, ub, cwa, cwb, cba, cbb, dn)
    else:
        per = tm // PREV_ROWS
        in_specs = [xrow, pl.BlockSpec((PREV_ROWS, d), lambda i, j: (jnp.maximum(i * per - 1, 0), 0))] + w_specs
        args = (h2, h2, g2, ua, ub, cwa, cwb, cba, cbb, dn)
    return pl.pallas_call(
        functools.partial(_ffn_body, tm=tm, seq_rows=seq_rows, seq_valid=seq_valid, decode=decode),
        grid=(rows // tm, n_chunks),
        in_specs=in_specs,
        out_specs=xrow,
        out_shape=jax.ShapeDtypeStruct(h2.shape, F32),
        scratch_shapes=scratch,
        compiler_params=_cparams("parallel", "arbitrary"),
        name="conv_ffn",
    )(*args)


def _up_rows_body(x_ref, g_ref, up_ref, o_ref):
    o_ref[...] = _dot(_rms(x_ref[...], g_ref[...]).astype(BF16), up_ref[...])


def _up_rows(x2, g2, up):
    rows, d = x2.shape
    n = up.shape[1]
    nc = 512
    return pl.pallas_call(
        _up_rows_body,
        grid=(n // nc,),
        in_specs=[pl.BlockSpec((rows, d), lambda j: (0, 0)), pl.BlockSpec((1, d), lambda j: (0, 0)),
                  pl.BlockSpec((d, nc), lambda j: (0, j))],
        out_specs=pl.BlockSpec((rows, nc), lambda j: (0, j)),
        out_shape=jax.ShapeDtypeStruct((rows, n), F32),
        compiler_params=_cparams("parallel"),
        name="conv_state_rows",
    )(x2, g2, up)


def _final_norm_body(x_ref, g_ref, o_ref):
    o_ref[...] = _rms(x_ref[...], g_ref[...])


def _final_norm(h2, g):
    rows, d = h2.shape
    tm = _row_tile(rows, 1024)
    spec = pl.BlockSpec((tm, d), lambda i: (i, 0))
    return pl.pallas_call(
        _final_norm_body,
        grid=(rows // tm,),
        in_specs=[spec, pl.BlockSpec((1, d), lambda i: (0, 0))],
        out_specs=spec,
        out_shape=jax.ShapeDtypeStruct(h2.shape, F32),
        compiler_params=_cparams("parallel"),
        name="final_norm",
    )(h2, g)


def _pack_state(s):
    b, h, v, k = s.shape
    return s.reshape(b, h // 2, 2, v, k).transpose(0, 1, 3, 2, 4).reshape(b, h // 2, v, 2 * k)


def _unpack_state(s):
    b, hp, v, k2 = s.shape
    return s.reshape(b, hp, v, 2, k2 // 2).transpose(0, 1, 3, 2, 4).reshape(b, 2 * hp, v, k2 // 2)


def _block_diag(x):
    g, a, c = x.shape
    eye = jnp.eye(g, dtype=bool)[:, None, :, None]
    return jnp.where(eye, x[:, :, None, :], 0.0).reshape(g * a, g * c)


def _pad_rows(x, n):
    return jnp.concatenate([x, jnp.zeros((n - x.shape[0],) + x.shape[1:], x.dtype)], axis=0)


def kernel(x_prompt, x_sample, cache_k, cache_v, cache_logf, page_table, state_rwkv_wkv, state_rwkv_shift, state_ssm_re, state_ssm_im, state_ffn_conv, meta_tokens, norm1_g, w_in, fox_bf, rwkv_mu, rwkv_w0, rwkv_w2, rwkv_a0, rwkv_a2, rwkv_g2, rwkv_kk, rwkv_ka, rwkv_rk, rwkv_lnx_g, rwkv_lnx_b, ssm_a_re, ssm_a_im, ssm_log_dt, ssm_b_re, ssm_b_im, ssm_c_re, ssm_c_im, ssm_d, ssm_glu_w, ssm_glu_b, proj_a, proj_b, proj_c, w_out, norm2_g, ffn_up, ffn_conv_w, ffn_conv_b, ffn_down, final_norm_g):
    bsz, seq, d = x_prompt.shape
    db, t_new, _ = x_sample.shape
    depth = norm1_g.shape[0]
    n_ha = fox_bf.shape[1]
    c_a = n_ha * HEAD_A
    n_rwkv = rwkv_mu.shape[1]
    c_b = rwkv_w0.shape[1]
    n_hb = c_b // HEAD_B
    g_c, p_c = ssm_a_re.shape[1], ssm_a_re.shape[2]
    c_c = g_c * SSM_GRP
    n_state = g_c * p_c
    d_ff = ffn_down.shape[1]
    n_pool, page = cache_k.shape[1], cache_k.shape[2]
    seq_len = N_META_TOK + seq
    lp = -(-seq_len // 256) * 256
    seq_tile = 256

    meta = jnp.broadcast_to(meta_tokens[None], (bsz, N_META_TOK, d))
    h_p = jnp.concatenate([meta, x_prompt, jnp.zeros((bsz, lp - seq_len, d), F32)], axis=1)
    h_p = h_p.reshape(bsz * lp, d)
    h_s = x_sample.reshape(db * t_new, d)

    assert t_new * n_ha <= LANES
    zeros_shift = jnp.zeros((bsz, 1, n_rwkv), F32)
    zeros_wkv = jnp.zeros((bsz, n_hb // 2, HEAD_B, 2 * HEAD_B), F32)
    assert bsz <= SUBLANES and db % SUBLANES == 0
    zeros_ssm = jnp.zeros((1, SUBLANES, n_state), F32)

    outs_p, outs_s = [], []
    for l in range(depth):
        w = w_in[l]
        o_f = 3 * c_a
        o_z = o_f + n_ha
        o_g = o_z + n_rwkv + c_c
        wp = jnp.concatenate([w[:, :o_f], w[:, o_f:o_z], jnp.zeros((d, LANES - n_ha), F32),
                              w[:, o_z:o_g]], axis=1).astype(BF16)
        wg = w[:, o_g:].astype(BF16)
        bfp = jnp.concatenate([fox_bf[l], jnp.zeros((LANES - n_ha,), F32)])[None]
        g1 = norm1_g[l][None]
        g2 = norm2_g[l][None]
        row1 = lambda x: x.reshape(1, -1)
        lora_pad = lambda x, o: jnp.zeros((n_rwkv - 3 * c_b, c_b), F32).at[o:o + x.shape[0]].set(x)
        rw = dict(mu=row1(rwkv_mu[l]), w0=row1(rwkv_w0[l]), w2=lora_pad(rwkv_w2[l], 0),
                  a0=row1(rwkv_a0[l]), a2=lora_pad(rwkv_a2[l], LORA_W_DIM),
                  g2=lora_pad(rwkv_g2[l], LORA_W_DIM + LORA_A_DIM), kkw=row1(rwkv_kk[l]),
                  ka=row1(rwkv_ka[l]), rk=row1(rwkv_rk[l]), lg=row1(rwkv_lnx_g[l]), lb=row1(rwkv_lnx_b[l]))
        ab_re, ab_im, bbt_re, bbt_im = _s5_prep(ssm_a_re[l], ssm_a_im[l], ssm_log_dt[l], ssm_b_re[l], ssm_b_im[l])
        bb = jnp.concatenate([_block_diag(bbt_re), _block_diag(bbt_im)], axis=1)
        bb_hi = bb.astype(BF16)
        bb_lo = (bb - bb_hi.astype(F32)).astype(BF16)
        s5p = dict(bb=jnp.stack([bb_hi, bb_lo]),
                   ab=jnp.concatenate([ab_re.reshape(1, n_state), ab_im.reshape(1, n_state)], axis=0),
                   cre=_block_diag(jnp.transpose(ssm_c_re[l], (0, 2, 1))).astype(BF16),
                   cim=_block_diag(jnp.transpose(ssm_c_im[l], (0, 2, 1))).astype(BF16),
                   d=row1(ssm_d[l]), gw=ssm_glu_w[l], gb=row1(ssm_glu_b[l]))
        pa, pb, pc, wo = (x[l].astype(BF16) for x in (proj_a, proj_b, proj_c, w_out))
        up = ffn_up[l].astype(BF16)
        ua, ub = up[:, :d_ff], up[:, d_ff:]
        cwa, cwb = ffn_conv_w[l][:, :d_ff], ffn_conv_w[l][:, d_ff:]
        cba, cbb = ffn_conv_b[l][None, :d_ff], ffn_conv_b[l][None, d_ff:]
        dn = ffn_down[l].astype(BF16)

        qb, k, v, kb, vb, lf, z, u = _inproj(h_p, g1, wp, bfp, c_a, n_rwkv, c_c)
        r3 = lambda x: x.reshape(bsz, lp, x.shape[1])
        o_a = _attn_prompt(*_fox_prep(r3(lf), r3(qb), r3(kb), r3(vb)), lp)
        o_b, wkv = _rwkv(r3(z), zeros_shift, zeros_wkv, rw, seq_len, seq_tile)
        o_c, re_n, im_n = _s5(r3(u), zeros_ssm, zeros_ssm, s5p, seq_len, bsz, seq_tile // 2, False)
        flat = lambda x: x.reshape(bsz * lp, x.shape[2])
        h_mid = _merge(h_p, flat(o_a), flat(o_b), flat(o_c), g1, wg, pa, pb, pc, wo)
        h_p = _ffn(h_mid, g2, ua, ub, cwa, cwb, cba, cbb, dn, lp, seq_len)
        tail = r3(h_mid)[:, seq_len - 2:seq_len].reshape(bsz * 2, d)
        conv = _up_rows(_pad_rows(tail, 16), g2, up)[:bsz * 2].reshape(bsz, 2, 2 * d_ff)
        outs_p.append((r3(k)[:, :seq_len].reshape(bsz, seq_len, n_ha, HEAD_A),
                       r3(v)[:, :seq_len].reshape(bsz, seq_len, n_ha, HEAD_A),
                       r3(lf)[:, :seq_len, :n_ha], _unpack_state(wkv), r3(z)[:, seq_len - 1],
                       re_n[0, :bsz].reshape(bsz, g_c, p_c), im_n[0, :bsz].reshape(bsz, g_c, p_c), conv))

        qb, k, v, kb, vb, lf, z, u = _inproj(h_s, g1, wp, bfp, c_a, n_rwkv, c_c)
        s3 = lambda x: x.reshape(db, t_new, x.shape[1])
        nr = t_new * n_ha
        rows_th = lambda x: x.reshape(db, nr, HEAD_A)
        lf_new = s3(lf)[:, :, :n_ha].reshape(db, nr)
        lf_col = jnp.broadcast_to(lf_new[:, :, None], (db, nr, LANES))
        lf_row = jnp.concatenate([lf_new, jnp.zeros((db, LANES - nr), F32)], axis=1)
        lf_row = jnp.broadcast_to(lf_row[:, None, :], (db, SUBLANES, LANES))
        suf, tot = _pool_suffix(cache_logf[l].reshape(n_pool, page * n_ha), n_ha)
        o_a = _attn_sample(page_table, l, rows_th(qb), rows_th(kb), rows_th(vb), lf_col, lf_row,
                           cache_k, cache_v, suf[:, None, :], tot[:, None, :])
        o_a = o_a.reshape(db, t_new, c_a)
        o_b, wkv = _rwkv(s3(z), state_rwkv_shift[l][:, None], _pack_state(state_rwkv_wkv[l]), rw, t_new, t_new)
        o_c, re_n, im_n = _s5(s3(u), state_ssm_re[l].reshape(db // SUBLANES, SUBLANES, n_state),
                              state_ssm_im[l].reshape(db // SUBLANES, SUBLANES, n_state), s5p, t_new,
                              SUBLANES, t_new, True)
        flat = lambda x: x.reshape(db * t_new, x.shape[2])
        h_mid = _merge(h_s, flat(o_a), flat(o_b), flat(o_c), g1, wg, pa, pb, pc, wo)
        prev = state_ffn_conv[l]
        prev_pad = jnp.concatenate([prev, jnp.zeros((db, t_new - prev.shape[1], prev.shape[2]), F32)], axis=1)
        h_s = _ffn(h_mid, g2, ua, ub, cwa, cwb, cba, cbb, dn, t_new, t_new,
                   prev_pad.reshape(db * t_new, 2 * d_ff))
        tail = s3(h_mid)[:, t_new - 2:].reshape(db * 2, d)
        conv = _up_rows(tail, g2, up).reshape(db, 2, 2 * d_ff)
        outs_s.append((s3(k).reshape(db, t_new, n_ha, HEAD_A), s3(v).reshape(db, t_new, n_ha, HEAD_A),
                       s3(lf)[:, :, :n_ha], _unpack_state(wkv), s3(z)[:, t_new - 1],
                       re_n.reshape(db, g_c, p_c), im_n.reshape(db, g_c, p_c), conv))

    fg = final_norm_g[None]
    y_p = _final_norm(h_p, fg).reshape(bsz, lp, d)[:, N_META_TOK:seq_len]
    y_s = _final_norm(h_s, fg).reshape(db, t_new, d)
    stack = lambda outs: [jnp.stack(t) for t in zip(*outs)]
    return tuple([y_p, y_s] + stack(outs_p) + stack(outs_s))
```

```python
import functools

import jax
import jax.numpy as jnp
from jax import lax
from jax.experimental import pallas as pl
from jax.experimental.pallas import tpu as pltpu

F32 = jnp.float32
BF16 = jnp.bfloat16

N_META_TOK = 16
RMS_EPS = 1e-6
HEAD_A = 64
HEAD_B = 64
LORA_W_DIM = 64
LORA_A_DIM = 64
LORA_G_DIM = 128
GROUPNORM_EPS = 64e-5
SSM_GRP = 16
SSM_P = 64
CONV_TAPS = 3
LANES = 128
SUBLANES = 8
VMEM_LIMIT = 52 * 1024 * 1024


def _cparams(*sem):
    return pltpu.CompilerParams(dimension_semantics=sem, vmem_limit_bytes=VMEM_LIMIT)


def _dot(a, b):
    return jnp.dot(a, b, preferred_element_type=F32)


def _dot_nt(a, b):
    return lax.dot_general(a, b, (((1,), (1,)), ((), ())), preferred_element_type=F32)


def _split2(x):
    hi = x.astype(BF16)
    lo = (x - hi.astype(F32)).astype(BF16)
    return hi, lo


def _split3(x):
    hi = x.astype(BF16)
    r = x - hi.astype(F32)
    mid = r.astype(BF16)
    lo = (r - mid.astype(F32)).astype(BF16)
    return hi, mid, lo


def _dot_l2(x, w_bf):
    hi, lo = _split2(x)
    return _dot(hi, w_bf) + _dot(lo, w_bf)


def _dot_l3(x, w_bf):
    hi, mid, lo = _split3(x)
    return _dot(hi, w_bf) + _dot(mid, w_bf) + _dot(lo, w_bf)


def _dot_r3(w_bf, x):
    hi, mid, lo = _split3(x)
    return _dot(w_bf, hi) + _dot(w_bf, mid) + _dot(w_bf, lo)


def _dot_x3(a, b):
    ah, al = _split2(a)
    bh, bl = _split2(b)
    return _dot(ah, bh) + _dot(al, bh) + _dot(ah, bl)


def _sigmoid(x):
    return 1.0 / (1.0 + jnp.exp(-x))


def _log_sigmoid(x):
    return jnp.minimum(x, 0.0) - jnp.log1p(jnp.exp(-jnp.abs(x)))


def _rms(x, g):
    ms = jnp.mean(x * x, axis=-1, keepdims=True)
    return x * lax.rsqrt(ms + RMS_EPS) * g


def _iota(shape, dim):
    return lax.broadcasted_iota(jnp.int32, shape, dim)


def _block_ones(n, blk):
    return jnp.where(_iota((n, n), 0) // blk == _iota((n, n), 1) // blk, 1.0, 0.0).astype(BF16)


def _row_tile(rows, cap):
    t = cap
    while rows % t:
        t //= 2
    return t


def _inproj_body(h_ref, g_ref, w_ref, bf_ref, qb_ref, k_ref, v_ref, kb_ref, vb_ref, lf_ref, z_ref,
                 u_ref, *, c_a, n_rwkv, c_c):
    xn = _rms(h_ref[...], g_ref[...]).astype(BF16)
    q = _dot(xn, w_ref[:, 0:c_a])
    qb_ref[...] = (q * (HEAD_A ** -0.5)).astype(BF16)
    k = _dot(xn, w_ref[:, c_a:2 * c_a])
    k_ref[...] = k
    kb_ref[...] = k.astype(BF16)
    v = _dot(xn, w_ref[:, 2 * c_a:3 * c_a])
    v_ref[...] = v
    vb_ref[...] = v.astype(BF16)
    o = 3 * c_a
    f = _dot(xn, w_ref[:, o:o + LANES]) + bf_ref[...]
    lf_ref[...] = _log_sigmoid(f)
    o += LANES
    z_ref[...] = _dot(xn, w_ref[:, o:o + n_rwkv])
    o += n_rwkv
    u_ref[...] = _dot(xn, w_ref[:, o:o + c_c])


def _inproj(h2, g1, wp, bfp, c_a, n_rwkv, c_c):
    rows, d = h2.shape
    tm = _row_tile(rows, 512)
    npj = wp.shape[1]
    row = lambda n: pl.BlockSpec((tm, n), lambda i: (i, 0))
    full = lambda a: pl.BlockSpec(a.shape, lambda i: (0,) * a.ndim)
    outs = [(c_a, BF16), (c_a, F32), (c_a, F32), (c_a, BF16), (c_a, BF16), (LANES, F32),
            (n_rwkv, F32), (c_c, F32)]
    return pl.pallas_call(
        functools.partial(_inproj_body, c_a=c_a, n_rwkv=n_rwkv, c_c=c_c),
        grid=(rows // tm,),
        in_specs=[row(d), full(g1), full(wp), full(bfp)],
        out_specs=[row(n) for n, _ in outs],
        out_shape=[jax.ShapeDtypeStruct((rows, n), dt) for n, dt in outs],
        compiler_params=_cparams("parallel"),
        name="inproj",
    )(h2, g1, wp, bfp)


N_BIAS = 3


def _fox_prep_body(lf_ref, q_ref, k_ref, v_ref, qx_ref, kx_ref, vx_ref, carry_ref, *, tc, n_heads, n_valid):
    @pl.when(pl.program_id(1) == 0)
    def _():
        carry_ref[...] = jnp.zeros_like(carry_ref)

    @pl.when(pl.program_id(1) >= n_valid)
    def _():
        for ref in (qx_ref, kx_ref, vx_ref):
            ref[...] = jnp.zeros_like(ref)

    @pl.when(pl.program_id(1) < n_valid)
    def _():
        _fox_prep_tile(lf_ref, q_ref, k_ref, v_ref, qx_ref, kx_ref, vx_ref, carry_ref, tc, n_heads)


def _fox_prep_tile(lf_ref, q_ref, k_ref, v_ref, qx_ref, kx_ref, vx_ref, carry_ref, tc, n_heads):
    tri = jnp.where(_iota((tc, tc), 0) >= _iota((tc, tc), 1), 1.0, 0.0).astype(BF16)
    cs = _dot_r3(tri, lf_ref[0]) + carry_ref[...]
    carry_ref[...] = cs[tc - 1:tc]
    pieces = jnp.concatenate(_split3(cs), axis=1)

    nl = n_heads * LANES
    src = _iota((N_BIAS * LANES, nl), 0)
    dst = _iota((N_BIAS * LANES, nl), 1)
    base = jnp.where((dst // LANES) % 2 == 0, HEAD_A, 0)
    same_head = src % LANES == dst // LANES
    off = dst % LANES - base
    place_q = jnp.where(same_head & (off == src // LANES), 1.0, 0.0).astype(BF16)
    place_k = jnp.where(same_head & (off == N_BIAS + src // LANES), -1.0, 0.0).astype(BF16)
    off1 = off[0:1]
    qe = _dot(pieces, place_q) + jnp.where((off1 >= N_BIAS) & (off1 < 2 * N_BIAS), 1.0, 0.0)
    ke = _dot(pieces, place_k) + jnp.where((off1 >= 0) & (off1 < N_BIAS), 1.0, 0.0)
    lane_half = _iota((tc, LANES), 1) // HEAD_A
    ones = jnp.ones((tc, LANES), BF16)
    for h in range(n_heads):
        pair = slice((h // 2) * LANES, (h // 2 + 1) * LANES)
        tile = slice(h * LANES, (h + 1) * LANES)
        own = lane_half == h % 2
        qx_ref[0, h] = jnp.where(own, q_ref[0][:, pair], qe[:, tile].astype(BF16))
        kx_ref[0, h] = jnp.where(own, k_ref[0][:, pair], ke[:, tile].astype(BF16))
        vx_ref[0, h] = jnp.where(own, v_ref[0][:, pair], ones)


ATTN_TQ = 256
ATTN_TK = 1024


def _fox_prep(lf3, qb, kb, vb):
    b, lp, c_a = qb.shape
    n_heads = c_a // HEAD_A
    tc = 256
    n_valid = lp // tc
    lkv = -(-lp // ATTN_TK) * ATTN_TK
    row = lambda n: pl.BlockSpec((1, tc, n), lambda i, j: (i, jnp.minimum(j, n_valid - 1), 0))
    out = pl.BlockSpec((1, n_heads, tc, LANES), lambda i, j: (i, 0, j, 0))
    return pl.pallas_call(
        functools.partial(_fox_prep_body, tc=tc, n_heads=n_heads, n_valid=n_valid),
        grid=(b, lkv // tc),
        in_specs=[row(LANES), row(c_a), row(c_a), row(c_a)],
        out_specs=[out, out, out],
        out_shape=[jax.ShapeDtypeStruct((b, n_heads, lkv, LANES), BF16)] * 3,
        scratch_shapes=[pltpu.VMEM((1, LANES), F32)],
        compiler_params=_cparams("parallel", "arbitrary"),
        name="fox_prep",
    )(lf3, qb, kb, vb)


def _attn_body(qx_ref, kx_ref, vx_ref, o_ref, s_scr, *, tq, tk):
    row0 = pl.program_id(2) * tq
    n_full = row0 // tk
    visible = _iota((tq, tk), 1) <= _iota((tq, tk), 0) + (row0 - n_full * tk)
    q = [qx_ref[0, hh] for hh in range(2)]

    def lane_tile_max(s):
        m = s[:, 0:LANES]
        for t in range(1, tk // LANES):
            m = jnp.maximum(m, s[:, t * LANES:(t + 1) * LANES])
        return m

    def pass_max(j, ms, masked):
        start = pl.multiple_of(j * tk, tk)
        out = []
        for hh in range(2):
            s = _dot_nt(q[hh], kx_ref[0, hh, pl.ds(start, tk), :])
            if masked:
                s = jnp.where(visible, s, -jnp.inf)
            s_scr[hh, :, pl.ds(start, tk)] = s
            out.append(jnp.maximum(ms[hh], lane_tile_max(s)))
        return tuple(out)

    ms = tuple(jnp.full((tq, LANES), -jnp.inf, F32) for _ in range(2))
    ms = lax.fori_loop(0, n_full, lambda j, c: pass_max(j, c, False), ms)
    ms = pass_max(n_full, ms, True)
    mb = []
    for hh in range(2):
        m1 = jnp.broadcast_to(jnp.max(ms[hh], axis=1, keepdims=True), (tq, LANES))
        mb.append(jnp.concatenate([m1] * (tk // LANES), axis=1))

    def pass_acc(j, accs):
        start = pl.multiple_of(j * tk, tk)
        out = []
        for hh in range(2):
            p = jnp.exp(s_scr[hh, :, pl.ds(start, tk)] - mb[hh]).astype(BF16)
            out.append(accs[hh] + _dot(p, vx_ref[0, hh, pl.ds(start, tk), :]))
        return tuple(out)

    accs = tuple(jnp.zeros((tq, LANES), F32) for _ in range(2))
    accs = lax.fori_loop(0, n_full + 1, pass_acc, accs)
    outs = [acc / pltpu.roll(acc, HEAD_A, 1) for acc in accs]
    o_ref[0] = jnp.where(_iota((tq, LANES), 1) < HEAD_A, outs[0], outs[1]).astype(BF16)


def _attn_prompt(qx, kx, vx, lp):
    b, n_heads, lkv, _ = qx.shape
    tq, tk = ATTN_TQ, ATTN_TK
    kv = pl.BlockSpec((1, 2, lkv, LANES), lambda bi, p, i: (bi, p, 0, 0))
    return pl.pallas_call(
        functools.partial(_attn_body, tq=tq, tk=tk),
        grid=(b, n_heads // 2, lp // tq),
        in_specs=[pl.BlockSpec((1, 2, tq, LANES), lambda bi, p, i: (bi, p, i, 0)), kv, kv],
        out_specs=pl.BlockSpec((1, tq, LANES), lambda bi, p, i: (bi, i, p)),
        out_shape=jax.ShapeDtypeStruct((b, lp, n_heads * HEAD_A), BF16),
        scratch_shapes=[pltpu.VMEM((2, tq, lkv), F32)],
        compiler_params=_cparams("parallel", "parallel", "arbitrary"),
        name="fox_prompt_attn",
    )(qx, kx, vx)


def _pool_suffix_body(x_ref, suf_ref, tot_ref):
    n = x_ref.shape[1]
    later = jnp.where(_iota((n, n), 0) > _iota((n, n), 1), 1.0, 0.0).astype(BF16)
    whole = jnp.ones((n, n), BF16)
    hi, mid, lo = _split3(x_ref[...])
    suf_ref[...] = _dot(hi, later) + _dot(mid, later) + _dot(lo, later)
    tot_ref[...] = _dot(hi, whole) + _dot(mid, whole) + _dot(lo, whole)


def _pool_suffix(ft2):
    rows, n = ft2.shape
    tm = _row_tile(rows, 1024)
    spec = pl.BlockSpec((tm, n), lambda i: (i, 0))
    return pl.pallas_call(
        _pool_suffix_body,
        grid=(rows // tm,),
        in_specs=[spec],
        out_specs=[spec, spec],
        out_shape=[jax.ShapeDtypeStruct(ft2.shape, F32)] * 2,
        compiler_params=_cparams("parallel"),
        name="fox_pool_suffix",
    )(ft2)


def _attn_sample_body(pt_ref, qbd_ref, kn_ref, vn_ref, lfn_ref, *rest, n_pages, t_new):
    kp = rest[0:n_pages]
    vp = rest[n_pages:2 * n_pages]
    sufp = rest[2 * n_pages:3 * n_pages]
    totp = rest[3 * n_pages:4 * n_pages]
    o_ref = rest[4 * n_pages]
    nr, c_a = qbd_ref.shape[1], qbd_ref.shape[2]
    n_h = nr // t_new
    qbd = qbd_ref[0]
    row = _iota((nr, LANES), 0)
    lane = _iota((nr, LANES), 1)

    lfn_pad = jnp.concatenate([lfn_ref[0], jnp.zeros((LANES - t_new, LANES), F32)], axis=0)
    tri = jnp.where(_iota((LANES, LANES), 0) >= _iota((LANES, LANES), 1), 1.0, 0.0).astype(BF16)
    cs = _dot_r3(tri, lfn_pad)
    rep = jnp.where(_iota((nr, LANES), 1) == row // n_h, 1.0, 0.0).astype(BF16)
    cs_rows = _dot_r3(rep, cs)
    cq = jnp.sum(jnp.where(lane == row % n_h, cs_rows, 0.0), axis=1, keepdims=True)
    cst = cs.T[0:n_h]

    tiles = [None] * (n_pages + 1)
    carry = jnp.zeros((n_h, LANES), F32)
    for j in reversed(range(n_pages)):
        kt = kp[j][...].reshape(c_a, kp[j].shape[2]).astype(BF16)
        bias = sufp[j][...] + carry
        tiles[j] = _dot(qbd, kt) + cq + jnp.concatenate([bias] * t_new, axis=0)
        carry = carry + totp[j][...]
    s = _dot_nt(qbd, kn_ref[0]) + cq - jnp.concatenate([cst] * t_new, axis=0)
    tiles[n_pages] = jnp.where((lane <= row // n_h) & (lane < t_new), s, -jnp.inf)

    m = tiles[0]
    for tl in tiles[1:]:
        m = jnp.maximum(m, tl)
    m = jnp.max(m, axis=1, keepdims=True)
    lsum = jnp.zeros((nr, LANES), F32)
    acc = jnp.zeros((nr, c_a), F32)
    for j in range(n_pages + 1):
        p = jnp.exp(tiles[j] - m)
        lsum = lsum + p
        if j == n_pages:
            acc = acc + _dot(p.astype(BF16), vn_ref[0])
        else:
            vt = vp[j][...].reshape(c_a, vp[j].shape[2]).astype(BF16)
            acc = acc + _dot_nt(p.astype(BF16), vt)
    o_full = acc / jnp.sum(lsum, axis=1, keepdims=True)
    keep = _iota((nr, c_a), 1) // HEAD_A == _iota((nr, c_a), 0) % n_h
    picked = jnp.where(keep, o_full, 0.0).astype(BF16)
    gather = jnp.where((_iota((nr, nr), 1) // n_h == _iota((nr, nr), 0)), 1.0, 0.0).astype(BF16)
    o_ref[0] = _dot(gather, picked)[0:t_new].astype(BF16)


def _attn_sample(page_table, layer, qbd, kn_pad, vn_pad, lfn, cache_kt, cache_vt, suf, tot):
    db, n_pages = page_table.shape
    nr, c_a = qbd.shape[1], qbd.shape[2]
    t_new = lfn.shape[1]
    n_h, hd, page = cache_kt.shape[2:]

    def seq_spec(a):
        return pl.BlockSpec((1,) + a.shape[1:], lambda b, pt: (b,) + (0,) * (a.ndim - 1))

    def pool_spec(j):
        return pl.BlockSpec((None, None, n_h, hd, page), lambda b, pt, j=j: (layer, pt[b, j], 0, 0, 0))

    def small_spec(j):
        return pl.BlockSpec((None, n_h, page), lambda b, pt, j=j: (pt[b, j], 0, 0))

    in_specs = ([seq_spec(qbd), seq_spec(kn_pad), seq_spec(vn_pad), seq_spec(lfn)]
                + [pool_spec(j) for j in range(n_pages)] * 2
                + [small_spec(j) for j in range(n_pages)] * 2)
    grid_spec = pltpu.PrefetchScalarGridSpec(
        num_scalar_prefetch=1,
        grid=(db,),
        in_specs=in_specs,
        out_specs=pl.BlockSpec((1, t_new, c_a), lambda b, pt: (b, 0, 0)),
    )
    args = ([qbd, kn_pad, vn_pad, lfn] + [cache_kt] * n_pages + [cache_vt] * n_pages
            + [suf] * n_pages + [tot] * n_pages)
    return pl.pallas_call(
        functools.partial(_attn_sample_body, n_pages=n_pages, t_new=t_new),
        grid_spec=grid_spec,
        out_shape=jax.ShapeDtypeStruct((db, t_new, c_a), BF16),
        compiler_params=_cparams("parallel"),
        name="fox_paged_attn",
    )(page_table, *args)


def _rwkv_body(z_ref, sp_ref, s0_ref, mu_ref, w0_ref, w2_ref, a0_ref, a2_ref, g2_ref, kkw_ref, ka_ref,
               rk_ref, lg_ref, lb_ref, o_ref, sout_ref,
               s_scr, carry_scr, nkk_s, w_s, be_s, k_s, wr_s, v_s, br_s, kr_s, g_s, bo_s, y_s,
               *, nb, tc, seq_len, c_b):
    c = pl.program_id(1)
    n_pairs = c_b // LANES
    n_chains = nb * n_pairs

    @pl.when(c == 0)
    def _():
        s_scr[...] = s0_ref[...]
        carry_scr[...] = sp_ref[...]

    q_heads = _block_ones(c_b, HEAD_B)
    rowid = _iota((tc, z_ref.shape[2]), 0)
    lora0 = 3 * c_b
    for b in range(nb):
        z = z_ref[b]
        zprev = jnp.where(rowid == 0, carry_scr[b], pltpu.roll(z, 1, 0))
        carry_scr[b] = z[tc - 1:tc]
        zm = z + (zprev - z) * mu_ref[...]
        r = zm[:, 0:c_b]
        k = zm[:, c_b:2 * c_b]
        v = zm[:, 2 * c_b:3 * c_b]
        x_lora = zm[:, lora0:]
        w = _log_sigmoid(w0_ref[...] + _dot_x3(jnp.tanh(x_lora), w2_ref[...])) - 0.5
        dec = jnp.exp(-jnp.exp(w))
        a = _sigmoid(a0_ref[...] + _dot_x3(x_lora, a2_ref[...]))
        g = _dot_x3(_sigmoid(x_lora), g2_ref[...])
        kk = k * kkw_ref[...]
        kk = kk / jnp.maximum(jnp.sqrt(_dot_l2(kk * kk, q_heads)), 1e-12)
        k2 = k * (1.0 + (a - 1.0) * ka_ref[...])
        beta = kk * a
        g_s[b] = g
        bo_s[b] = _dot_l2(r * k2 * rk_ref[...], q_heads) * v
        per_step = ((nkk_s, -kk), (w_s, dec), (be_s, beta), (k_s, k2), (wr_s, dec * r), (v_s, v),
                    (br_s, _dot_l2(beta * r, q_heads)), (kr_s, _dot_l2(k2 * r, q_heads)))
        for ref, val in per_step:
            for p in range(n_pairs):
                ref[pl.ds(b * n_pairs + p, tc, stride=n_chains), :] = val[:, p * LANES:(p + 1) * LANES]

    diag = jnp.where(_iota((HEAD_B, LANES), 0) == _iota((HEAD_B, LANES), 1) % HEAD_B, 1.0, 0.0)
    q_pair = _block_ones(LANES, HEAD_B)
    n_steps = jnp.clip(seq_len - c * tc, 0, tc)
    chains = [(b, p) for b in range(nb) for p in range(n_pairs)]

    @pl.when(n_steps < tc)
    def _():
        y_s[...] = jnp.zeros_like(y_s)

    def step(t, carry):
        rows = pl.ds(pl.multiple_of(t * n_chains, n_chains), n_chains)
        nkk_t, wr_t, v_t = nkk_s[rows, :], wr_s[rows, :], v_s[rows, :]
        parts = []
        for ci, (b, p) in enumerate(chains):
            st = s_scr[b, p]
            m1h, m1l = _split2(st * nkk_t[ci:ci + 1])
            parts += [m1h, m1l, (st * wr_t[ci:ci + 1]).astype(BF16), (diag * v_t[ci:ci + 1]).astype(BF16)]
        res = _dot(jnp.concatenate(parts, axis=0), q_pair)
        w_t, be_t, k_t, br_t, kr_t = w_s[rows, :], be_s[rows, :], k_s[rows, :], br_s[rows, :], kr_s[rows, :]
        y_rows = []
        for ci, (b, p) in enumerate(chains):
            o = ci * 4 * HEAD_B
            sa = res[o:o + HEAD_B] + res[o + HEAD_B:o + 2 * HEAD_B]
            y1 = res[o + 2 * HEAD_B:o + 3 * HEAD_B]
            vb = res[o + 3 * HEAD_B:o + 4 * HEAD_B]
            st = s_scr[b, p]
            yb = y1 + sa * br_t[ci:ci + 1] + vb * kr_t[ci:ci + 1]
            s_scr[b, p] = st * w_t[ci:ci + 1] + sa * be_t[ci:ci + 1] + vb * k_t[ci:ci + 1]
            y_rows.append(jnp.sum(diag * yb, axis=0, keepdims=True))
        y_s[rows, :] = jnp.concatenate(y_rows, axis=0)
        return carry

    lax.fori_loop(0, n_steps, step, 0)

    inv_n = 1.0 / HEAD_B
    for b in range(nb):
        y = jnp.concatenate([y_s[pl.ds(b * n_pairs + p, tc, stride=n_chains), :] for p in range(n_pairs)], axis=1)
        mu = _dot_l2(y, q_heads) * inv_n
        d = y - mu
        var = _dot_l2(d * d, q_heads) * inv_n
        yn = d * lax.rsqrt(var + GROUPNORM_EPS) * lg_ref[...] + lb_ref[...]
        o_ref[b] = ((yn + bo_s[b]) * g_s[b]).astype(BF16)
    sout_ref[...] = s_scr[...]


def _rwkv(z3, shift_prev, s0p, prm, seq_len, tc):
    bt, lp, n_rwkv = z3.shape
    nb = 4
    c_b = prm["w0"].shape[1]
    n_pairs = c_b // LANES
    full = lambda a: pl.BlockSpec(a.shape, lambda g, c: (0,) * a.ndim)
    names = ["mu", "w0", "w2", "a0", "a2", "g2", "kkw", "ka", "rk", "lg", "lb"]
    plist = [prm[n] for n in names]
    assert nb * n_pairs == SUBLANES, "one time-major tile row per (sequence, head pair) chain"
    chunk = lambda: pltpu.VMEM((nb, tc, c_b), F32)
    tiles = lambda: pltpu.VMEM((tc * SUBLANES, LANES), F32)
    return pl.pallas_call(
        functools.partial(_rwkv_body, nb=nb, tc=tc, seq_len=seq_len, c_b=c_b),
        grid=(bt // nb, lp // tc),
        in_specs=[pl.BlockSpec((nb, tc, n_rwkv), lambda g, c: (g, c, 0)),
                  pl.BlockSpec((nb, 1, n_rwkv), lambda g, c: (g, 0, 0)),
                  pl.BlockSpec((nb, n_pairs, HEAD_B, LANES), lambda g, c: (g, 0, 0, 0))]
                 + [full(a) for a in plist],
        out_specs=[pl.BlockSpec((nb, tc, c_b), lambda g, c: (g, c, 0)),
                   pl.BlockSpec((nb, n_pairs, HEAD_B, LANES), lambda g, c: (g, 0, 0, 0))],
        out_shape=[jax.ShapeDtypeStruct((bt, lp, c_b), BF16),
                   jax.ShapeDtypeStruct((bt, n_pairs, HEAD_B, LANES), F32)],
        scratch_shapes=[pltpu.VMEM((nb, n_pairs, HEAD_B, LANES), F32),
                        pltpu.VMEM((nb, 1, n_rwkv), F32)] + [tiles() for _ in range(8)]
                       + [chunk(), chunk(), tiles()],
        compiler_params=_cparams("parallel", "arbitrary"),
        name="rwkv7",
    )(z3, shift_prev, s0p, *plist)


def _s5_prep_body(are_ref, aim_ref, ldt_ref, bre_ref, bim_ref, abre_ref, abim_ref, bbre_ref, bbim_ref):
    a_re = are_ref[...]
    a_im = aim_ref[...]
    step = jnp.exp(ldt_ref[...])
    mag = jnp.exp(step * a_re)
    ab_re = mag * jnp.cos(step * a_im)
    ab_im = mag * jnp.sin(step * a_im)
    den = a_re * a_re + a_im * a_im
    cf_re = ((ab_re - 1.0) * a_re + ab_im * a_im) / den
    cf_im = (ab_im * a_re - (ab_re - 1.0) * a_im) / den
    abre_ref[...] = ab_re
    abim_ref[...] = ab_im
    bbre_ref[...] = cf_re * bre_ref[...] - cf_im * bim_ref[...]
    bbim_ref[...] = cf_re * bim_ref[...] + cf_im * bre_ref[...]


def _s5_prep(a_re, a_im, log_dt, b_re, b_im):
    g, p = a_re.shape
    grp = b_re.shape[2]
    r3 = lambda x: x.reshape(g, 1, p)
    ldt = jnp.broadcast_to(log_dt[:, None, None], (g, 1, p))
    bt = lambda x: jnp.transpose(x, (0, 2, 1))
    vec = jax.ShapeDtypeStruct((g, 1, p), F32)
    mat = jax.ShapeDtypeStruct((g, grp, p), F32)
    return pl.pallas_call(_s5_prep_body, out_shape=[vec, vec, mat, mat], name="s5_prep")(
        r3(a_re), r3(a_im), ldt, bt(b_re), bt(b_im))


def _gelu_tanh(x):
    return 0.5 * x * (1.0 + jnp.tanh(0.7978845608028654 * (x + 0.044715 * (x * x * x))))


def _s5_body(u_ref, re0_ref, im0_ref, bb_ref, ab_ref, cre_ref, cim_ref, d_ref, gw_ref, gb_ref,
             o_ref, reo_ref, imo_ref, xr_scr, xi_scr, u_s, xr_s, xi_s, o_s,
             *, nb, tc, seq_len, precise):
    c = pl.program_id(1)
    n_state = ab_ref.shape[1]

    @pl.when(c == 0)
    def _():
        xr_scr[...] = re0_ref[0]
        xi_scr[...] = im0_ref[0]
        if nb < SUBLANES:
            u_s[...] = jnp.zeros_like(u_s)

    n_lane_tiles = u_s.shape[0]
    for b in range(nb):
        ub = u_ref[b]
        for j in range(n_lane_tiles):
            u_s[j, pl.ds(b, tc, stride=SUBLANES), :] = ub[:, j * LANES:(j + 1) * LANES]
    u = jnp.concatenate([u_s[j] for j in range(n_lane_tiles)], axis=1)
    for half, dst in enumerate((xr_s, xi_s)):
        cols = slice(half * n_state, (half + 1) * n_state)
        if precise:
            uh, ul = _split2(u)
            dst[...] = _dot(uh, bb_ref[0, :, cols]) + _dot(ul, bb_ref[0, :, cols]) + _dot(uh, bb_ref[1, :, cols])
        else:
            dst[...] = _dot(u.astype(BF16), bb_ref[0, :, cols])
    ar = ab_ref[0:1, :]
    ai = ab_ref[1:2, :]

    def step(t, carry):
        xr, xi = carry
        rows = pl.ds(pl.multiple_of(t * SUBLANES, SUBLANES), SUBLANES)
        nr = ar * xr - ai * xi + xr_s[rows, :]
        ni = ar * xi + ai * xr + xi_s[rows, :]
        xr_s[rows, :] = nr
        xi_s[rows, :] = ni
        return nr, ni

    n_steps = jnp.clip(seq_len - c * tc, 0, tc)
    xr, xi = lax.fori_loop(0, n_steps, step, (xr_scr[...], xi_scr[...]))
    xr_scr[...] = xr
    xi_scr[...] = xi
    reo_ref[0] = xr
    imo_ref[0] = xi
    y = (_dot(xr_s[...].astype(BF16), cre_ref[...]) - _dot(xi_s[...].astype(BF16), cim_ref[...])
         + d_ref[...] * u)
    hg = _gelu_tanh(y)
    out = hg * _sigmoid(_dot_x3(hg, gw_ref[...]) + gb_ref[...])
    for j in range(n_lane_tiles):
        o_s[j] = out[:, j * LANES:(j + 1) * LANES]
    for b in range(nb):
        o_ref[b] = jnp.concatenate([o_s[j, pl.ds(b, tc, stride=SUBLANES), :] for j in range(n_lane_tiles)],
                                   axis=1).astype(BF16)


def _s5(u3, re0, im0, prm, seq_len, nb, tc, precise):
    bt, lp, c_c = u3.shape
    n_state = re0.shape[2]
    full = lambda a: pl.BlockSpec(a.shape, lambda g, c: (0,) * a.ndim)
    plist = [prm[n] for n in ("bb", "ab", "cre", "cim", "d", "gw", "gb")]
    st = pl.BlockSpec((1, SUBLANES, n_state), lambda g, c: (g, 0, 0))
    st_shape = jax.ShapeDtypeStruct((bt // nb, SUBLANES, n_state), F32)
    rows = tc * SUBLANES
    return pl.pallas_call(
        functools.partial(_s5_body, nb=nb, tc=tc, seq_len=seq_len, precise=precise),
        grid=(bt // nb, lp // tc),
        in_specs=[pl.BlockSpec((nb, tc, c_c), lambda g, c: (g, c, 0)), st, st] + [full(a) for a in plist],
        out_specs=[pl.BlockSpec((nb, tc, c_c), lambda g, c: (g, c, 0)), st, st],
        out_shape=[jax.ShapeDtypeStruct((bt, lp, c_c), BF16), st_shape, st_shape],
        scratch_shapes=[pltpu.VMEM((SUBLANES, n_state), F32), pltpu.VMEM((SUBLANES, n_state), F32),
                        pltpu.VMEM((c_c // LANES, rows, LANES), F32), pltpu.VMEM((rows, n_state), F32),
                        pltpu.VMEM((rows, n_state), F32), pltpu.VMEM((c_c // LANES, rows, LANES), F32)],
        compiler_params=_cparams("parallel", "arbitrary"),
        name="s5",
    )(u3, re0, im0, *plist)


def _merge_body(h_ref, oa_ref, ob_ref, oc_ref, g_ref, wg_ref, pa_ref, pb_ref, pc_ref, wo_ref, o_ref, *, d):
    h = h_ref[...]
    xn = _rms(h, g_ref[...]).astype(BF16)
    merged = None
    for n, (br_ref, pj_ref) in enumerate(((oa_ref, pa_ref), (ob_ref, pb_ref), (oc_ref, pc_ref))):
        gate = _sigmoid(_dot(xn, wg_ref[:, n * d:(n + 1) * d]))
        term = gate * _dot(br_ref[...], pj_ref[...])
        merged = term if merged is None else merged + term
    o_ref[...] = h + _dot(merged.astype(BF16), wo_ref[...])


def _merge(h2, oa, ob, oc, g1, wg, pa, pb, pc, wo):
    rows, d = h2.shape
    tm = _row_tile(rows, 512)
    row = lambda a: pl.BlockSpec((tm, a.shape[1]), lambda i: (i, 0))
    full = lambda a: pl.BlockSpec(a.shape, lambda i: (0,) * a.ndim)
    return pl.pallas_call(
        functools.partial(_merge_body, d=d),
        grid=(rows // tm,),
        in_specs=[row(h2), row(oa), row(ob), row(oc)] + [full(a) for a in (g1, wg, pa, pb, pc, wo)],
        out_specs=row(h2),
        out_shape=jax.ShapeDtypeStruct(h2.shape, F32),
        compiler_params=_cparams("parallel"),
        name="merge_outproj",
    )(h2, oa, ob, oc, g1, wg, pa, pb, pc, wo)


PREV_ROWS = 16


def _ffn_body(*refs, tm, seq_rows, seq_valid, decode):
    if decode:
        (x_ref, pa_ref, pb_ref, g_ref, ua_ref, ub_ref, cwa_ref, cwb_ref, cba_ref, cbb_ref, dn_ref,
         o_ref, xn_scr) = refs
    else:
        (x_ref, xp_ref, g_ref, ua_ref, ub_ref, cwa_ref, cwb_ref, cba_ref, cbb_ref, dn_ref,
         o_ref, xn_scr) = refs
    i = pl.program_id(0)
    j = pl.program_id(1)

    @pl.when(j == 0)
    def _():
        o_ref[...] = x_ref[...]
        if decode:
            xn_scr[...] = _rms(x_ref[...], g_ref[...]).astype(BF16)
        else:
            d = x_ref.shape[1]
            pos_p = (i * tm - PREV_ROWS + _iota((PREV_ROWS, d), 0)) % seq_rows
            keep = (pos_p < seq_valid) & (i > 0)
            xn_scr[0:PREV_ROWS] = jnp.where(keep, _rms(xp_ref[...], g_ref[...]), 0.0).astype(BF16)
            pos = (i * tm + _iota((tm, d), 0)) % seq_rows
            xn_scr[PREV_ROWS:] = jnp.where(pos < seq_valid, _rms(x_ref[...], g_ref[...]), 0.0).astype(BF16)

    nc = ua_ref.shape[1]
    xn = xn_scr[...]

    def conv(up_ref, cw_ref, cb_ref, prev_ref):
        h_ext = _dot(xn, up_ref[...])
        if decode:
            hu = h_ext
            prev = prev_ref[...]
            pos = _iota((tm, nc), 0) % seq_rows
            sh1 = jnp.where(pos == 0, pltpu.roll(prev, tm - 1, 0), pltpu.roll(hu, 1, 0))
            sh2 = jnp.where(pos < 2, prev, pltpu.roll(hu, 2, 0))
        else:
            hu = h_ext[PREV_ROWS:]
            last = h_ext[PREV_ROWS - 1:PREV_ROWS]
            last2 = h_ext[PREV_ROWS - 2:PREV_ROWS - 1]
            r1 = pltpu.roll(hu, 1, 0)
            r2 = pltpu.roll(hu, 2, 0)
            row8 = _iota((SUBLANES, nc), 0)
            head1 = jnp.where(row8 == 0, last, r1[0:SUBLANES])
            head2 = jnp.where(row8 == 0, last2, jnp.where(row8 == 1, last, r2[0:SUBLANES]))
            sh1 = jnp.concatenate([head1, r1[SUBLANES:]], axis=0)
            sh2 = jnp.concatenate([head2, r2[SUBLANES:]], axis=0)
        return cw_ref[0:1, :] * sh2 + cw_ref[1:2, :] * sh1 + cw_ref[2:3, :] * hu + cb_ref[...]

    a = conv(ua_ref, cwa_ref, cba_ref, pa_ref if decode else None)
    bgate = conv(ub_ref, cwb_ref, cbb_ref, pb_ref if decode else None)
    act = (a * _sigmoid(a)) * bgate
    o_ref[...] += _dot(act.astype(BF16), dn_ref[...])


def _ffn(h2, g2, ua, ub, cwa, cwb, cba, cbb, dn, seq_rows, seq_valid, prev_pad=None):
    assert prev_pad is not None or seq_rows - seq_valid >= CONV_TAPS - 1
    rows, d = h2.shape
    d_ff = ua.shape[1]
    nc = 256
    n_chunks = d_ff // nc
    decode = prev_pad is not None
    tm = _row_tile(rows, 1024)
    xrow = pl.BlockSpec((tm, d), lambda i, j: (i, 0))
    col = lambda r: pl.BlockSpec((r, nc), lambda i, j: (0, j))
    w_specs = [pl.BlockSpec((1, d), lambda i, j: (0, 0)), col(d), col(d), col(CONV_TAPS), col(CONV_TAPS),
               col(1), col(1), pl.BlockSpec((nc, d), lambda i, j: (j, 0))]
    scratch = [pltpu.VMEM((tm if decode else tm + PREV_ROWS, d), BF16)]
    if decode:
        in_specs = [xrow, pl.BlockSpec((tm, nc), lambda i, j: (i, j)),
                    pl.BlockSpec((tm, nc), lambda i, j: (i, n_chunks + j))] + w_specs
        args = (h2, prev_pad, prev_pad, g2, ua, ub, cwa, cwb, cba, cbb, dn)
    else:
        per = tm // PREV_ROWS
        in_specs = [xrow, pl.BlockSpec((PREV_ROWS, d), lambda i, j: (jnp.maximum(i * per - 1, 0), 0))] + w_specs
        args = (h2, h2, g2, ua, ub, cwa, cwb, cba, cbb, dn)
    return pl.pallas_call(
        functools.partial(_ffn_body, tm=tm, seq_rows=seq_rows, seq_valid=seq_valid, decode=decode),
        grid=(rows // tm, n_chunks),
        in_specs=in_specs,
        out_specs=xrow,
        out_shape=jax.ShapeDtypeStruct(h2.shape, F32),
        scratch_shapes=scratch,
        compiler_params=_cparams("parallel", "arbitrary"),
        name="conv_ffn",
    )(*args)


def _up_rows_body(x_ref, g_ref, up_ref, o_ref):
    o_ref[...] = _dot(_rms(x_ref[...], g_ref[...]).astype(BF16), up_ref[...])


def _up_rows(x2, g2, up):
    rows, d = x2.shape
    n = up.shape[1]
    nc = 512
    return pl.pallas_call(
        _up_rows_body,
        grid=(n // nc,),
        in_specs=[pl.BlockSpec((rows, d), lambda j: (0, 0)), pl.BlockSpec((1, d), lambda j: (0, 0)),
                  pl.BlockSpec((d, nc), lambda j: (0, j))],
        out_specs=pl.BlockSpec((rows, nc), lambda j: (0, j)),
        out_shape=jax.ShapeDtypeStruct((rows, n), F32),
        compiler_params=_cparams("parallel"),
        name="conv_state_rows",
    )(x2, g2, up)


def _final_norm_body(x_ref, g_ref, o_ref):
    o_ref[...] = _rms(x_ref[...], g_ref[...])


def _final_norm(h2, g):
    rows, d = h2.shape
    tm = _row_tile(rows, 1024)
    spec = pl.BlockSpec((tm, d), lambda i: (i, 0))
    return pl.pallas_call(
        _final_norm_body,
        grid=(rows // tm,),
        in_specs=[spec, pl.BlockSpec((1, d), lambda i: (0, 0))],
        out_specs=spec,
        out_shape=jax.ShapeDtypeStruct(h2.shape, F32),
        compiler_params=_cparams("parallel"),
        name="final_norm",
    )(h2, g)


def _pack_state(s):
    b, h, v, k = s.shape
    return s.reshape(b, h // 2, 2, v, k).transpose(0, 1, 3, 2, 4).reshape(b, h // 2, v, 2 * k)


def _unpack_state(s):
    b, hp, v, k2 = s.shape
    return s.reshape(b, hp, v, 2, k2 // 2).transpose(0, 1, 3, 2, 4).reshape(b, 2 * hp, v, k2 // 2)


def _block_diag(x):
    g, a, c = x.shape
    eye = jnp.eye(g, dtype=bool)[:, None, :, None]
    return jnp.where(eye, x[:, :, None, :], 0.0).reshape(g * a, g * c)


def _pad_rows(x, n):
    return jnp.concatenate([x, jnp.zeros((n - x.shape[0],) + x.shape[1:], x.dtype)], axis=0)


def kernel(x_prompt, x_sample, cache_k, cache_v, cache_logf, page_table, state_rwkv_wkv, state_rwkv_shift, state_ssm_re, state_ssm_im, state_ffn_conv, meta_tokens, norm1_g, w_in, fox_bf, rwkv_mu, rwkv_w0, rwkv_w2, rwkv_a0, rwkv_a2, rwkv_g2, rwkv_kk, rwkv_ka, rwkv_rk, rwkv_lnx_g, rwkv_lnx_b, ssm_a_re, ssm_a_im, ssm_log_dt, ssm_b_re, ssm_b_im, ssm_c_re, ssm_c_im, ssm_d, ssm_glu_w, ssm_glu_b, proj_a, proj_b, proj_c, w_out, norm2_g, ffn_up, ffn_conv_w, ffn_conv_b, ffn_down, final_norm_g):
    bsz, seq, d = x_prompt.shape
    db, t_new, _ = x_sample.shape
    depth = norm1_g.shape[0]
    n_ha = fox_bf.shape[1]
    c_a = n_ha * HEAD_A
    n_rwkv = rwkv_mu.shape[1]
    c_b = rwkv_w0.shape[1]
    n_hb = c_b // HEAD_B
    g_c, p_c = ssm_a_re.shape[1], ssm_a_re.shape[2]
    c_c = g_c * SSM_GRP
    n_state = g_c * p_c
    d_ff = ffn_down.shape[1]
    n_pool, page = cache_k.shape[1], cache_k.shape[2]
    seq_len = N_META_TOK + seq
    lp = -(-seq_len // 256) * 256
    seq_tile = 256

    meta = jnp.broadcast_to(meta_tokens[None], (bsz, N_META_TOK, d))
    h_p = jnp.concatenate([meta, x_prompt, jnp.zeros((bsz, lp - seq_len, d), F32)], axis=1)
    h_p = h_p.reshape(bsz * lp, d)
    h_s = x_sample.reshape(db * t_new, d)

    cache_kt = jnp.transpose(cache_k, (0, 1, 3, 4, 2))
    cache_vt = jnp.transpose(cache_v, (0, 1, 3, 4, 2))
    eye_h = jnp.eye(n_ha, dtype=bool)

    zeros_shift = jnp.zeros((bsz, 1, n_rwkv), F32)
    zeros_wkv = jnp.zeros((bsz, n_hb // 2, HEAD_B, 2 * HEAD_B), F32)
    assert bsz <= SUBLANES and db % SUBLANES == 0
    zeros_ssm = jnp.zeros((1, SUBLANES, n_state), F32)

    outs_p, outs_s = [], []
    for l in range(depth):
        w = w_in[l]
        o_f = 3 * c_a
        o_z = o_f + n_ha
        o_g = o_z + n_rwkv + c_c
        wp = jnp.concatenate([w[:, :o_f], w[:, o_f:o_z], jnp.zeros((d, LANES - n_ha), F32),
                              w[:, o_z:o_g]], axis=1).astype(BF16)
        wg = w[:, o_g:].astype(BF16)
        bfp = jnp.concatenate([fox_bf[l], jnp.zeros((LANES - n_ha,), F32)])[None]
        g1 = norm1_g[l][None]
        g2 = norm2_g[l][None]
        row1 = lambda x: x.reshape(1, -1)
        lora_pad = lambda x, o: jnp.zeros((n_rwkv - 3 * c_b, c_b), F32).at[o:o + x.shape[0]].set(x)
        rw = dict(mu=row1(rwkv_mu[l]), w0=row1(rwkv_w0[l]), w2=lora_pad(rwkv_w2[l], 0),
                  a0=row1(rwkv_a0[l]), a2=lora_pad(rwkv_a2[l], LORA_W_DIM),
                  g2=lora_pad(rwkv_g2[l], LORA_W_DIM + LORA_A_DIM), kkw=row1(rwkv_kk[l]),
                  ka=row1(rwkv_ka[l]), rk=row1(rwkv_rk[l]), lg=row1(rwkv_lnx_g[l]), lb=row1(rwkv_lnx_b[l]))
        ab_re, ab_im, bbt_re, bbt_im = _s5_prep(ssm_a_re[l], ssm_a_im[l], ssm_log_dt[l], ssm_b_re[l], ssm_b_im[l])
        bb = jnp.concatenate([_block_diag(bbt_re), _block_diag(bbt_im)], axis=1)
        bb_hi = bb.astype(BF16)
        bb_lo = (bb - bb_hi.astype(F32)).astype(BF16)
        s5p = dict(bb=jnp.stack([bb_hi, bb_lo]),
                   ab=jnp.concatenate([ab_re.reshape(1, n_state), ab_im.reshape(1, n_state)], axis=0),
                   cre=_block_diag(jnp.transpose(ssm_c_re[l], (0, 2, 1))).astype(BF16),
                   cim=_block_diag(jnp.transpose(ssm_c_im[l], (0, 2, 1))).astype(BF16),
                   d=row1(ssm_d[l]), gw=ssm_glu_w[l], gb=row1(ssm_glu_b[l]))
        pa, pb, pc, wo = (x[l].astype(BF16) for x in (proj_a, proj_b, proj_c, w_out))
        up = ffn_up[l].astype(BF16)
        ua, ub = up[:, :d_ff], up[:, d_ff:]
        cwa, cwb = ffn_conv_w[l][:, :d_ff], ffn_conv_w[l][:, d_ff:]
        cba, cbb = ffn_conv_b[l][None, :d_ff], ffn_conv_b[l][None, d_ff:]
        dn = ffn_down[l].astype(BF16)

        qb, k, v, kb, vb, lf, z, u = _inproj(h_p, g1, wp, bfp, c_a, n_rwkv, c_c)
        r3 = lambda x: x.reshape(bsz, lp, x.shape[1])
        o_a = _attn_prompt(*_fox_prep(r3(lf), r3(qb), r3(kb), r3(vb)), lp)
        o_b, wkv = _rwkv(r3(z), zeros_shift, zeros_wkv, rw, seq_len, seq_tile)
        o_c, re_n, im_n = _s5(r3(u), zeros_ssm, zeros_ssm, s5p, seq_len, bsz, seq_tile // 2, False)
        flat = lambda x: x.reshape(bsz * lp, x.shape[2])
        h_mid = _merge(h_p, flat(o_a), flat(o_b), flat(o_c), g1, wg, pa, pb, pc, wo)
        h_p = _ffn(h_mid, g2, ua, ub, cwa, cwb, cba, cbb, dn, lp, seq_len)
        tail = r3(h_mid)[:, seq_len - 2:seq_len].reshape(bsz * 2, d)
        conv = _up_rows(_pad_rows(tail, 16), g2, up)[:bsz * 2].reshape(bsz, 2, 2 * d_ff)
        outs_p.append((r3(k)[:, :seq_len].reshape(bsz, seq_len, n_ha, HEAD_A),
                       r3(v)[:, :seq_len].reshape(bsz, seq_len, n_ha, HEAD_A),
                       r3(lf)[:, :seq_len, :n_ha], _unpack_state(wkv), r3(z)[:, seq_len - 1],
                       re_n[0, :bsz].reshape(bsz, g_c, p_c), im_n[0, :bsz].reshape(bsz, g_c, p_c), conv))

        qb, k, v, kb, vb, lf, z, u = _inproj(h_s, g1, wp, bfp, c_a, n_rwkv, c_c)
        s3 = lambda x: x.reshape(db, t_new, x.shape[1])
        q4 = s3(qb).reshape(db, t_new, n_ha, HEAD_A)
        qbd = jnp.where(eye_h[None, None, :, :, None], q4[:, :, :, None, :], jnp.zeros((), BF16))
        qbd = qbd.reshape(db, t_new * n_ha, c_a)
        pad_new = lambda x: jnp.concatenate([s3(x), jnp.zeros((db, LANES - t_new, c_a), BF16)], axis=1)
        suf, tot = _pool_suffix(jnp.transpose(cache_logf[l], (0, 2, 1)).reshape(n_pool * n_ha, page))
        o_a = _attn_sample(page_table, l, qbd, pad_new(kb), pad_new(vb), s3(lf), cache_kt, cache_vt,
                           suf.reshape(n_pool, n_ha, page), tot.reshape(n_pool, n_ha, page))
        o_b, wkv = _rwkv(s3(z), state_rwkv_shift[l][:, None], _pack_state(state_rwkv_wkv[l]), rw, t_new, t_new)
        o_c, re_n, im_n = _s5(s3(u), state_ssm_re[l].reshape(db // SUBLANES, SUBLANES, n_state),
                              state_ssm_im[l].reshape(db // SUBLANES, SUBLANES, n_state), s5p, t_new,
                              SUBLANES, t_new, True)
        flat = lambda x: x.reshape(db * t_new, x.shape[2])
        h_mid = _merge(h_s, flat(o_a), flat(o_b), flat(o_c), g1, wg, pa, pb, pc, wo)
        prev = state_ffn_conv[l]
        prev_pad = jnp.concatenate([prev, jnp.zeros((db, t_new - prev.shape[1], prev.shape[2]), F32)], axis=1)
        h_s = _ffn(h_mid, g2, ua, ub, cwa, cwb, cba, cbb, dn, t_new, t_new,
                   prev_pad.reshape(db * t_new, 2 * d_ff))
        tail = s3(h_mid)[:, t_new - 2:].reshape(db * 2, d)
        conv = _up_rows(tail, g2, up).reshape(db, 2, 2 * d_ff)
        outs_s.append((s3(k).reshape(db, t_new, n_ha, HEAD_A), s3(v).reshape(db, t_new, n_ha, HEAD_A),
                       s3(lf)[:, :, :n_ha], _unpack_state(wkv), s3(z)[:, t_new - 1],
                       re_n.reshape(db, g_c, p_c), im_n.reshape(db, g_c, p_c), conv))

    fg = final_norm_g[None]
    y_p = _final_norm(h_p, fg).reshape(bsz, lp, d)[:, N_META_TOK:seq_len]
    y_s = _final_norm(h_s, fg).reshape(db, t_new, d)
    stack = lambda outs: [jnp.stack(t) for t in zip(*outs)]
    return tuple([y_p, y_s] + stack(outs_p) + stack(outs_s))
```

```python
import functools

import jax
import jax.numpy as jnp
from jax import lax
from jax.experimental import pallas as pl
from jax.experimental.pallas import tpu as pltpu

F32 = jnp.float32
BF16 = jnp.bfloat16

N_META_TOK = 16
RMS_EPS = 1e-6
HEAD_A = 64
HEAD_B = 64
LORA_W_DIM = 64
LORA_A_DIM = 64
LORA_G_DIM = 128
GROUPNORM_EPS = 64e-5
SSM_GRP = 16
SSM_P = 64
CONV_TAPS = 3
LANES = 128
SUBLANES = 8
VMEM_LIMIT = 52 * 1024 * 1024


def _cparams(*sem):
    return pltpu.CompilerParams(dimension_semantics=sem, vmem_limit_bytes=VMEM_LIMIT)


def _dot(a, b):
    return jnp.dot(a, b, preferred_element_type=F32)


def _dot_nt(a, b):
    return lax.dot_general(a, b, (((1,), (1,)), ((), ())), preferred_element_type=F32)


def _split2(x):
    hi = x.astype(BF16)
    lo = (x - hi.astype(F32)).astype(BF16)
    return hi, lo


def _split3(x):
    hi = x.astype(BF16)
    r = x - hi.astype(F32)
    mid = r.astype(BF16)
    lo = (r - mid.astype(F32)).astype(BF16)
    return hi, mid, lo


def _dot_l2(x, w_bf):
    hi, lo = _split2(x)
    return _dot(hi, w_bf) + _dot(lo, w_bf)


def _dot_l3(x, w_bf):
    hi, mid, lo = _split3(x)
    return _dot(hi, w_bf) + _dot(mid, w_bf) + _dot(lo, w_bf)


def _dot_r3(w_bf, x):
    hi, mid, lo = _split3(x)
    return _dot(w_bf, hi) + _dot(w_bf, mid) + _dot(w_bf, lo)


def _dot_x3(a, b):
    ah, al = _split2(a)
    bh, bl = _split2(b)
    return _dot(ah, bh) + _dot(al, bh) + _dot(ah, bl)


def _sigmoid(x):
    return 1.0 / (1.0 + jnp.exp(-x))


def _log_sigmoid(x):
    return jnp.minimum(x, 0.0) - jnp.log1p(jnp.exp(-jnp.abs(x)))


def _rms(x, g):
    ms = jnp.mean(x * x, axis=-1, keepdims=True)
    return x * lax.rsqrt(ms + RMS_EPS) * g


def _iota(shape, dim):
    return lax.broadcasted_iota(jnp.int32, shape, dim)


def _block_ones(n, blk):
    return jnp.where(_iota((n, n), 0) // blk == _iota((n, n), 1) // blk, 1.0, 0.0).astype(BF16)


def _row_tile(rows, cap):
    t = cap
    while rows % t:
        t //= 2
    return t


def _inproj_body(h_ref, g_ref, w_ref, bf_ref, qb_ref, k_ref, v_ref, kb_ref, vb_ref, lf_ref, z_ref,
                 u_ref, *, c_a, n_rwkv, c_c):
    xn = _rms(h_ref[...], g_ref[...]).astype(BF16)
    q = _dot(xn, w_ref[:, 0:c_a])
    qb_ref[...] = (q * (HEAD_A ** -0.5)).astype(BF16)
    k = _dot(xn, w_ref[:, c_a:2 * c_a])
    k_ref[...] = k
    kb_ref[...] = k.astype(BF16)
    v = _dot(xn, w_ref[:, 2 * c_a:3 * c_a])
    v_ref[...] = v
    vb_ref[...] = v.astype(BF16)
    o = 3 * c_a
    f = _dot(xn, w_ref[:, o:o + LANES]) + bf_ref[...]
    lf_ref[...] = _log_sigmoid(f)
    o += LANES
    z_ref[...] = _dot(xn, w_ref[:, o:o + n_rwkv])
    o += n_rwkv
    u_ref[...] = _dot(xn, w_ref[:, o:o + c_c])


def _inproj(h2, g1, wp, bfp, c_a, n_rwkv, c_c):
    rows, d = h2.shape
    tm = _row_tile(rows, 512)
    npj = wp.shape[1]
    row = lambda n: pl.BlockSpec((tm, n), lambda i: (i, 0))
    full = lambda a: pl.BlockSpec(a.shape, lambda i: (0,) * a.ndim)
    outs = [(c_a, BF16), (c_a, F32), (c_a, F32), (c_a, BF16), (c_a, BF16), (LANES, F32),
            (n_rwkv, F32), (c_c, F32)]
    return pl.pallas_call(
        functools.partial(_inproj_body, c_a=c_a, n_rwkv=n_rwkv, c_c=c_c),
        grid=(rows // tm,),
        in_specs=[row(d), full(g1), full(wp), full(bfp)],
        out_specs=[row(n) for n, _ in outs],
        out_shape=[jax.ShapeDtypeStruct((rows, n), dt) for n, dt in outs],
        compiler_params=_cparams("parallel"),
        name="inproj",
    )(h2, g1, wp, bfp)


N_BIAS = 3


def _fox_prep_body(lf_ref, q_ref, k_ref, v_ref, qx_ref, kx_ref, vx_ref, carry_ref, *, tc, n_heads, n_valid):
    @pl.when(pl.program_id(1) == 0)
    def _():
        carry_ref[...] = jnp.zeros_like(carry_ref)

    @pl.when(pl.program_id(1) >= n_valid)
    def _():
        for ref in (qx_ref, kx_ref, vx_ref):
            ref[...] = jnp.zeros_like(ref)

    @pl.when(pl.program_id(1) < n_valid)
    def _():
        _fox_prep_tile(lf_ref, q_ref, k_ref, v_ref, qx_ref, kx_ref, vx_ref, carry_ref, tc, n_heads)


def _fox_prep_tile(lf_ref, q_ref, k_ref, v_ref, qx_ref, kx_ref, vx_ref, carry_ref, tc, n_heads):
    tri = jnp.where(_iota((tc, tc), 0) >= _iota((tc, tc), 1), 1.0, 0.0).astype(BF16)
    cs = _dot_r3(tri, lf_ref[0]) + carry_ref[...]
    carry_ref[...] = cs[tc - 1:tc]
    pieces = jnp.concatenate(_split3(cs), axis=1)

    nl = n_heads * LANES
    src = _iota((N_BIAS * LANES, nl), 0)
    dst = _iota((N_BIAS * LANES, nl), 1)
    base = jnp.where((dst // LANES) % 2 == 0, HEAD_A, 0)
    same_head = src % LANES == dst // LANES
    off = dst % LANES - base
    place_q = jnp.where(same_head & (off == src // LANES), 1.0, 0.0).astype(BF16)
    place_k = jnp.where(same_head & (off == N_BIAS + src // LANES), -1.0, 0.0).astype(BF16)
    off1 = off[0:1]
    qe = _dot(pieces, place_q) + jnp.where((off1 >= N_BIAS) & (off1 < 2 * N_BIAS), 1.0, 0.0)
    ke = _dot(pieces, place_k) + jnp.where((off1 >= 0) & (off1 < N_BIAS), 1.0, 0.0)
    lane_half = _iota((tc, LANES), 1) // HEAD_A
    ones = jnp.ones((tc, LANES), BF16)
    for h in range(n_heads):
        pair = slice((h // 2) * LANES, (h // 2 + 1) * LANES)
        tile = slice(h * LANES, (h + 1) * LANES)
        own = lane_half == h % 2
        qx_ref[0, h] = jnp.where(own, q_ref[0][:, pair], qe[:, tile].astype(BF16))
        kx_ref[0, h] = jnp.where(own, k_ref[0][:, pair], ke[:, tile].astype(BF16))
        vx_ref[0, h] = jnp.where(own, v_ref[0][:, pair], ones)


ATTN_TQ = 256
ATTN_TK = 1024


def _fox_prep(lf3, qb, kb, vb):
    b, lp, c_a = qb.shape
    n_heads = c_a // HEAD_A
    tc = 256
    n_valid = lp // tc
    lkv = -(-lp // ATTN_TK) * ATTN_TK
    row = lambda n: pl.BlockSpec((1, tc, n), lambda i, j: (i, jnp.minimum(j, n_valid - 1), 0))
    out = pl.BlockSpec((1, n_heads, tc, LANES), lambda i, j: (i, 0, j, 0))
    return pl.pallas_call(
        functools.partial(_fox_prep_body, tc=tc, n_heads=n_heads, n_valid=n_valid),
        grid=(b, lkv // tc),
        in_specs=[row(LANES), row(c_a), row(c_a), row(c_a)],
        out_specs=[out, out, out],
        out_shape=[jax.ShapeDtypeStruct((b, n_heads, lkv, LANES), BF16)] * 3,
        scratch_shapes=[pltpu.VMEM((1, LANES), F32)],
        compiler_params=_cparams("parallel", "arbitrary"),
        name="fox_prep",
    )(lf3, qb, kb, vb)


def _attn_body(qx_ref, kx_ref, vx_ref, o_ref, s_scr, *, tq, tk):
    row0 = pl.program_id(2) * tq
    n_full = row0 // tk
    visible = _iota((tq, tk), 1) <= _iota((tq, tk), 0) + (row0 - n_full * tk)
    q = [qx_ref[0, hh] for hh in range(2)]

    def lane_tile_max(s):
        m = s[:, 0:LANES]
        for t in range(1, tk // LANES):
            m = jnp.maximum(m, s[:, t * LANES:(t + 1) * LANES])
        return m

    def pass_max(j, ms, masked):
        start = pl.multiple_of(j * tk, tk)
        out = []
        for hh in range(2):
            s = _dot_nt(q[hh], kx_ref[0, hh, pl.ds(start, tk), :])
            if masked:
                s = jnp.where(visible, s, -jnp.inf)
            s_scr[hh, :, pl.ds(start, tk)] = s
            out.append(jnp.maximum(ms[hh], lane_tile_max(s)))
        return tuple(out)

    ms = tuple(jnp.full((tq, LANES), -jnp.inf, F32) for _ in range(2))
    ms = lax.fori_loop(0, n_full, lambda j, c: pass_max(j, c, False), ms)
    ms = pass_max(n_full, ms, True)
    mb = []
    for hh in range(2):
        m1 = jnp.broadcast_to(jnp.max(ms[hh], axis=1, keepdims=True), (tq, LANES))
        mb.append(jnp.concatenate([m1] * (tk // LANES), axis=1))

    def pass_acc(j, accs):
        start = pl.multiple_of(j * tk, tk)
        out = []
        for hh in range(2):
            p = jnp.exp(s_scr[hh, :, pl.ds(start, tk)] - mb[hh]).astype(BF16)
            out.append(accs[hh] + _dot(p, vx_ref[0, hh, pl.ds(start, tk), :]))
        return tuple(out)

    accs = tuple(jnp.zeros((tq, LANES), F32) for _ in range(2))
    accs = lax.fori_loop(0, n_full + 1, pass_acc, accs)
    outs = [acc / pltpu.roll(acc, HEAD_A, 1) for acc in accs]
    o_ref[0] = jnp.where(_iota((tq, LANES), 1) < HEAD_A, outs[0], outs[1]).astype(BF16)


def _attn_prompt(qx, kx, vx, lp):
    b, n_heads, lkv, _ = qx.shape
    tq, tk = ATTN_TQ, ATTN_TK
    kv = pl.BlockSpec((1, 2, lkv, LANES), lambda bi, p, i: (bi, p, 0, 0))
    return pl.pallas_call(
        functools.partial(_attn_body, tq=tq, tk=tk),
        grid=(b, n_heads // 2, lp // tq),
        in_specs=[pl.BlockSpec((1, 2, tq, LANES), lambda bi, p, i: (bi, p, i, 0)), kv, kv],
        out_specs=pl.BlockSpec((1, tq, LANES), lambda bi, p, i: (bi, i, p)),
        out_shape=jax.ShapeDtypeStruct((b, lp, n_heads * HEAD_A), BF16),
        scratch_shapes=[pltpu.VMEM((2, tq, lkv), F32)],
        compiler_params=_cparams("parallel", "parallel", "arbitrary"),
        name="fox_prompt_attn",
    )(qx, kx, vx)


def _pool_suffix_body(x_ref, suf_ref, tot_ref):
    n = x_ref.shape[1]
    later = jnp.where(_iota((n, n), 0) > _iota((n, n), 1), 1.0, 0.0).astype(BF16)
    whole = jnp.ones((n, n), BF16)
    hi, mid, lo = _split3(x_ref[...])
    suf_ref[...] = _dot(hi, later) + _dot(mid, later) + _dot(lo, later)
    tot_ref[...] = _dot(hi, whole) + _dot(mid, whole) + _dot(lo, whole)


def _pool_suffix(ft2):
    rows, n = ft2.shape
    tm = _row_tile(rows, 1024)
    spec = pl.BlockSpec((tm, n), lambda i: (i, 0))
    return pl.pallas_call(
        _pool_suffix_body,
        grid=(rows // tm,),
        in_specs=[spec],
        out_specs=[spec, spec],
        out_shape=[jax.ShapeDtypeStruct(ft2.shape, F32)] * 2,
        compiler_params=_cparams("parallel"),
        name="fox_pool_suffix",
    )(ft2)


def _attn_sample_body(pt_ref, qbd_ref, kn_ref, vn_ref, lfn_ref, *rest, n_pages, t_new):
    kp = rest[0:n_pages]
    vp = rest[n_pages:2 * n_pages]
    sufp = rest[2 * n_pages:3 * n_pages]
    totp = rest[3 * n_pages:4 * n_pages]
    o_ref = rest[4 * n_pages]
    nr, c_a = qbd_ref.shape[1], qbd_ref.shape[2]
    n_h = nr // t_new
    qbd = qbd_ref[0]
    row = _iota((nr, LANES), 0)
    lane = _iota((nr, LANES), 1)

    lfn_pad = jnp.concatenate([lfn_ref[0], jnp.zeros((LANES - t_new, LANES), F32)], axis=0)
    tri = jnp.where(_iota((LANES, LANES), 0) >= _iota((LANES, LANES), 1), 1.0, 0.0).astype(BF16)
    cs = _dot_r3(tri, lfn_pad)
    rep = jnp.where(_iota((nr, LANES), 1) == row // n_h, 1.0, 0.0).astype(BF16)
    cs_rows = _dot_r3(rep, cs)
    cq = jnp.sum(jnp.where(lane == row % n_h, cs_rows, 0.0), axis=1, keepdims=True)
    cst = cs.T[0:n_h]

    tiles = [None] * (n_pages + 1)
    carry = jnp.zeros((n_h, LANES), F32)
    for j in reversed(range(n_pages)):
        kt = kp[j][...].reshape(c_a, kp[j].shape[2]).astype(BF16)
        bias = sufp[j][...] + carry
        tiles[j] = _dot(qbd, kt) + cq + jnp.concatenate([bias] * t_new, axis=0)
        carry = carry + totp[j][...]
    s = _dot_nt(qbd, kn_ref[0]) + cq - jnp.concatenate([cst] * t_new, axis=0)
    tiles[n_pages] = jnp.where((lane <= row // n_h) & (lane < t_new), s, -jnp.inf)

    m = tiles[0]
    for tl in tiles[1:]:
        m = jnp.maximum(m, tl)
    m = jnp.max(m, axis=1, keepdims=True)
    lsum = jnp.zeros((nr, LANES), F32)
    acc = jnp.zeros((nr, c_a), F32)
    for j in range(n_pages + 1):
        p = jnp.exp(tiles[j] - m)
        lsum = lsum + p
        if j == n_pages:
            acc = acc + _dot(p.astype(BF16), vn_ref[0])
        else:
            vt = vp[j][...].reshape(c_a, vp[j].shape[2]).astype(BF16)
            acc = acc + _dot_nt(p.astype(BF16), vt)
    o_full = acc / jnp.sum(lsum, axis=1, keepdims=True)
    keep = _iota((nr, c_a), 1) // HEAD_A == _iota((nr, c_a), 0) % n_h
    picked = jnp.where(keep, o_full, 0.0).astype(BF16)
    gather = jnp.where((_iota((nr, nr), 1) // n_h == _iota((nr, nr), 0)), 1.0, 0.0).astype(BF16)
    o_ref[0] = _dot(gather, picked)[0:t_new].astype(BF16)


def _attn_sample(page_table, layer, qbd, kn_pad, vn_pad, lfn, cache_kt, cache_vt, suf, tot):
    db, n_pages = page_table.shape
    nr, c_a = qbd.shape[1], qbd.shape[2]
    t_new = lfn.shape[1]
    n_h, hd, page = cache_kt.shape[2:]

    def seq_spec(a):
        return pl.BlockSpec((1,) + a.shape[1:], lambda b, pt: (b,) + (0,) * (a.ndim - 1))

    def pool_spec(j):
        return pl.BlockSpec((None, None, n_h, hd, page), lambda b, pt, j=j: (layer, pt[b, j], 0, 0, 0))

    def small_spec(j):
        return pl.BlockSpec((None, n_h, page), lambda b, pt, j=j: (pt[b, j], 0, 0))

    in_specs = ([seq_spec(qbd), seq_spec(kn_pad), seq_spec(vn_pad), seq_spec(lfn)]
                + [pool_spec(j) for j in range(n_pages)] * 2
                + [small_spec(j) for j in range(n_pages)] * 2)
    grid_spec = pltpu.PrefetchScalarGridSpec(
        num_scalar_prefetch=1,
        grid=(db,),
        in_specs=in_specs,
        out_specs=pl.BlockSpec((1, t_new, c_a), lambda b, pt: (b, 0, 0)),
    )
    args = ([qbd, kn_pad, vn_pad, lfn] + [cache_kt] * n_pages + [cache_vt] * n_pages
            + [suf] * n_pages + [tot] * n_pages)
    return pl.pallas_call(
        functools.partial(_attn_sample_body, n_pages=n_pages, t_new=t_new),
        grid_spec=grid_spec,
        out_shape=jax.ShapeDtypeStruct((db, t_new, c_a), BF16),
        compiler_params=_cparams("parallel"),
        name="fox_paged_attn",
    )(page_table, *args)


BLK = SUBLANES


def _rwkv_body(z_ref, sp_ref, s0_ref, mu_ref, w0_ref, w2_ref, a0_ref, a2_ref, g2_ref, kkw_ref, ka_ref,
               rk_ref, lg_ref, lb_ref, o_ref, sout_ref,
               s_scr, carry_scr, at_s, rt_s, bt_s, kt_s, bh_s, kh_s, eg_s, v_s, g_s, bo_s, y_s,
               *, nb, tc, seq_len, c_b):
    c = pl.program_id(1)
    n_pairs = c_b // LANES

    @pl.when(c == 0)
    def _():
        s_scr[...] = s0_ref[...]
        carry_scr[...] = sp_ref[...]

    q_heads = _block_ones(c_b, HEAD_B)
    rowid = _iota((tc, z_ref.shape[2]), 0)
    lora0 = 3 * c_b
    for b in range(nb):
        z = z_ref[b]
        zprev = jnp.where(rowid == 0, carry_scr[b], pltpu.roll(z, 1, 0))
        carry_scr[b] = z[tc - 1:tc]
        zm = z + (zprev - z) * mu_ref[...]
        r = zm[:, 0:c_b]
        k = zm[:, c_b:2 * c_b]
        v = zm[:, 2 * c_b:3 * c_b]
        x_lora = zm[:, lora0:]
        w = _log_sigmoid(w0_ref[...] + _dot_x3(jnp.tanh(x_lora), w2_ref[...])) - 0.5
        a = _sigmoid(a0_ref[...] + _dot_x3(x_lora, a2_ref[...]))
        g = _dot_x3(_sigmoid(x_lora), g2_ref[...])
        kk = k * kkw_ref[...]
        kk = kk / jnp.maximum(jnp.sqrt(_dot_l2(kk * kk, q_heads)), 1e-12)
        k2 = k * (1.0 + (a - 1.0) * ka_ref[...])
        beta = kk * a
        g_s[b] = g
        bo_s[b] = _dot_l2(r * k2 * rk_ref[...], q_heads) * v
        lw = -jnp.exp(w)
        pos = _iota(lw.shape, 0) % BLK
        cl, sx = lw, lw
        shift = 1
        while shift < BLK:
            cl = cl + jnp.where(pos >= shift, pltpu.roll(cl, shift, 0), 0.0)
            sx = sx + jnp.where(pos < BLK - shift, pltpu.roll(sx, tc - shift, 0), 0.0)
            shift *= 2
        sx = sx - lw
        e_inv = jnp.exp(-cl)
        e_out = jnp.exp(sx)
        at_s[b] = -kk * jnp.exp(cl - lw)
        rt_s[b] = r * jnp.exp(cl)
        bt_s[b] = beta * e_inv
        kt_s[b] = k2 * e_inv
        bh_s[b] = beta * e_out
        kh_s[b] = k2 * e_out
        eg_s[b] = jnp.exp(cl + sx)
        v_s[b] = v

    q_pair = _block_ones(LANES, HEAD_B)
    lane = _iota((BLK, LANES), 1)
    sub = _iota((BLK, LANES), 0)
    head0 = lane < HEAD_B
    n_steps = jnp.clip(seq_len - c * tc, 0, tc)
    chains = [(b, p) for b in range(nb) for p in range(n_pairs)]
    bc = lambda x, j: jnp.broadcast_to(x[j:j + 1], x.shape)

    @pl.when(n_steps < tc)
    def _():
        y_s[...] = jnp.zeros_like(y_s)

    def block(gi, carry):
        rows = pl.ds(pl.multiple_of(gi * BLK, BLK), BLK)
        tiles, pair_hi, pair_lo, m1 = [], [], [], []
        for b, p in chains:
            cs = slice(p * LANES, (p + 1) * LANES)
            at, rt, bt, kt = at_s[b, rows, cs], rt_s[b, rows, cs], bt_s[b, rows, cs], kt_s[b, rows, cs]
            tiles.append((at, rt))
            fb, out = [], []
            for j in range(BLK):
                bj, kj = bc(bt, j), bc(kt, j)
                if j < BLK - 1:
                    fb += [jnp.where(sub > j, at * bj, 0.0), jnp.where(sub > j, at * kj, 0.0)]
                out += [jnp.where(sub >= j, rt * bj, 0.0), jnp.where(sub >= j, rt * kj, 0.0)]
            fb = jnp.concatenate(fb, axis=0)
            fb_hi = fb.astype(BF16)
            pair_hi += [fb_hi, jnp.concatenate(out, axis=0).astype(BF16)]
            pair_lo.append((fb - fb_hi.astype(F32)).astype(BF16))
            lhs = jnp.concatenate([jnp.where(head0, at, 0.0), jnp.where(head0, 0.0, at),
                                   jnp.where(head0, rt, 0.0), jnp.where(head0, 0.0, rt)], axis=0)
            lhs_hi = lhs.astype(BF16)
            lhs_lo = (lhs - lhs_hi.astype(F32)).astype(BF16)
            st_hi, st_lo = _split2(s_scr[b, p])
            z = _dot_nt(jnp.concatenate([lhs_hi, lhs_lo], axis=0), st_hi)
            z = z[0:4 * BLK] + z[4 * BLK:] + _dot_nt(lhs_hi, st_lo)
            m1.append(z)
        n_fb, n_out = 2 * (BLK - 1) * BLK, 2 * BLK * BLK
        dots_hi = _dot(jnp.concatenate(pair_hi, axis=0), q_pair)
        dots_lo = _dot(jnp.concatenate(pair_lo, axis=0), q_pair)
        for ci, (b, p) in enumerate(chains):
            cs = slice(p * LANES, (p + 1) * LANES)
            o = ci * (n_fb + n_out)
            fb = dots_hi[o:o + n_fb] + dots_lo[ci * n_fb:(ci + 1) * n_fb]
            out = dots_hi[o + n_fb:o + n_fb + n_out]
            z = m1[ci]
            vv = v_s[b, rows, cs]
            u = jnp.concatenate([z[0:BLK], z[BLK:2 * BLK]], axis=1)
            y = jnp.concatenate([z[2 * BLK:3 * BLK], z[3 * BLK:]], axis=1)
            for j in range(BLK - 1):
                u = u + fb[(2 * j + 1) * BLK:(2 * j + 2) * BLK] * bc(vv, j)
            for j in range(BLK - 1):
                u = u + fb[2 * j * BLK:(2 * j + 1) * BLK] * bc(u, j)
            for j in range(BLK):
                y = y + out[2 * j * BLK:(2 * j + 1) * BLK] * bc(u, j) + out[(2 * j + 1) * BLK:(2 * j + 2) * BLK] * bc(vv, j)
            y_s[b, rows, cs] = y
            u_hi = u.astype(BF16).astype(F32)
            v_hi = vv.astype(BF16).astype(F32)
            bh, kh = bh_s[b, rows, cs], kh_s[b, rows, cs]
            bh_hi = bh.astype(BF16).astype(F32)
            kh_hi = kh.astype(BF16).astype(F32)
            left = jnp.concatenate([u_hi, u - u_hi, u_hi, v_hi, vv - v_hi, v_hi], axis=0).astype(BF16)
            right = jnp.concatenate([bh_hi, bh_hi, bh - bh_hi, kh_hi, kh_hi, kh - kh_hi], axis=0).astype(BF16)
            upd = lax.dot_general(left, right, (((0,), (0,)), ((), ())), preferred_element_type=F32)
            upd = jnp.where(_iota((HEAD_B, LANES), 1) < HEAD_B, upd[0:HEAD_B], upd[HEAD_B:])
            s_scr[b, p] = s_scr[b, p] * eg_s[b, rows, cs][0:1] + upd
        return carry

    lax.fori_loop(0, n_steps // BLK, block, 0)

    inv_n = 1.0 / HEAD_B
    for b in range(nb):
        y = y_s[b]
        mu = _dot_l2(y, q_heads) * inv_n
        d = y - mu
        var = _dot_l2(d * d, q_heads) * inv_n
        yn = d * lax.rsqrt(var + GROUPNORM_EPS) * lg_ref[...] + lb_ref[...]
        o_ref[b] = ((yn + bo_s[b]) * g_s[b]).astype(BF16)
    sout_ref[...] = s_scr[...]


def _rwkv(z3, shift_prev, s0p, prm, seq_len, tc):
    bt, lp, n_rwkv = z3.shape
    nb = 4
    c_b = prm["w0"].shape[1]
    n_pairs = c_b // LANES
    full = lambda a: pl.BlockSpec(a.shape, lambda g, c: (0,) * a.ndim)
    names = ["mu", "w0", "w2", "a0", "a2", "g2", "kkw", "ka", "rk", "lg", "lb"]
    plist = [prm[n] for n in names]
    assert tc % BLK == 0 and seq_len % BLK == 0
    chunk = lambda: pltpu.VMEM((nb, tc, c_b), F32)
    return pl.pallas_call(
        functools.partial(_rwkv_body, nb=nb, tc=tc, seq_len=seq_len, c_b=c_b),
        grid=(bt // nb, lp // tc),
        in_specs=[pl.BlockSpec((nb, tc, n_rwkv), lambda g, c: (g, c, 0)),
                  pl.BlockSpec((nb, 1, n_rwkv), lambda g, c: (g, 0, 0)),
                  pl.BlockSpec((nb, n_pairs, HEAD_B, LANES), lambda g, c: (g, 0, 0, 0))]
                 + [full(a) for a in plist],
        out_specs=[pl.BlockSpec((nb, tc, c_b), lambda g, c: (g, c, 0)),
                   pl.BlockSpec((nb, n_pairs, HEAD_B, LANES), lambda g, c: (g, 0, 0, 0))],
        out_shape=[jax.ShapeDtypeStruct((bt, lp, c_b), BF16),
                   jax.ShapeDtypeStruct((bt, n_pairs, HEAD_B, LANES), F32)],
        scratch_shapes=[pltpu.VMEM((nb, n_pairs, HEAD_B, LANES), F32),
                        pltpu.VMEM((nb, 1, n_rwkv), F32)] + [chunk() for _ in range(11)],
        compiler_params=_cparams("parallel", "arbitrary"),
        name="rwkv7",
    )(z3, shift_prev, s0p, *plist)


def _s5_prep_body(are_ref, aim_ref, ldt_ref, bre_ref, bim_ref, abre_ref, abim_ref, bbre_ref, bbim_ref):
    a_re = are_ref[...]
    a_im = aim_ref[...]
    step = jnp.exp(ldt_ref[...])
    mag = jnp.exp(step * a_re)
    ab_re = mag * jnp.cos(step * a_im)
    ab_im = mag * jnp.sin(step * a_im)
    den = a_re * a_re + a_im * a_im
    cf_re = ((ab_re - 1.0) * a_re + ab_im * a_im) / den
    cf_im = (ab_im * a_re - (ab_re - 1.0) * a_im) / den
    abre_ref[...] = ab_re
    abim_ref[...] = ab_im
    bbre_ref[...] = cf_re * bre_ref[...] - cf_im * bim_ref[...]
    bbim_ref[...] = cf_re * bim_ref[...] + cf_im * bre_ref[...]


def _s5_prep(a_re, a_im, log_dt, b_re, b_im):
    g, p = a_re.shape
    grp = b_re.shape[2]
    r3 = lambda x: x.reshape(g, 1, p)
    ldt = jnp.broadcast_to(log_dt[:, None, None], (g, 1, p))
    bt = lambda x: jnp.transpose(x, (0, 2, 1))
    vec = jax.ShapeDtypeStruct((g, 1, p), F32)
    mat = jax.ShapeDtypeStruct((g, grp, p), F32)
    return pl.pallas_call(_s5_prep_body, out_shape=[vec, vec, mat, mat], name="s5_prep")(
        r3(a_re), r3(a_im), ldt, bt(b_re), bt(b_im))


def _gelu_tanh(x):
    return 0.5 * x * (1.0 + jnp.tanh(0.7978845608028654 * (x + 0.044715 * (x * x * x))))


def _s5_body(u_ref, re0_ref, im0_ref, bb_ref, ab_ref, cre_ref, cim_ref, d_ref, gw_ref, gb_ref,
             o_ref, reo_ref, imo_ref, xr_scr, xi_scr, u_s, xr_s, xi_s, o_s,
             *, nb, tc, seq_len, precise):
    c = pl.program_id(1)
    n_state = ab_ref.shape[1]

    @pl.when(c == 0)
    def _():
        xr_scr[...] = re0_ref[0]
        xi_scr[...] = im0_ref[0]
        if nb < SUBLANES:
            u_s[...] = jnp.zeros_like(u_s)

    n_lane_tiles = u_s.shape[0]
    for b in range(nb):
        ub = u_ref[b]
        for j in range(n_lane_tiles):
            u_s[j, pl.ds(b, tc, stride=SUBLANES), :] = ub[:, j * LANES:(j + 1) * LANES]
    u = jnp.concatenate([u_s[j] for j in range(n_lane_tiles)], axis=1)
    for half, dst in enumerate((xr_s, xi_s)):
        cols = slice(half * n_state, (half + 1) * n_state)
        if precise:
            uh, ul = _split2(u)
            dst[...] = _dot(uh, bb_ref[0, :, cols]) + _dot(ul, bb_ref[0, :, cols]) + _dot(uh, bb_ref[1, :, cols])
        else:
            dst[...] = _dot(u.astype(BF16), bb_ref[0, :, cols])
    ar = ab_ref[0:1, :]
    ai = ab_ref[1:2, :]

    def step(t, carry):
        xr, xi = carry
        rows = pl.ds(pl.multiple_of(t * SUBLANES, SUBLANES), SUBLANES)
        nr = ar * xr - ai * xi + xr_s[rows, :]
        ni = ar * xi + ai * xr + xi_s[rows, :]
        xr_s[rows, :] = nr
        xi_s[rows, :] = ni
        return nr, ni

    n_steps = jnp.clip(seq_len - c * tc, 0, tc)
    xr, xi = lax.fori_loop(0, n_steps, step, (xr_scr[...], xi_scr[...]))
    xr_scr[...] = xr
    xi_scr[...] = xi
    reo_ref[0] = xr
    imo_ref[0] = xi
    y = (_dot(xr_s[...].astype(BF16), cre_ref[...]) - _dot(xi_s[...].astype(BF16), cim_ref[...])
         + d_ref[...] * u)
    hg = _gelu_tanh(y)
    out = hg * _sigmoid(_dot_x3(hg, gw_ref[...]) + gb_ref[...])
    for j in range(n_lane_tiles):
        o_s[j] = out[:, j * LANES:(j + 1) * LANES]
    for b in range(nb):
        o_ref[b] = jnp.concatenate([o_s[j, pl.ds(b, tc, stride=SUBLANES), :] for j in range(n_lane_tiles)],
                                   axis=1).astype(BF16)


def _s5(u3, re0, im0, prm, seq_len, nb, tc, precise):
    bt, lp, c_c = u3.shape
    n_state = re0.shape[2]
    full = lambda a: pl.BlockSpec(a.shape, lambda g, c: (0,) * a.ndim)
    plist = [prm[n] for n in ("bb", "ab", "cre", "cim", "d", "gw", "gb")]
    st = pl.BlockSpec((1, SUBLANES, n_state), lambda g, c: (g, 0, 0))
    st_shape = jax.ShapeDtypeStruct((bt // nb, SUBLANES, n_state), F32)
    rows = tc * SUBLANES
    return pl.pallas_call(
        functools.partial(_s5_body, nb=nb, tc=tc, seq_len=seq_len, precise=precise),
        grid=(bt // nb, lp // tc),
        in_specs=[pl.BlockSpec((nb, tc, c_c), lambda g, c: (g, c, 0)), st, st] + [full(a) for a in plist],
        out_specs=[pl.BlockSpec((nb, tc, c_c), lambda g, c: (g, c, 0)), st, st],
        out_shape=[jax.ShapeDtypeStruct((bt, lp, c_c), BF16), st_shape, st_shape],
        scratch_shapes=[pltpu.VMEM((SUBLANES, n_state), F32), pltpu.VMEM((SUBLANES, n_state), F32),
                        pltpu.VMEM((c_c // LANES, rows, LANES), F32), pltpu.VMEM((rows, n_state), F32),
                        pltpu.VMEM((rows, n_state), F32), pltpu.VMEM((c_c // LANES, rows, LANES), F32)],
        compiler_params=_cparams("parallel", "arbitrary"),
        name="s5",
    )(u3, re0, im0, *plist)


def _merge_body(h_ref, oa_ref, ob_ref, oc_ref, g_ref, wg_ref, pa_ref, pb_ref, pc_ref, wo_ref, o_ref, *, d):
    h = h_ref[...]
    xn = _rms(h, g_ref[...]).astype(BF16)
    merged = None
    for n, (br_ref, pj_ref) in enumerate(((oa_ref, pa_ref), (ob_ref, pb_ref), (oc_ref, pc_ref))):
        gate = _sigmoid(_dot(xn, wg_ref[:, n * d:(n + 1) * d]))
        term = gate * _dot(br_ref[...], pj_ref[...])
        merged = term if merged is None else merged + term
    o_ref[...] = h + _dot(merged.astype(BF16), wo_ref[...])


def _merge(h2, oa, ob, oc, g1, wg, pa, pb, pc, wo):
    rows, d = h2.shape
    tm = _row_tile(rows, 512)
    row = lambda a: pl.BlockSpec((tm, a.shape[1]), lambda i: (i, 0))
    full = lambda a: pl.BlockSpec(a.shape, lambda i: (0,) * a.ndim)
    return pl.pallas_call(
        functools.partial(_merge_body, d=d),
        grid=(rows // tm,),
        in_specs=[row(h2), row(oa), row(ob), row(oc)] + [full(a) for a in (g1, wg, pa, pb, pc, wo)],
        out_specs=row(h2),
        out_shape=jax.ShapeDtypeStruct(h2.shape, F32),
        compiler_params=_cparams("parallel"),
        name="merge_outproj",
    )(h2, oa, ob, oc, g1, wg, pa, pb, pc, wo)


PREV_ROWS = 16


def _ffn_body(*refs, tm, seq_rows, seq_valid, decode):
    if decode:
        (x_ref, pa_ref, pb_ref, g_ref, ua_ref, ub_ref, cwa_ref, cwb_ref, cba_ref, cbb_ref, dn_ref,
         o_ref, xn_scr) = refs
    else:
        (x_ref, xp_ref, g_ref, ua_ref, ub_ref, cwa_ref, cwb_ref, cba_ref, cbb_ref, dn_ref,
         o_ref, xn_scr) = refs
    i = pl.program_id(0)
    j = pl.program_id(1)

    @pl.when(j == 0)
    def _():
        o_ref[...] = x_ref[...]
        if decode:
            xn_scr[...] = _rms(x_ref[...], g_ref[...]).astype(BF16)
        else:
            d = x_ref.shape[1]
            pos_p = (i * tm - PREV_ROWS + _iota((PREV_ROWS, d), 0)) % seq_rows
            keep = (pos_p < seq_valid) & (i > 0)
            xn_scr[0:PREV_ROWS] = jnp.where(keep, _rms(xp_ref[...], g_ref[...]), 0.0).astype(BF16)
            pos = (i * tm + _iota((tm, d), 0)) % seq_rows
            xn_scr[PREV_ROWS:] = jnp.where(pos < seq_valid, _rms(x_ref[...], g_ref[...]), 0.0).astype(BF16)

    nc = ua_ref.shape[1]
    xn = xn_scr[...]

    def conv(up_ref, cw_ref, cb_ref, prev_ref):
        h_ext = _dot(xn, up_ref[...])
        if decode:
            hu = h_ext
            prev = prev_ref[...]
            pos = _iota((tm, nc), 0) % seq_rows
            sh1 = jnp.where(pos == 0, pltpu.roll(prev, tm - 1, 0), pltpu.roll(hu, 1, 0))
            sh2 = jnp.where(pos < 2, prev, pltpu.roll(hu, 2, 0))
        else:
            hu = h_ext[PREV_ROWS:]
            last = h_ext[PREV_ROWS - 1:PREV_ROWS]
            last2 = h_ext[PREV_ROWS - 2:PREV_ROWS - 1]
            r1 = pltpu.roll(hu, 1, 0)
            r2 = pltpu.roll(hu, 2, 0)
            row8 = _iota((SUBLANES, nc), 0)
            head1 = jnp.where(row8 == 0, last, r1[0:SUBLANES])
            head2 = jnp.where(row8 == 0, last2, jnp.where(row8 == 1, last, r2[0:SUBLANES]))
            sh1 = jnp.concatenate([head1, r1[SUBLANES:]], axis=0)
            sh2 = jnp.concatenate([head2, r2[SUBLANES:]], axis=0)
        return cw_ref[0:1, :] * sh2 + cw_ref[1:2, :] * sh1 + cw_ref[2:3, :] * hu + cb_ref[...]

    a = conv(ua_ref, cwa_ref, cba_ref, pa_ref if decode else None)
    bgate = conv(ub_ref, cwb_ref, cbb_ref, pb_ref if decode else None)
    act = (a * _sigmoid(a)) * bgate
    o_ref[...] += _dot(act.astype(BF16), dn_ref[...])


def _ffn(h2, g2, ua, ub, cwa, cwb, cba, cbb, dn, seq_rows, seq_valid, prev_pad=None):
    assert prev_pad is not None or seq_rows - seq_valid >= CONV_TAPS - 1
    rows, d = h2.shape
    d_ff = ua.shape[1]
    nc = 256
    n_chunks = d_ff // nc
    decode = prev_pad is not None
    tm = _row_tile(rows, 1024)
    xrow = pl.BlockSpec((tm, d), lambda i, j: (i, 0))
    col = lambda r: pl.BlockSpec((r, nc), lambda i, j: (0, j))
    w_specs = [pl.BlockSpec((1, d), lambda i, j: (0, 0)), col(d), col(d), col(CONV_TAPS), col(CONV_TAPS),
               col(1), col(1), pl.BlockSpec((nc, d), lambda i, j: (j, 0))]
    scratch = [pltpu.VMEM((tm if decode else tm + PREV_ROWS, d), BF16)]
    if decode:
        in_specs = [xrow, pl.BlockSpec((tm, nc), lambda i, j: (i, j)),
                    pl.BlockSpec((tm, nc), lambda i, j: (i, n_chunks + j))] + w_specs
        args = (h2, prev_pad, prev_pad, g2, ua, ub, cwa, cwb, cba, cbb, dn)
    else:
        per = tm // PREV_ROWS
        in_specs = [xrow, pl.BlockSpec((PREV_ROWS, d), lambda i, j: (jnp.maximum(i * per - 1, 0), 0))] + w_specs
        args = (h2, h2, g2, ua, ub, cwa, cwb, cba, cbb, dn)
    return pl.pallas_call(
        functools.partial(_ffn_body, tm=tm, seq_rows=seq_rows, seq_valid=seq_valid, decode=decode),
        grid=(rows // tm, n_chunks),
        in_specs=in_specs,
        out_specs=xrow,
        out_shape=jax.ShapeDtypeStruct(h2.shape, F32),
        scratch_shapes=scratch,
        compiler_params=_cparams("parallel", "arbitrary"),
        name="conv_ffn",
    )(*args)


def _up_rows_body(x_ref, g_ref, up_ref, o_ref):
    o_ref[...] = _dot(_rms(x_ref[...], g_ref[...]).astype(BF16), up_ref[...])


def _up_rows(x2, g2, up):
    rows, d = x2.shape
    n = up.shape[1]
    nc = 512
    return pl.pallas_call(
        _up_rows_body,
        grid=(n // nc,),
        in_specs=[pl.BlockSpec((rows, d), lambda j: (0, 0)), pl.BlockSpec((1, d), lambda j: (0, 0)),
                  pl.BlockSpec((d, nc), lambda j: (0, j))],
        out_specs=pl.BlockSpec((rows, nc), lambda j: (0, j)),
        out_shape=jax.ShapeDtypeStruct((rows, n), F32),
        compiler_params=_cparams("parallel"),
        name="conv_state_rows",
    )(x2, g2, up)


def _final_norm_body(x_ref, g_ref, o_ref):
    o_ref[...] = _rms(x_ref[...], g_ref[...])


def _final_norm(h2, g):
    rows, d = h2.shape
    tm = _row_tile(rows, 1024)
    spec = pl.BlockSpec((tm, d), lambda i: (i, 0))
    return pl.pallas_call(
        _final_norm_body,
        grid=(rows // tm,),
        in_specs=[spec, pl.BlockSpec((1, d), lambda i: (0, 0))],
        out_specs=spec,
        out_shape=jax.ShapeDtypeStruct(h2.shape, F32),
        compiler_params=_cparams("parallel"),
        name="final_norm",
    )(h2, g)


def _pack_state(s):
    b, h, v, k = s.shape
    return s.reshape(b, h // 2, 2, v, k).transpose(0, 1, 3, 2, 4).reshape(b, h // 2, v, 2 * k)


def _unpack_state(s):
    b, hp, v, k2 = s.shape
    return s.reshape(b, hp, v, 2, k2 // 2).transpose(0, 1, 3, 2, 4).reshape(b, 2 * hp, v, k2 // 2)


def _block_diag(x):
    g, a, c = x.shape
    eye = jnp.eye(g, dtype=bool)[:, None, :, None]
    return jnp.where(eye, x[:, :, None, :], 0.0).reshape(g * a, g * c)


def _pad_rows(x, n):
    return jnp.concatenate([x, jnp.zeros((n - x.shape[0],) + x.shape[1:], x.dtype)], axis=0)


def kernel(x_prompt, x_sample, cache_k, cache_v, cache_logf, page_table, state_rwkv_wkv, state_rwkv_shift, state_ssm_re, state_ssm_im, state_ffn_conv, meta_tokens, norm1_g, w_in, fox_bf, rwkv_mu, rwkv_w0, rwkv_w2, rwkv_a0, rwkv_a2, rwkv_g2, rwkv_kk, rwkv_ka, rwkv_rk, rwkv_lnx_g, rwkv_lnx_b, ssm_a_re, ssm_a_im, ssm_log_dt, ssm_b_re, ssm_b_im, ssm_c_re, ssm_c_im, ssm_d, ssm_glu_w, ssm_glu_b, proj_a, proj_b, proj_c, w_out, norm2_g, ffn_up, ffn_conv_w, ffn_conv_b, ffn_down, final_norm_g):
    bsz, seq, d = x_prompt.shape
    db, t_new, _ = x_sample.shape
    depth = norm1_g.shape[0]
    n_ha = fox_bf.shape[1]
    c_a = n_ha * HEAD_A
    n_rwkv = rwkv_mu.shape[1]
    c_b = rwkv_w0.shape[1]
    n_hb = c_b // HEAD_B
    g_c, p_c = ssm_a_re.shape[1], ssm_a_re.shape[2]
    c_c = g_c * SSM_GRP
    n_state = g_c * p_c
    d_ff = ffn_down.shape[1]
    n_pool, page = cache_k.shape[1], cache_k.shape[2]
    seq_len = N_META_TOK + seq
    lp = -(-seq_len // 256) * 256
    seq_tile = 256

    meta = jnp.broadcast_to(meta_tokens[None], (bsz, N_META_TOK, d))
    h_p = jnp.concatenate([meta, x_prompt, jnp.zeros((bsz, lp - seq_len, d), F32)], axis=1)
    h_p = h_p.reshape(bsz * lp, d)
    h_s = x_sample.reshape(db * t_new, d)

    cache_kt = jnp.transpose(cache_k, (0, 1, 3, 4, 2))
    cache_vt = jnp.transpose(cache_v, (0, 1, 3, 4, 2))
    eye_h = jnp.eye(n_ha, dtype=bool)

    zeros_shift = jnp.zeros((bsz, 1, n_rwkv), F32)
    zeros_wkv = jnp.zeros((bsz, n_hb // 2, HEAD_B, 2 * HEAD_B), F32)
    assert bsz <= SUBLANES and db % SUBLANES == 0
    zeros_ssm = jnp.zeros((1, SUBLANES, n_state), F32)

    outs_p, outs_s = [], []
    for l in range(depth):
        w = w_in[l]
        o_f = 3 * c_a
        o_z = o_f + n_ha
        o_g = o_z + n_rwkv + c_c
        wp = jnp.concatenate([w[:, :o_f], w[:, o_f:o_z], jnp.zeros((d, LANES - n_ha), F32),
                              w[:, o_z:o_g]], axis=1).astype(BF16)
        wg = w[:, o_g:].astype(BF16)
        bfp = jnp.concatenate([fox_bf[l], jnp.zeros((LANES - n_ha,), F32)])[None]
        g1 = norm1_g[l][None]
        g2 = norm2_g[l][None]
        row1 = lambda x: x.reshape(1, -1)
        lora_pad = lambda x, o: jnp.zeros((n_rwkv - 3 * c_b, c_b), F32).at[o:o + x.shape[0]].set(x)
        rw = dict(mu=row1(rwkv_mu[l]), w0=row1(rwkv_w0[l]), w2=lora_pad(rwkv_w2[l], 0),
                  a0=row1(rwkv_a0[l]), a2=lora_pad(rwkv_a2[l], LORA_W_DIM),
                  g2=lora_pad(rwkv_g2[l], LORA_W_DIM + LORA_A_DIM), kkw=row1(rwkv_kk[l]),
                  ka=row1(rwkv_ka[l]), rk=row1(rwkv_rk[l]), lg=row1(rwkv_lnx_g[l]), lb=row1(rwkv_lnx_b[l]))
        ab_re, ab_im, bbt_re, bbt_im = _s5_prep(ssm_a_re[l], ssm_a_im[l], ssm_log_dt[l], ssm_b_re[l], ssm_b_im[l])
        bb = jnp.concatenate([_block_diag(bbt_re), _block_diag(bbt_im)], axis=1)
        bb_hi = bb.astype(BF16)
        bb_lo = (bb - bb_hi.astype(F32)).astype(BF16)
        s5p = dict(bb=jnp.stack([bb_hi, bb_lo]),
                   ab=jnp.concatenate([ab_re.reshape(1, n_state), ab_im.reshape(1, n_state)], axis=0),
                   cre=_block_diag(jnp.transpose(ssm_c_re[l], (0, 2, 1))).astype(BF16),
                   cim=_block_diag(jnp.transpose(ssm_c_im[l], (0, 2, 1))).astype(BF16),
                   d=row1(ssm_d[l]), gw=ssm_glu_w[l], gb=row1(ssm_glu_b[l]))
        pa, pb, pc, wo = (x[l].astype(BF16) for x in (proj_a, proj_b, proj_c, w_out))
        up = ffn_up[l].astype(BF16)
        ua, ub = up[:, :d_ff], up[:, d_ff:]
        cwa, cwb = ffn_conv_w[l][:, :d_ff], ffn_conv_w[l][:, d_ff:]
        cba, cbb = ffn_conv_b[l][None, :d_ff], ffn_conv_b[l][None, d_ff:]
        dn = ffn_down[l].astype(BF16)

        qb, k, v, kb, vb, lf, z, u = _inproj(h_p, g1, wp, bfp, c_a, n_rwkv, c_c)
        r3 = lambda x: x.reshape(bsz, lp, x.shape[1])
        o_a = _attn_prompt(*_fox_prep(r3(lf), r3(qb), r3(kb), r3(vb)), lp)
        o_b, wkv = _rwkv(r3(z), zeros_shift, zeros_wkv, rw, seq_len, seq_tile)
        o_c, re_n, im_n = _s5(r3(u), zeros_ssm, zeros_ssm, s5p, seq_len, bsz, seq_tile // 2, False)
        flat = lambda x: x.reshape(bsz * lp, x.shape[2])
        h_mid = _merge(h_p, flat(o_a), flat(o_b), flat(o_c), g1, wg, pa, pb, pc, wo)
        h_p = _ffn(h_mid, g2, ua, ub, cwa, cwb, cba, cbb, dn, lp, seq_len)
        tail = r3(h_mid)[:, seq_len - 2:seq_len].reshape(bsz * 2, d)
        conv = _up_rows(_pad_rows(tail, 16), g2, up)[:bsz * 2].reshape(bsz, 2, 2 * d_ff)
        outs_p.append((r3(k)[:, :seq_len].reshape(bsz, seq_len, n_ha, HEAD_A),
                       r3(v)[:, :seq_len].reshape(bsz, seq_len, n_ha, HEAD_A),
                       r3(lf)[:, :seq_len, :n_ha], _unpack_state(wkv), r3(z)[:, seq_len - 1],
                       re_n[0, :bsz].reshape(bsz, g_c, p_c), im_n[0, :bsz].reshape(bsz, g_c, p_c), conv))

        qb, k, v, kb, vb, lf, z, u = _inproj(h_s, g1, wp, bfp, c_a, n_rwkv, c_c)
        s3 = lambda x: x.reshape(db, t_new, x.shape[1])
        q4 = s3(qb).reshape(db, t_new, n_ha, HEAD_A)
        qbd = jnp.where(eye_h[None, None, :, :, None], q4[:, :, :, None, :], jnp.zeros((), BF16))
        qbd = qbd.reshape(db, t_new * n_ha, c_a)
        pad_new = lambda x: jnp.concatenate([s3(x), jnp.zeros((db, LANES - t_new, c_a), BF16)], axis=1)
        suf, tot = _pool_suffix(jnp.transpose(cache_logf[l], (0, 2, 1)).reshape(n_pool * n_ha, page))
        o_a = _attn_sample(page_table, l, qbd, pad_new(kb), pad_new(vb), s3(lf), cache_kt, cache_vt,
                           suf.reshape(n_pool, n_ha, page), tot.reshape(n_pool, n_ha, page))
        o_b, wkv = _rwkv(s3(z), state_rwkv_shift[l][:, None], _pack_state(state_rwkv_wkv[l]), rw, t_new, t_new)
        o_c, re_n, im_n = _s5(s3(u), state_ssm_re[l].reshape(db // SUBLANES, SUBLANES, n_state),
                              state_ssm_im[l].reshape(db // SUBLANES, SUBLANES, n_state), s5p, t_new,
                              SUBLANES, t_new, True)
        flat = lambda x: x.reshape(db * t_new, x.shape[2])
        h_mid = _merge(h_s, flat(o_a), flat(o_b), flat(o_c), g1, wg, pa, pb, pc, wo)
        prev = state_ffn_conv[l]
        prev_pad = jnp.concatenate([prev, jnp.zeros((db, t_new - prev.shape[1], prev.shape[2]), F32)], axis=1)
        h_s = _ffn(h_mid, g2, ua, ub, cwa, cwb, cba, cbb, dn, t_new, t_new,
                   prev_pad.reshape(db * t_new, 2 * d_ff))
        tail = s3(h_mid)[:, t_new - 2:].reshape(db * 2, d)
        conv = _up_rows(tail, g2, up).reshape(db, 2, 2 * d_ff)
        outs_s.append((s3(k).reshape(db, t_new, n_ha, HEAD_A), s3(v).reshape(db, t_new, n_ha, HEAD_A),
                       s3(lf)[:, :, :n_ha], _unpack_state(wkv), s3(z)[:, t_new - 1],
                       re_n.reshape(db, g_c, p_c), im_n.reshape(db, g_c, p_c), conv))

    fg = final_norm_g[None]
    y_p = _final_norm(h_p, fg).reshape(bsz, lp, d)[:, N_META_TOK:seq_len]
    y_s = _final_norm(h_s, fg).reshape(db, t_new, d)
    stack = lambda outs: [jnp.stack(t) for t in zip(*outs)]
    return tuple([y_p, y_s] + stack(outs_p) + stack(outs_s))
```

```python
import functools

import jax
import jax.numpy as jnp
from jax import lax
from jax.experimental import pallas as pl
from jax.experimental.pallas import tpu as pltpu

F32 = jnp.float32
BF16 = jnp.bfloat16

N_META_TOK = 16
RMS_EPS = 1e-6
HEAD_A = 64
HEAD_B = 64
LORA_W_DIM = 64
LORA_A_DIM = 64
LORA_G_DIM = 128
GROUPNORM_EPS = 64e-5
SSM_GRP = 16
SSM_P = 64
CONV_TAPS = 3
LANES = 128
SUBLANES = 8
VMEM_LIMIT = 52 * 1024 * 1024


def _cparams(*sem):
    return pltpu.CompilerParams(dimension_semantics=sem, vmem_limit_bytes=VMEM_LIMIT)


def _dot(a, b):
    return jnp.dot(a, b, preferred_element_type=F32)


def _dot_nt(a, b):
    return lax.dot_general(a, b, (((1,), (1,)), ((), ())), preferred_element_type=F32)


def _split2(x):
    hi = x.astype(BF16)
    lo = (x - hi.astype(F32)).astype(BF16)
    return hi, lo


def _split3(x):
    hi = x.astype(BF16)
    r = x - hi.astype(F32)
    mid = r.astype(BF16)
    lo = (r - mid.astype(F32)).astype(BF16)
    return hi, mid, lo


def _dot_l2(x, w_bf):
    hi, lo = _split2(x)
    return _dot(hi, w_bf) + _dot(lo, w_bf)


def _dot_l3(x, w_bf):
    hi, mid, lo = _split3(x)
    return _dot(hi, w_bf) + _dot(mid, w_bf) + _dot(lo, w_bf)


def _dot_r3(w_bf, x):
    hi, mid, lo = _split3(x)
    return _dot(w_bf, hi) + _dot(w_bf, mid) + _dot(w_bf, lo)


def _dot_x3(a, b):
    ah, al = _split2(a)
    bh, bl = _split2(b)
    return _dot(ah, bh) + _dot(al, bh) + _dot(ah, bl)


def _sigmoid(x):
    return 1.0 / (1.0 + jnp.exp(-x))


def _log_sigmoid(x):
    return jnp.minimum(x, 0.0) - jnp.log1p(jnp.exp(-jnp.abs(x)))


def _rms(x, g):
    ms = jnp.mean(x * x, axis=-1, keepdims=True)
    return x * lax.rsqrt(ms + RMS_EPS) * g


def _iota(shape, dim):
    return lax.broadcasted_iota(jnp.int32, shape, dim)


def _block_ones(n, blk):
    return jnp.where(_iota((n, n), 0) // blk == _iota((n, n), 1) // blk, 1.0, 0.0).astype(BF16)


def _row_tile(rows, cap):
    t = cap
    while rows % t:
        t //= 2
    return t


def _inproj_body(*refs, c_a, n_rwkv, c_c, q_scale, kv_transposed):
    h_ref, g_ref, w_ref, bf_ref = refs[0:4]
    qb_ref, k_ref, v_ref, kb_ref, vb_ref, lf_ref, z_ref, u_ref = refs[-8:]
    xn = _rms(h_ref[...], g_ref[...]).astype(BF16)
    q = _dot(xn, w_ref[:, 0:c_a])
    qb_ref[...] = (q * q_scale).astype(BF16)
    k = _dot(xn, w_ref[:, c_a:2 * c_a])
    k_ref[...] = k.T if kv_transposed else k
    kb_ref[...] = k.astype(BF16)
    v = _dot(xn, w_ref[:, 2 * c_a:3 * c_a])
    v_ref[...] = v.T if kv_transposed else v
    vb_ref[...] = v.astype(BF16)
    o = 3 * c_a
    f = _dot(xn, w_ref[:, o:o + LANES]) + bf_ref[...]
    lf_ref[...] = _log_sigmoid(f)
    o += LANES
    z_ref[...] = _dot(xn, w_ref[:, o:o + n_rwkv])
    o += n_rwkv
    u_ref[...] = _dot(xn, w_ref[:, o:o + c_c])


def _inproj(h2, g1, wp, bfp, c_a, n_rwkv, c_c, q_scale, kv_seq=None):
    rows, d = h2.shape
    row = lambda n: pl.BlockSpec((tm, n), lambda i: (i, 0))
    full = lambda a: pl.BlockSpec(a.shape, lambda i: (0,) * a.ndim)
    outs = [(c_a, BF16), (c_a, F32), (c_a, F32), (c_a, BF16), (c_a, BF16), (LANES, F32),
            (n_rwkv, F32), (c_c, F32)]
    args = [h2, g1, wp, bfp]
    aliases = {}
    if kv_seq is None:
        tm = _row_tile(rows, 512)
        out_specs = [row(n) for n, _ in outs]
        out_shape = [jax.ShapeDtypeStruct((rows, n), dt) for n, dt in outs]
        in_specs = [row(d), full(g1), full(wp), full(bfp)]
    else:
        layer, depth, bsz, seq_rows, kt, vt = kv_seq
        tm = _row_tile(seq_rows, 256)
        per_seq = seq_rows // tm
        out_specs = [row(n) for n, _ in outs]
        out_shape = [jax.ShapeDtypeStruct((rows, n), dt) for n, dt in outs]
        kv_spec = pl.BlockSpec((None, None, c_a, tm), lambda i: (layer, i // per_seq, 0, i % per_seq))
        kv_shape = jax.ShapeDtypeStruct((depth, bsz, c_a, seq_rows), F32)
        out_specs[1], out_specs[2] = kv_spec, kv_spec
        out_shape[1], out_shape[2] = kv_shape, kv_shape
        in_specs = [row(d), full(g1), full(wp), full(bfp)]
        if kt is not None:
            args += [kt, vt]
            in_specs += [pl.BlockSpec(memory_space=pl.ANY)] * 2
            aliases = {4: 1, 5: 2}
    return pl.pallas_call(
        functools.partial(_inproj_body, c_a=c_a, n_rwkv=n_rwkv, c_c=c_c, q_scale=q_scale,
                          kv_transposed=kv_seq is not None),
        grid=(rows // tm,),
        in_specs=in_specs,
        out_specs=out_specs,
        out_shape=out_shape,
        input_output_aliases=aliases,
        compiler_params=_cparams("parallel"),
        name="inproj",
    )(*args)


N_BIAS = 3
LOG2E = 1.4426950408889634


def _fox_prep_body(lf_ref, q_ref, k_ref, v_ref, qx_ref, kx_ref, vx_ref, carry_ref, *, tc, n_heads, n_valid):
    @pl.when(pl.program_id(1) == 0)
    def _():
        carry_ref[...] = jnp.zeros_like(carry_ref)

    @pl.when(pl.program_id(1) >= n_valid)
    def _():
        for ref in (qx_ref, kx_ref, vx_ref):
            ref[...] = jnp.zeros_like(ref)

    @pl.when(pl.program_id(1) < n_valid)
    def _():
        _fox_prep_tile(lf_ref, q_ref, k_ref, v_ref, qx_ref, kx_ref, vx_ref, carry_ref, tc, n_heads)


def _fox_prep_tile(lf_ref, q_ref, k_ref, v_ref, qx_ref, kx_ref, vx_ref, carry_ref, tc, n_heads):
    tri = jnp.where(_iota((tc, tc), 0) >= _iota((tc, tc), 1), 1.0, 0.0).astype(BF16)
    cs = _dot_r3(tri, lf_ref[0]) + carry_ref[...]
    carry_ref[...] = cs[tc - 1:tc]
    pieces = jnp.concatenate(_split3(cs * LOG2E), axis=1)

    nl = n_heads * LANES
    src = _iota((N_BIAS * LANES, nl), 0)
    dst = _iota((N_BIAS * LANES, nl), 1)
    base = jnp.where((dst // LANES) % 2 == 0, HEAD_A, 0)
    same_head = src % LANES == dst // LANES
    off = dst % LANES - base
    place_q = jnp.where(same_head & (off == src // LANES), 1.0, 0.0).astype(BF16)
    place_k = jnp.where(same_head & (off == N_BIAS + src // LANES), -1.0, 0.0).astype(BF16)
    off1 = off[0:1]
    qe = _dot(pieces, place_q) + jnp.where((off1 >= N_BIAS) & (off1 < 2 * N_BIAS), 1.0, 0.0)
    ke = _dot(pieces, place_k) + jnp.where((off1 >= 0) & (off1 < N_BIAS), 1.0, 0.0)
    lane_half = _iota((tc, LANES), 1) // HEAD_A
    ones = jnp.ones((tc, LANES), BF16)
    for h in range(n_heads):
        pair = slice((h // 2) * LANES, (h // 2 + 1) * LANES)
        tile = slice(h * LANES, (h + 1) * LANES)
        own = lane_half == h % 2
        qx_ref[0, h] = jnp.where(own, q_ref[0][:, pair], qe[:, tile].astype(BF16))
        kx_ref[0, h] = jnp.where(own, k_ref[0][:, pair], ke[:, tile].astype(BF16))
        vx_ref[0, h] = jnp.where(own, v_ref[0][:, pair], ones)


ATTN_TQ = 256
ATTN_TK = 1024


def _fox_prep(lf3, qb, kb, vb):
    b, lp, c_a = qb.shape
    n_heads = c_a // HEAD_A
    tc = 256
    n_valid = lp // tc
    lkv = -(-lp // ATTN_TK) * ATTN_TK
    row = lambda n: pl.BlockSpec((1, tc, n), lambda i, j: (i, jnp.minimum(j, n_valid - 1), 0))
    out = pl.BlockSpec((1, n_heads, tc, LANES), lambda i, j: (i, 0, j, 0))
    return pl.pallas_call(
        functools.partial(_fox_prep_body, tc=tc, n_heads=n_heads, n_valid=n_valid),
        grid=(b, lkv // tc),
        in_specs=[row(LANES), row(c_a), row(c_a), row(c_a)],
        out_specs=[out, out, out],
        out_shape=[jax.ShapeDtypeStruct((b, n_heads, lkv, LANES), BF16)] * 3,
        scratch_shapes=[pltpu.VMEM((1, LANES), F32)],
        compiler_params=_cparams("parallel", "arbitrary"),
        name="fox_prep",
    )(lf3, qb, kb, vb)


def _attn_body(qx_ref, kx_ref, vx_ref, o_ref, s_scr, *, tq, tk):
    row0 = pl.program_id(2) * tq
    n_full = row0 // tk
    visible = _iota((tq, tk), 1) <= _iota((tq, tk), 0) + (row0 - n_full * tk)
    q = [qx_ref[0, hh] for hh in range(2)]

    def lane_tile_max(s):
        m = s[:, 0:LANES]
        for t in range(1, tk // LANES):
            m = jnp.maximum(m, s[:, t * LANES:(t + 1) * LANES])
        return m

    def pass_max(j, ms, masked):
        start = pl.multiple_of(j * tk, tk)
        out = []
        for hh in range(2):
            s = _dot_nt(q[hh], kx_ref[0, hh, pl.ds(start, tk), :])
            if masked:
                s = jnp.where(visible, s, -jnp.inf)
            s_scr[hh, :, pl.ds(start, tk)] = s
            out.append(jnp.maximum(ms[hh], lane_tile_max(s)))
        return tuple(out)

    ms = tuple(jnp.full((tq, LANES), -jnp.inf, F32) for _ in range(2))
    ms = lax.fori_loop(0, n_full, lambda j, c: pass_max(j, c, False), ms)
    ms = pass_max(n_full, ms, True)
    mb = []
    for hh in range(2):
        m1 = jnp.broadcast_to(jnp.max(ms[hh], axis=1, keepdims=True), (tq, LANES))
        mb.append(jnp.concatenate([m1] * (tk // LANES), axis=1))

    def pass_acc(j, accs):
        start = pl.multiple_of(j * tk, tk)
        out = []
        for hh in range(2):
            p = jnp.exp2(s_scr[hh, :, pl.ds(start, tk)] - mb[hh]).astype(BF16)
            out.append(accs[hh] + _dot(p, vx_ref[0, hh, pl.ds(start, tk), :]))
        return tuple(out)

    accs = tuple(jnp.zeros((tq, LANES), F32) for _ in range(2))
    accs = lax.fori_loop(0, n_full + 1, pass_acc, accs)
    outs = [acc / pltpu.roll(acc, HEAD_A, 1) for acc in accs]
    o_ref[0] = jnp.where(_iota((tq, LANES), 1) < HEAD_A, outs[0], outs[1]).astype(BF16)


def _attn_prompt(qx, kx, vx, lp):
    b, n_heads, lkv, _ = qx.shape
    tq, tk = ATTN_TQ, ATTN_TK
    kv = pl.BlockSpec((1, 2, lkv, LANES), lambda bi, p, i: (bi, p, 0, 0))
    return pl.pallas_call(
        functools.partial(_attn_body, tq=tq, tk=tk),
        grid=(b, n_heads // 2, lp // tq),
        in_specs=[pl.BlockSpec((1, 2, tq, LANES), lambda bi, p, i: (bi, p, i, 0)), kv, kv],
        out_specs=pl.BlockSpec((1, tq, LANES), lambda bi, p, i: (bi, i, p)),
        out_shape=jax.ShapeDtypeStruct((b, lp, n_heads * HEAD_A), BF16),
        scratch_shapes=[pltpu.VMEM((2, tq, lkv), F32)],
        compiler_params=_cparams("parallel", "parallel", "arbitrary"),
        name="fox_prompt_attn",
    )(qx, kx, vx)


def _pool_suffix_body(x_ref, suf_ref, tot_ref):
    n = x_ref.shape[1]
    later = jnp.where(_iota((n, n), 0) > _iota((n, n), 1), 1.0, 0.0).astype(BF16)
    whole = jnp.ones((n, n), BF16)
    hi, mid, lo = _split3(x_ref[...])
    suf_ref[...] = _dot(hi, later) + _dot(mid, later) + _dot(lo, later)
    tot_ref[...] = _dot(hi, whole) + _dot(mid, whole) + _dot(lo, whole)


def _pool_suffix(ft2):
    rows, n = ft2.shape
    tm = _row_tile(rows, 1024)
    spec = pl.BlockSpec((tm, n), lambda i: (i, 0))
    return pl.pallas_call(
        _pool_suffix_body,
        grid=(rows // tm,),
        in_specs=[spec],
        out_specs=[spec, spec],
        out_shape=[jax.ShapeDtypeStruct(ft2.shape, F32)] * 2,
        compiler_params=_cparams("parallel"),
        name="fox_pool_suffix",
    )(ft2)


def _attn_sample_body(pt_ref, qbd_ref, kn_ref, vn_ref, lfn_ref, *rest, n_pages, t_new):
    kp = rest[0:n_pages]
    vp = rest[n_pages:2 * n_pages]
    sufp = rest[2 * n_pages:3 * n_pages]
    totp = rest[3 * n_pages:4 * n_pages]
    o_ref = rest[4 * n_pages]
    nr, c_a = qbd_ref.shape[1], qbd_ref.shape[2]
    n_h = nr // t_new
    qbd = qbd_ref[0]
    row = _iota((nr, LANES), 0)
    lane = _iota((nr, LANES), 1)

    lfn_pad = jnp.concatenate([lfn_ref[0], jnp.zeros((LANES - t_new, LANES), F32)], axis=0)
    tri = jnp.where(_iota((LANES, LANES), 0) >= _iota((LANES, LANES), 1), 1.0, 0.0).astype(BF16)
    cs = _dot_r3(tri, lfn_pad)
    rep = jnp.where(_iota((nr, LANES), 1) == row // n_h, 1.0, 0.0).astype(BF16)
    cs_rows = _dot_r3(rep, cs)
    cq = jnp.sum(jnp.where(lane == row % n_h, cs_rows, 0.0), axis=1, keepdims=True)
    cst = cs.T[0:n_h]

    tiles = [None] * (n_pages + 1)
    carry = jnp.zeros((n_h, LANES), F32)
    for j in reversed(range(n_pages)):
        kt = kp[j][...].reshape(c_a, kp[j].shape[2]).astype(BF16)
        bias = sufp[j][...] + carry
        tiles[j] = _dot(qbd, kt) + cq + jnp.concatenate([bias] * t_new, axis=0)
        carry = carry + totp[j][...]
    s = _dot_nt(qbd, kn_ref[0]) + cq - jnp.concatenate([cst] * t_new, axis=0)
    tiles[n_pages] = jnp.where((lane <= row // n_h) & (lane < t_new), s, -jnp.inf)

    m = tiles[0]
    for tl in tiles[1:]:
        m = jnp.maximum(m, tl)
    m = jnp.max(m, axis=1, keepdims=True)
    lsum = jnp.zeros((nr, LANES), F32)
    acc = jnp.zeros((nr, c_a), F32)
    for j in range(n_pages + 1):
        p = jnp.exp(tiles[j] - m)
        lsum = lsum + p
        if j == n_pages:
            acc = acc + _dot(p.astype(BF16), vn_ref[0])
        else:
            vt = vp[j][...].reshape(c_a, vp[j].shape[2]).astype(BF16)
            acc = acc + _dot_nt(p.astype(BF16), vt)
    o_full = acc / jnp.sum(lsum, axis=1, keepdims=True)
    keep = _iota((nr, c_a), 1) // HEAD_A == _iota((nr, c_a), 0) % n_h
    picked = jnp.where(keep, o_full, 0.0).astype(BF16)
    gather = jnp.where((_iota((nr, nr), 1) // n_h == _iota((nr, nr), 0)), 1.0, 0.0).astype(BF16)
    o_ref[0] = _dot(gather, picked)[0:t_new].astype(BF16)


def _attn_sample(page_table, layer, qbd, kn_pad, vn_pad, lfn, cache_kt, cache_vt, suf, tot):
    db, n_pages = page_table.shape
    nr, c_a = qbd.shape[1], qbd.shape[2]
    t_new = lfn.shape[1]
    n_h, hd, page = cache_kt.shape[2:]

    def seq_spec(a):
        return pl.BlockSpec((1,) + a.shape[1:], lambda b, pt: (b,) + (0,) * (a.ndim - 1))

    def pool_spec(j):
        return pl.BlockSpec((None, None, n_h, hd, page), lambda b, pt, j=j: (layer, pt[b, j], 0, 0, 0))

    def small_spec(j):
        return pl.BlockSpec((None, n_h, page), lambda b, pt, j=j: (pt[b, j], 0, 0))

    in_specs = ([seq_spec(qbd), seq_spec(kn_pad), seq_spec(vn_pad), seq_spec(lfn)]
                + [pool_spec(j) for j in range(n_pages)] * 2
                + [small_spec(j) for j in range(n_pages)] * 2)
    grid_spec = pltpu.PrefetchScalarGridSpec(
        num_scalar_prefetch=1,
        grid=(db,),
        in_specs=in_specs,
        out_specs=pl.BlockSpec((1, t_new, c_a), lambda b, pt: (b, 0, 0)),
    )
    args = ([qbd, kn_pad, vn_pad, lfn] + [cache_kt] * n_pages + [cache_vt] * n_pages
            + [suf] * n_pages + [tot] * n_pages)
    return pl.pallas_call(
        functools.partial(_attn_sample_body, n_pages=n_pages, t_new=t_new),
        grid_spec=grid_spec,
        out_shape=jax.ShapeDtypeStruct((db, t_new, c_a), BF16),
        compiler_params=_cparams("parallel"),
        name="fox_paged_attn",
    )(page_table, *args)


BLK = SUBLANES


def _rwkv_body(z_ref, sp_ref, s0_ref, mu_ref, w0_ref, w2_ref, a0_ref, a2_ref, g2_ref, kkw_ref, ka_ref,
               rk_ref, lg_ref, lb_ref, o_ref, sout_ref,
               s_scr, carry_scr, at_s, rt_s, bt_s, kt_s, bh_s, kh_s, eg_s, v_s, g_s, bo_s, y_s,
               *, nb, tc, seq_len, c_b):
    c = pl.program_id(1)
    n_pairs = c_b // LANES

    @pl.when(c == 0)
    def _():
        s_scr[...] = s0_ref[...]
        carry_scr[...] = sp_ref[...]

    q_heads = _block_ones(c_b, HEAD_B)
    rowid = _iota((tc, z_ref.shape[2]), 0)
    lora0 = 3 * c_b
    for b in range(nb):
        z = z_ref[b]
        zprev = jnp.where(rowid == 0, carry_scr[b], pltpu.roll(z, 1, 0))
        carry_scr[b] = z[tc - 1:tc]
        zm = z + (zprev - z) * mu_ref[...]
        r = zm[:, 0:c_b]
        k = zm[:, c_b:2 * c_b]
        v = zm[:, 2 * c_b:3 * c_b]
        x_lora = zm[:, lora0:]
        w = _log_sigmoid(w0_ref[...] + _dot_x3(jnp.tanh(x_lora), w2_ref[...])) - 0.5
        a = _sigmoid(a0_ref[...] + _dot_x3(x_lora, a2_ref[...]))
        g = _dot_x3(_sigmoid(x_lora), g2_ref[...])
        kk = k * kkw_ref[...]
        kk = kk / jnp.maximum(jnp.sqrt(_dot_l2(kk * kk, q_heads)), 1e-12)
        k2 = k * (1.0 + (a - 1.0) * ka_ref[...])
        beta = kk * a
        g_s[b] = g
        bo_s[b] = _dot_l2(r * k2 * rk_ref[...], q_heads) * v
        lw = -jnp.exp(w)
        pos = _iota(lw.shape, 0) % BLK
        cl, sx = lw, lw
        shift = 1
        while shift < BLK:
            cl = cl + jnp.where(pos >= shift, pltpu.roll(cl, shift, 0), 0.0)
            sx = sx + jnp.where(pos < BLK - shift, pltpu.roll(sx, tc - shift, 0), 0.0)
            shift *= 2
        sx = sx - lw
        e_inv = jnp.exp(-cl)
        e_out = jnp.exp(sx)
        at_s[b] = -kk * jnp.exp(cl - lw)
        rt_s[b] = r * jnp.exp(cl)
        bt_s[b] = beta * e_inv
        kt_s[b] = k2 * e_inv
        bh_s[b] = beta * e_out
        kh_s[b] = k2 * e_out
        eg_s[b] = jnp.exp(cl + sx)
        v_s[b] = v

    q_pair = _block_ones(LANES, HEAD_B)
    lane = _iota((BLK, LANES), 1)
    sub = _iota((BLK, LANES), 0)
    head0 = lane < HEAD_B
    n_steps = jnp.clip(seq_len - c * tc, 0, tc)
    chains = [(b, p) for b in range(nb) for p in range(n_pairs)]
    bc = lambda x, j: jnp.broadcast_to(x[j:j + 1], x.shape)

    @pl.when(n_steps < tc)
    def _():
        y_s[...] = jnp.zeros_like(y_s)

    def block(gi, carry):
        rows = pl.ds(pl.multiple_of(gi * BLK, BLK), BLK)
        tiles, pair_hi, pair_lo, m1 = [], [], [], []
        for b, p in chains:
            cs = slice(p * LANES, (p + 1) * LANES)
            at, rt, bt, kt = at_s[b, rows, cs], rt_s[b, rows, cs], bt_s[b, rows, cs], kt_s[b, rows, cs]
            tiles.append((at, rt))
            fb, out = [], []
            for j in range(BLK):
                bj, kj = bc(bt, j), bc(kt, j)
                if j < BLK - 1:
                    fb += [jnp.where(sub > j, at * bj, 0.0), jnp.where(sub > j, at * kj, 0.0)]
                out += [jnp.where(sub >= j, rt * bj, 0.0), jnp.where(sub >= j, rt * kj, 0.0)]
            fb = jnp.concatenate(fb, axis=0)
            fb_hi = fb.astype(BF16)
            pair_hi += [fb_hi, jnp.concatenate(out, axis=0).astype(BF16)]
            pair_lo.append((fb - fb_hi.astype(F32)).astype(BF16))
            lhs = jnp.concatenate([jnp.where(head0, at, 0.0), jnp.where(head0, 0.0, at),
                                   jnp.where(head0, rt, 0.0), jnp.where(head0, 0.0, rt)], axis=0)
            lhs_hi = lhs.astype(BF16)
            lhs_lo = (lhs - lhs_hi.astype(F32)).astype(BF16)
            st_hi, st_lo = _split2(s_scr[b, p])
            z = _dot_nt(jnp.concatenate([lhs_hi, lhs_lo], axis=0), st_hi)
            z = z[0:4 * BLK] + z[4 * BLK:] + _dot_nt(lhs_hi, st_lo)
            m1.append(z)
        n_fb, n_out = 2 * (BLK - 1) * BLK, 2 * BLK * BLK
        dots_hi = _dot(jnp.concatenate(pair_hi, axis=0), q_pair)
        dots_lo = _dot(jnp.concatenate(pair_lo, axis=0), q_pair)
        for ci, (b, p) in enumerate(chains):
            cs = slice(p * LANES, (p + 1) * LANES)
            o = ci * (n_fb + n_out)
            fb = dots_hi[o:o + n_fb] + dots_lo[ci * n_fb:(ci + 1) * n_fb]
            out = dots_hi[o + n_fb:o + n_fb + n_out]
            z = m1[ci]
            vv = v_s[b, rows, cs]
            u = jnp.concatenate([z[0:BLK], z[BLK:2 * BLK]], axis=1)
            y = jnp.concatenate([z[2 * BLK:3 * BLK], z[3 * BLK:]], axis=1)
            for j in range(BLK - 1):
                u = u + fb[(2 * j + 1) * BLK:(2 * j + 2) * BLK] * bc(vv, j)
            for j in range(BLK - 1):
                u = u + fb[2 * j * BLK:(2 * j + 1) * BLK] * bc(u, j)
            for j in range(BLK):
                y = y + out[2 * j * BLK:(2 * j + 1) * BLK] * bc(u, j) + out[(2 * j + 1) * BLK:(2 * j + 2) * BLK] * bc(vv, j)
            y_s[b, rows, cs] = y
            u_hi = u.astype(BF16).astype(F32)
            v_hi = vv.astype(BF16).astype(F32)
            bh, kh = bh_s[b, rows, cs], kh_s[b, rows, cs]
            bh_hi = bh.astype(BF16).astype(F32)
            kh_hi = kh.astype(BF16).astype(F32)
            left = jnp.concatenate([u_hi, u - u_hi, u_hi, v_hi, vv - v_hi, v_hi], axis=0).astype(BF16)
            right = jnp.concatenate([bh_hi, bh_hi, bh - bh_hi, kh_hi, kh_hi, kh - kh_hi], axis=0).astype(BF16)
            upd = lax.dot_general(left, right, (((0,), (0,)), ((), ())), preferred_element_type=F32)
            upd = jnp.where(_iota((HEAD_B, LANES), 1) < HEAD_B, upd[0:HEAD_B], upd[HEAD_B:])
            s_scr[b, p] = s_scr[b, p] * eg_s[b, rows, cs][0:1] + upd
        return carry

    lax.fori_loop(0, n_steps // BLK, block, 0)

    inv_n = 1.0 / HEAD_B
    for b in range(nb):
        y = y_s[b]
        mu = _dot_l2(y, q_heads) * inv_n
        d = y - mu
        var = _dot_l2(d * d, q_heads) * inv_n
        yn = d * lax.rsqrt(var + GROUPNORM_EPS) * lg_ref[...] + lb_ref[...]
        o_ref[b] = ((yn + bo_s[b]) * g_s[b]).astype(BF16)
    sout_ref[...] = s_scr[...]


def _rwkv(z3, shift_prev, s0p, prm, seq_len, tc):
    bt, lp, n_rwkv = z3.shape
    nb = 4
    c_b = prm["w0"].shape[1]
    n_pairs = c_b // LANES
    full = lambda a: pl.BlockSpec(a.shape, lambda g, c: (0,) * a.ndim)
    names = ["mu", "w0", "w2", "a0", "a2", "g2", "kkw", "ka", "rk", "lg", "lb"]
    plist = [prm[n] for n in names]
    assert tc % BLK == 0 and seq_len % BLK == 0
    chunk = lambda: pltpu.VMEM((nb, tc, c_b), F32)
    return pl.pallas_call(
        functools.partial(_rwkv_body, nb=nb, tc=tc, seq_len=seq_len, c_b=c_b),
        grid=(bt // nb, lp // tc),
        in_specs=[pl.BlockSpec((nb, tc, n_rwkv), lambda g, c: (g, c, 0)),
                  pl.BlockSpec((nb, 1, n_rwkv), lambda g, c: (g, 0, 0)),
                  pl.BlockSpec((nb, n_pairs, HEAD_B, LANES), lambda g, c: (g, 0, 0, 0))]
                 + [full(a) for a in plist],
        out_specs=[pl.BlockSpec((nb, tc, c_b), lambda g, c: (g, c, 0)),
                   pl.BlockSpec((nb, n_pairs, HEAD_B, LANES), lambda g, c: (g, 0, 0, 0))],
        out_shape=[jax.ShapeDtypeStruct((bt, lp, c_b), BF16),
                   jax.ShapeDtypeStruct((bt, n_pairs, HEAD_B, LANES), F32)],
        scratch_shapes=[pltpu.VMEM((nb, n_pairs, HEAD_B, LANES), F32),
                        pltpu.VMEM((nb, 1, n_rwkv), F32)] + [chunk() for _ in range(11)],
        compiler_params=_cparams("parallel", "arbitrary"),
        name="rwkv7",
    )(z3, shift_prev, s0p, *plist)


def _s5_prep_body(are_ref, aim_ref, ldt_ref, bre_ref, bim_ref, abre_ref, abim_ref, bbre_ref, bbim_ref):
    a_re = are_ref[...]
    a_im = aim_ref[...]
    step = jnp.exp(ldt_ref[...])
    mag = jnp.exp(step * a_re)
    ab_re = mag * jnp.cos(step * a_im)
    ab_im = mag * jnp.sin(step * a_im)
    den = a_re * a_re + a_im * a_im
    cf_re = ((ab_re - 1.0) * a_re + ab_im * a_im) / den
    cf_im = (ab_im * a_re - (ab_re - 1.0) * a_im) / den
    abre_ref[...] = ab_re
    abim_ref[...] = ab_im
    bbre_ref[...] = cf_re * bre_ref[...] - cf_im * bim_ref[...]
    bbim_ref[...] = cf_re * bim_ref[...] + cf_im * bre_ref[...]


def _s5_prep(a_re, a_im, log_dt, b_re, b_im):
    g, p = a_re.shape
    grp = b_re.shape[2]
    r3 = lambda x: x.reshape(g, 1, p)
    ldt = jnp.broadcast_to(log_dt[:, None, None], (g, 1, p))
    bt = lambda x: jnp.transpose(x, (0, 2, 1))
    vec = jax.ShapeDtypeStruct((g, 1, p), F32)
    mat = jax.ShapeDtypeStruct((g, grp, p), F32)
    return pl.pallas_call(_s5_prep_body, out_shape=[vec, vec, mat, mat], name="s5_prep")(
        r3(a_re), r3(a_im), ldt, bt(b_re), bt(b_im))


def _gelu_tanh(x):
    return 0.5 * x * (1.0 + jnp.tanh(0.7978845608028654 * (x + 0.044715 * (x * x * x))))


def _s5_body(u_ref, re0_ref, im0_ref, bb_ref, ab_ref, cre_ref, cim_ref, d_ref, gw_ref, gb_ref,
             o_ref, reo_ref, imo_ref, xr_scr, xi_scr, u_s, xr_s, xi_s, o_s,
             *, nb, tc, seq_len, precise):
    c = pl.program_id(1)
    n_state = ab_ref.shape[1]

    @pl.when(c == 0)
    def _():
        xr_scr[...] = re0_ref[0]
        xi_scr[...] = im0_ref[0]
        if nb < SUBLANES:
            u_s[...] = jnp.zeros_like(u_s)

    n_lane_tiles = u_s.shape[0]
    for b in range(nb):
        ub = u_ref[b]
        for j in range(n_lane_tiles):
            u_s[j, pl.ds(b, tc, stride=SUBLANES), :] = ub[:, j * LANES:(j + 1) * LANES]
    u = jnp.concatenate([u_s[j] for j in range(n_lane_tiles)], axis=1)
    for half, dst in enumerate((xr_s, xi_s)):
        cols = slice(half * n_state, (half + 1) * n_state)
        if precise:
            uh, ul = _split2(u)
            dst[...] = _dot(uh, bb_ref[0, :, cols]) + _dot(ul, bb_ref[0, :, cols]) + _dot(uh, bb_ref[1, :, cols])
        else:
            dst[...] = _dot(u.astype(BF16), bb_ref[0, :, cols])
    ar = ab_ref[0:1, :]
    ai = ab_ref[1:2, :]

    def step(t, carry):
        xr, xi = carry
        rows = pl.ds(pl.multiple_of(t * SUBLANES, SUBLANES), SUBLANES)
        nr = ar * xr - ai * xi + xr_s[rows, :]
        ni = ar * xi + ai * xr + xi_s[rows, :]
        xr_s[rows, :] = nr
        xi_s[rows, :] = ni
        return nr, ni

    n_steps = jnp.clip(seq_len - c * tc, 0, tc)
    xr, xi = lax.fori_loop(0, n_steps, step, (xr_scr[...], xi_scr[...]))
    xr_scr[...] = xr
    xi_scr[...] = xi
    reo_ref[0] = xr
    imo_ref[0] = xi
    y = (_dot(xr_s[...].astype(BF16), cre_ref[...]) - _dot(xi_s[...].astype(BF16), cim_ref[...])
         + d_ref[...] * u)
    hg = _gelu_tanh(y)
    out = hg * _sigmoid(_dot_x3(hg, gw_ref[...]) + gb_ref[...])
    for j in range(n_lane_tiles):
        o_s[j] = out[:, j * LANES:(j + 1) * LANES]
    for b in range(nb):
        o_ref[b] = jnp.concatenate([o_s[j, pl.ds(b, tc, stride=SUBLANES), :] for j in range(n_lane_tiles)],
                                   axis=1).astype(BF16)


def _s5(u3, re0, im0, prm, seq_len, nb, tc, precise):
    bt, lp, c_c = u3.shape
    n_state = re0.shape[2]
    full = lambda a: pl.BlockSpec(a.shape, lambda g, c: (0,) * a.ndim)
    plist = [prm[n] for n in ("bb", "ab", "cre", "cim", "d", "gw", "gb")]
    st = pl.BlockSpec((1, SUBLANES, n_state), lambda g, c: (g, 0, 0))
    st_shape = jax.ShapeDtypeStruct((bt // nb, SUBLANES, n_state), F32)
    rows = tc * SUBLANES
    return pl.pallas_call(
        functools.partial(_s5_body, nb=nb, tc=tc, seq_len=seq_len, precise=precise),
        grid=(bt // nb, lp // tc),
        in_specs=[pl.BlockSpec((nb, tc, c_c), lambda g, c: (g, c, 0)), st, st] + [full(a) for a in plist],
        out_specs=[pl.BlockSpec((nb, tc, c_c), lambda g, c: (g, c, 0)), st, st],
        out_shape=[jax.ShapeDtypeStruct((bt, lp, c_c), BF16), st_shape, st_shape],
        scratch_shapes=[pltpu.VMEM((SUBLANES, n_state), F32), pltpu.VMEM((SUBLANES, n_state), F32),
                        pltpu.VMEM((c_c // LANES, rows, LANES), F32), pltpu.VMEM((rows, n_state), F32),
                        pltpu.VMEM((rows, n_state), F32), pltpu.VMEM((c_c // LANES, rows, LANES), F32)],
        compiler_params=_cparams("parallel", "arbitrary"),
        name="s5",
    )(u3, re0, im0, *plist)


def _merge_body(h_ref, oa_ref, ob_ref, oc_ref, g_ref, wg_ref, pa_ref, pb_ref, pc_ref, wo_ref, o_ref, *, d):
    h = h_ref[...]
    xn = _rms(h, g_ref[...]).astype(BF16)
    merged = None
    for n, (br_ref, pj_ref) in enumerate(((oa_ref, pa_ref), (ob_ref, pb_ref), (oc_ref, pc_ref))):
        gate = _sigmoid(_dot(xn, wg_ref[:, n * d:(n + 1) * d]))
        term = gate * _dot(br_ref[...], pj_ref[...])
        merged = term if merged is None else merged + term
    o_ref[...] = h + _dot(merged.astype(BF16), wo_ref[...])


def _merge(h2, oa, ob, oc, g1, wg, pa, pb, pc, wo):
    rows, d = h2.shape
    tm = _row_tile(rows, 512)
    row = lambda a: pl.BlockSpec((tm, a.shape[1]), lambda i: (i, 0))
    full = lambda a: pl.BlockSpec(a.shape, lambda i: (0,) * a.ndim)
    return pl.pallas_call(
        functools.partial(_merge_body, d=d),
        grid=(rows // tm,),
        in_specs=[row(h2), row(oa), row(ob), row(oc)] + [full(a) for a in (g1, wg, pa, pb, pc, wo)],
        out_specs=row(h2),
        out_shape=jax.ShapeDtypeStruct(h2.shape, F32),
        compiler_params=_cparams("parallel"),
        name="merge_outproj",
    )(h2, oa, ob, oc, g1, wg, pa, pb, pc, wo)


PREV_ROWS = 16
FFN_ROW_GROUPS = 1


def _ffn_body(*refs, tm, seq_rows, seq_valid, decode):
    if decode:
        (x_ref, pa_ref, pb_ref, g_ref, ua_ref, ub_ref, cwa_ref, cwb_ref, cba_ref, cbb_ref, dn_ref,
         o_ref, xn_scr) = refs
    else:
        (x_ref, xp_ref, g_ref, ua_ref, ub_ref, cwa_ref, cwb_ref, cba_ref, cbb_ref, dn_ref,
         o_ref, xn_scr) = refs
    i = pl.program_id(0)
    j = pl.program_id(1)

    @pl.when(j == 0)
    def _():
        o_ref[...] = x_ref[...]
        if decode:
            xn_scr[...] = _rms(x_ref[...], g_ref[...]).astype(BF16)
        else:
            d = x_ref.shape[1]
            pos_p = (i * tm - PREV_ROWS + _iota((PREV_ROWS, d), 0)) % seq_rows
            keep = (pos_p < seq_valid) & (i > 0)
            xn_scr[0:PREV_ROWS] = jnp.where(keep, _rms(xp_ref[...], g_ref[...]), 0.0).astype(BF16)
            pos = (i * tm + _iota((tm, d), 0)) % seq_rows
            xn_scr[PREV_ROWS:] = jnp.where(pos < seq_valid, _rms(x_ref[...], g_ref[...]), 0.0).astype(BF16)

    nc = ua_ref.shape[1]
    n_groups = 1 if decode else FFN_ROW_GROUPS
    rg = tm // n_groups

    def conv(xs, r0, up_ref, cw_ref, cb_ref, prev_ref):
        h_ext = _dot(xs, up_ref[...])
        if decode:
            hu = h_ext
            prev = prev_ref[...]
            pos = _iota((tm, nc), 0) % seq_rows
            sh1 = jnp.where(pos == 0, pltpu.roll(prev, tm - 1, 0), pltpu.roll(hu, 1, 0))
            sh2 = jnp.where(pos < 2, prev, pltpu.roll(hu, 2, 0))
        else:
            hu = h_ext[PREV_ROWS:]
            last = h_ext[PREV_ROWS - 1:PREV_ROWS]
            last2 = h_ext[PREV_ROWS - 2:PREV_ROWS - 1]
            r1 = pltpu.roll(hu, 1, 0)
            r2 = pltpu.roll(hu, 2, 0)
            row8 = _iota((SUBLANES, nc), 0)
            head1 = jnp.where(row8 == 0, last, r1[0:SUBLANES])
            head2 = jnp.where(row8 == 0, last2, jnp.where(row8 == 1, last, r2[0:SUBLANES]))
            sh1 = jnp.concatenate([head1, r1[SUBLANES:]], axis=0)
            sh2 = jnp.concatenate([head2, r2[SUBLANES:]], axis=0)
        return cw_ref[0:1, :] * sh2 + cw_ref[1:2, :] * sh1 + cw_ref[2:3, :] * hu + cb_ref[...]

    for gi in range(n_groups):
        r0 = gi * rg
        xs = xn_scr[...] if decode else xn_scr[r0:r0 + rg + PREV_ROWS]
        a = conv(xs, r0, ua_ref, cwa_ref, cba_ref, pa_ref if decode else None)
        bgate = conv(xs, r0, ub_ref, cwb_ref, cbb_ref, pb_ref if decode else None)
        act = (a * _sigmoid(a)) * bgate
        o_ref[r0:r0 + rg] += _dot(act.astype(BF16), dn_ref[...])


def _ffn(h2, g2, ua, ub, cwa, cwb, cba, cbb, dn, seq_rows, seq_valid, prev_pad=None):
    assert prev_pad is not None or seq_rows - seq_valid >= CONV_TAPS - 1
    rows, d = h2.shape
    d_ff = ua.shape[1]
    nc = 256
    n_chunks = d_ff // nc
    decode = prev_pad is not None
    tm = _row_tile(rows, 1024)
    xrow = pl.BlockSpec((tm, d), lambda i, j: (i, 0))
    col = lambda r: pl.BlockSpec((r, nc), lambda i, j: (0, j))
    w_specs = [pl.BlockSpec((1, d), lambda i, j: (0, 0)), col(d), col(d), col(CONV_TAPS), col(CONV_TAPS),
               col(1), col(1), pl.BlockSpec((nc, d), lambda i, j: (j, 0))]
    scratch = [pltpu.VMEM((tm if decode else tm + PREV_ROWS, d), BF16)]
    if decode:
        in_specs = [xrow, pl.BlockSpec((tm, nc), lambda i, j: (i, j)),
                    pl.BlockSpec((tm, nc), lambda i, j: (i, n_chunks + j))] + w_specs
        args = (h2, prev_pad, prev_pad, g2, ua, ub, cwa, cwb, cba, cbb, dn)
    else:
        per = tm // PREV_ROWS
        in_specs = [xrow, pl.BlockSpec((PREV_ROWS, d), lambda i, j: (jnp.maximum(i * per - 1, 0), 0))] + w_specs
        args = (h2, h2, g2, ua, ub, cwa, cwb, cba, cbb, dn)
    return pl.pallas_call(
        functools.partial(_ffn_body, tm=tm, seq_rows=seq_rows, seq_valid=seq_valid, decode=decode),
        grid=(rows // tm, n_chunks),
        in_specs=in_specs,
        out_specs=xrow,
        out_shape=jax.ShapeDtypeStruct(h2.shape, F32),
        scratch_shapes=scratch,
        compiler_params=_cparams("parallel", "arbitrary"),
        name="conv_ffn",
    )(*args)


def _up_rows_body(x_ref, g_ref, up_ref, o_ref):
    o_ref[...] = _dot(_rms(x_ref[...], g_ref[...]).astype(BF16), up_ref[...])


def _up_rows(x2, g2, up):
    rows, d = x2.shape
    n = up.shape[1]
    nc = 512
    return pl.pallas_call(
        _up_rows_body,
        grid=(n // nc,),
        in_specs=[pl.BlockSpec((rows, d), lambda j: (0, 0)), pl.BlockSpec((1, d), lambda j: (0, 0)),
                  pl.BlockSpec((d, nc), lambda j: (0, j))],
        out_specs=pl.BlockSpec((rows, nc), lambda j: (0, j)),
        out_shape=jax.ShapeDtypeStruct((rows, n), F32),
        compiler_params=_cparams("parallel"),
        name="conv_state_rows",
    )(x2, g2, up)


def _final_norm_body(x_ref, g_ref, o_ref):
    o_ref[...] = _rms(x_ref[...], g_ref[...])


def _final_norm(h2, g):
    rows, d = h2.shape
    tm = _row_tile(rows, 1024)
    spec = pl.BlockSpec((tm, d), lambda i: (i, 0))
    return pl.pallas_call(
        _final_norm_body,
        grid=(rows // tm,),
        in_specs=[spec, pl.BlockSpec((1, d), lambda i: (0, 0))],
        out_specs=spec,
        out_shape=jax.ShapeDtypeStruct(h2.shape, F32),
        compiler_params=_cparams("parallel"),
        name="final_norm",
    )(h2, g)


def _pack_state(s):
    b, h, v, k = s.shape
    return s.reshape(b, h // 2, 2, v, k).transpose(0, 1, 3, 2, 4).reshape(b, h // 2, v, 2 * k)


def _unpack_state(s):
    b, hp, v, k2 = s.shape
    return s.reshape(b, hp, v, 2, k2 // 2).transpose(0, 1, 3, 2, 4).reshape(b, 2 * hp, v, k2 // 2)


def _block_diag(x):
    g, a, c = x.shape
    eye = jnp.eye(g, dtype=bool)[:, None, :, None]
    return jnp.where(eye, x[:, :, None, :], 0.0).reshape(g * a, g * c)


def _pad_rows(x, n):
    return jnp.concatenate([x, jnp.zeros((n - x.shape[0],) + x.shape[1:], x.dtype)], axis=0)


def kernel(x_prompt, x_sample, cache_k, cache_v, cache_logf, page_table, state_rwkv_wkv, state_rwkv_shift, state_ssm_re, state_ssm_im, state_ffn_conv, meta_tokens, norm1_g, w_in, fox_bf, rwkv_mu, rwkv_w0, rwkv_w2, rwkv_a0, rwkv_a2, rwkv_g2, rwkv_kk, rwkv_ka, rwkv_rk, rwkv_lnx_g, rwkv_lnx_b, ssm_a_re, ssm_a_im, ssm_log_dt, ssm_b_re, ssm_b_im, ssm_c_re, ssm_c_im, ssm_d, ssm_glu_w, ssm_glu_b, proj_a, proj_b, proj_c, w_out, norm2_g, ffn_up, ffn_conv_w, ffn_conv_b, ffn_down, final_norm_g):
    bsz, seq, d = x_prompt.shape
    db, t_new, _ = x_sample.shape
    depth = norm1_g.shape[0]
    n_ha = fox_bf.shape[1]
    c_a = n_ha * HEAD_A
    n_rwkv = rwkv_mu.shape[1]
    c_b = rwkv_w0.shape[1]
    n_hb = c_b // HEAD_B
    g_c, p_c = ssm_a_re.shape[1], ssm_a_re.shape[2]
    c_c = g_c * SSM_GRP
    n_state = g_c * p_c
    d_ff = ffn_down.shape[1]
    n_pool, page = cache_k.shape[1], cache_k.shape[2]
    seq_len = N_META_TOK + seq
    lp = -(-seq_len // 256) * 256
    seq_tile = 256

    meta = jnp.broadcast_to(meta_tokens[None], (bsz, N_META_TOK, d))
    h_p = jnp.concatenate([meta, x_prompt, jnp.zeros((bsz, lp - seq_len, d), F32)], axis=1)
    h_p = h_p.reshape(bsz * lp, d)
    h_s = x_sample.reshape(db * t_new, d)

    cache_kt = jnp.transpose(cache_k, (0, 1, 3, 4, 2))
    cache_vt = jnp.transpose(cache_v, (0, 1, 3, 4, 2))
    eye_h = jnp.eye(n_ha, dtype=bool)

    zeros_shift = jnp.zeros((bsz, 1, n_rwkv), F32)
    zeros_wkv = jnp.zeros((bsz, n_hb // 2, HEAD_B, 2 * HEAD_B), F32)
    assert bsz <= SUBLANES and db % SUBLANES == 0
    zeros_ssm = jnp.zeros((1, SUBLANES, n_state), F32)

    outs_p, outs_s = [], []
    kt_p = vt_p = None
    for l in range(depth):
        w = w_in[l]
        o_f = 3 * c_a
        o_z = o_f + n_ha
        o_g = o_z + n_rwkv + c_c
        wp = jnp.concatenate([w[:, :o_f], w[:, o_f:o_z], jnp.zeros((d, LANES - n_ha), F32),
                              w[:, o_z:o_g]], axis=1).astype(BF16)
        wg = w[:, o_g:].astype(BF16)
        bfp = jnp.concatenate([fox_bf[l], jnp.zeros((LANES - n_ha,), F32)])[None]
        g1 = norm1_g[l][None]
        g2 = norm2_g[l][None]
        row1 = lambda x: x.reshape(1, -1)
        lora_pad = lambda x, o: jnp.zeros((n_rwkv - 3 * c_b, c_b), F32).at[o:o + x.shape[0]].set(x)
        rw = dict(mu=row1(rwkv_mu[l]), w0=row1(rwkv_w0[l]), w2=lora_pad(rwkv_w2[l], 0),
                  a0=row1(rwkv_a0[l]), a2=lora_pad(rwkv_a2[l], LORA_W_DIM),
                  g2=lora_pad(rwkv_g2[l], LORA_W_DIM + LORA_A_DIM), kkw=row1(rwkv_kk[l]),
                  ka=row1(rwkv_ka[l]), rk=row1(rwkv_rk[l]), lg=row1(rwkv_lnx_g[l]), lb=row1(rwkv_lnx_b[l]))
        ab_re, ab_im, bbt_re, bbt_im = _s5_prep(ssm_a_re[l], ssm_a_im[l], ssm_log_dt[l], ssm_b_re[l], ssm_b_im[l])
        bb = jnp.concatenate([_block_diag(bbt_re), _block_diag(bbt_im)], axis=1)
        bb_hi = bb.astype(BF16)
        bb_lo = (bb - bb_hi.astype(F32)).astype(BF16)
        s5p = dict(bb=jnp.stack([bb_hi, bb_lo]),
                   ab=jnp.concatenate([ab_re.reshape(1, n_state), ab_im.reshape(1, n_state)], axis=0),
                   cre=_block_diag(jnp.transpose(ssm_c_re[l], (0, 2, 1))).astype(BF16),
                   cim=_block_diag(jnp.transpose(ssm_c_im[l], (0, 2, 1))).astype(BF16),
                   d=row1(ssm_d[l]), gw=ssm_glu_w[l], gb=row1(ssm_glu_b[l]))
        pa, pb, pc, wo = (x[l].astype(BF16) for x in (proj_a, proj_b, proj_c, w_out))
        up = ffn_up[l].astype(BF16)
        ua, ub = up[:, :d_ff], up[:, d_ff:]
        cwa, cwb = ffn_conv_w[l][:, :d_ff], ffn_conv_w[l][:, d_ff:]
        cba, cbb = ffn_conv_b[l][None, :d_ff], ffn_conv_b[l][None, d_ff:]
        dn = ffn_down[l].astype(BF16)

        qb, kt_p, vt_p, kb, vb, lf, z, u = _inproj(h_p, g1, wp, bfp, c_a, n_rwkv, c_c, HEAD_A ** -0.5 * LOG2E,
                                                   kv_seq=(l, depth, bsz, lp, kt_p, vt_p))
        r3 = lambda x: x.reshape(bsz, lp, x.shape[1])
        o_a = _attn_prompt(*_fox_prep(r3(lf), r3(qb), r3(kb), r3(vb)), lp)
        o_b, wkv = _rwkv(r3(z), zeros_shift, zeros_wkv, rw, seq_len, seq_tile)
        o_c, re_n, im_n = _s5(r3(u), zeros_ssm, zeros_ssm, s5p, seq_len, bsz, seq_tile // 2, False)
        flat = lambda x: x.reshape(bsz * lp, x.shape[2])
        h_mid = _merge(h_p, flat(o_a), flat(o_b), flat(o_c), g1, wg, pa, pb, pc, wo)
        h_p = _ffn(h_mid, g2, ua, ub, cwa, cwb, cba, cbb, dn, lp, seq_len)
        tail = r3(h_mid)[:, seq_len - 2:seq_len].reshape(bsz * 2, d)
        conv = _up_rows(_pad_rows(tail, 16), g2, up)[:bsz * 2].reshape(bsz, 2, 2 * d_ff)
        outs_p.append((r3(lf)[:, :seq_len, :n_ha], _unpack_state(wkv), r3(z)[:, seq_len - 1],
                       re_n[0, :bsz].reshape(bsz, g_c, p_c), im_n[0, :bsz].reshape(bsz, g_c, p_c), conv))

        qb, k, v, kb, vb, lf, z, u = _inproj(h_s, g1, wp, bfp, c_a, n_rwkv, c_c, HEAD_A ** -0.5)
        s3 = lambda x: x.reshape(db, t_new, x.shape[1])
        q4 = s3(qb).reshape(db, t_new, n_ha, HEAD_A)
        qbd = jnp.where(eye_h[None, None, :, :, None], q4[:, :, :, None, :], jnp.zeros((), BF16))
        qbd = qbd.reshape(db, t_new * n_ha, c_a)
        pad_new = lambda x: jnp.concatenate([s3(x), jnp.zeros((db, LANES - t_new, c_a), BF16)], axis=1)
        suf, tot = _pool_suffix(jnp.transpose(cache_logf[l], (0, 2, 1)).reshape(n_pool * n_ha, page))
        o_a = _attn_sample(page_table, l, qbd, pad_new(kb), pad_new(vb), s3(lf), cache_kt, cache_vt,
                           suf.reshape(n_pool, n_ha, page), tot.reshape(n_pool, n_ha, page))
        o_b, wkv = _rwkv(s3(z), state_rwkv_shift[l][:, None], _pack_state(state_rwkv_wkv[l]), rw, t_new, t_new)
        o_c, re_n, im_n = _s5(s3(u), state_ssm_re[l].reshape(db // SUBLANES, SUBLANES, n_state),
                              state_ssm_im[l].reshape(db // SUBLANES, SUBLANES, n_state), s5p, t_new,
                              SUBLANES, t_new, True)
        flat = lambda x: x.reshape(db * t_new, x.shape[2])
        h_mid = _merge(h_s, flat(o_a), flat(o_b), flat(o_c), g1, wg, pa, pb, pc, wo)
        prev = state_ffn_conv[l]
        prev_pad = jnp.concatenate([prev, jnp.zeros((db, t_new - prev.shape[1], prev.shape[2]), F32)], axis=1)
        h_s = _ffn(h_mid, g2, ua, ub, cwa, cwb, cba, cbb, dn, t_new, t_new,
                   prev_pad.reshape(db * t_new, 2 * d_ff))
        tail = s3(h_mid)[:, t_new - 2:].reshape(db * 2, d)
        conv = _up_rows(tail, g2, up).reshape(db, 2, 2 * d_ff)
        outs_s.append((s3(k).reshape(db, t_new, n_ha, HEAD_A), s3(v).reshape(db, t_new, n_ha, HEAD_A),
                       s3(lf)[:, :, :n_ha], _unpack_state(wkv), s3(z)[:, t_new - 1],
                       re_n.reshape(db, g_c, p_c), im_n.reshape(db, g_c, p_c), conv))

    fg = final_norm_g[None]
    y_p = _final_norm(h_p, fg).reshape(bsz, lp, d)[:, N_META_TOK:seq_len]
    y_s = _final_norm(h_s, fg).reshape(db, t_new, d)
    stack = lambda outs: [jnp.stack(t) for t in zip(*outs)]
    seq_last = lambda x: jnp.transpose(x[..., :seq_len].reshape(depth, bsz, n_ha, HEAD_A, seq_len),
                                       (0, 1, 4, 2, 3))
    return tuple([y_p, y_s, seq_last(kt_p), seq_last(vt_p)] + stack(outs_p) + stack(outs_s))
```

```python
import functools

import jax
import jax.numpy as jnp
from jax import lax
from jax.experimental import pallas as pl
from jax.experimental.pallas import tpu as pltpu

F32 = jnp.float32
BF16 = jnp.bfloat16

N_META_TOK = 16
RMS_EPS = 1e-6
HEAD_A = 64
HEAD_B = 64
LORA_W_DIM = 64
LORA_A_DIM = 64
LORA_G_DIM = 128
GROUPNORM_EPS = 64e-5
SSM_GRP = 16
SSM_P = 64
CONV_TAPS = 3
LANES = 128
SUBLANES = 8
VMEM_LIMIT = 52 * 1024 * 1024


def _cparams(*sem):
    return pltpu.CompilerParams(dimension_semantics=sem, vmem_limit_bytes=VMEM_LIMIT)


def _dot(a, b):
    return jnp.dot(a, b, preferred_element_type=F32)


def _dot_nt(a, b):
    return lax.dot_general(a, b, (((1,), (1,)), ((), ())), preferred_element_type=F32)


def _split2(x):
    hi = x.astype(BF16)
    lo = (x - hi.astype(F32)).astype(BF16)
    return hi, lo


def _split3(x):
    hi = x.astype(BF16)
    r = x - hi.astype(F32)
    mid = r.astype(BF16)
    lo = (r - mid.astype(F32)).astype(BF16)
    return hi, mid, lo


def _dot_l2(x, w_bf):
    hi, lo = _split2(x)
    return _dot(hi, w_bf) + _dot(lo, w_bf)


def _dot_r3(w_bf, x):
    hi, mid, lo = _split3(x)
    return _dot(w_bf, hi) + _dot(w_bf, mid) + _dot(w_bf, lo)


def _dot_x3(a, b):
    ah, al = _split2(a)
    bh, bl = _split2(b)
    return _dot(ah, bh) + _dot(al, bh) + _dot(ah, bl)


def _sigmoid(x):
    return 1.0 / (1.0 + jnp.exp(-x))


def _log_sigmoid(x):
    return jnp.minimum(x, 0.0) - jnp.log1p(jnp.exp(-jnp.abs(x)))


def _rms(x, g):
    ms = jnp.mean(x * x, axis=-1, keepdims=True)
    return x * lax.rsqrt(ms + RMS_EPS) * g


def _iota(shape, dim):
    return lax.broadcasted_iota(jnp.int32, shape, dim)


def _block_ones(n, blk):
    return jnp.where(_iota((n, n), 0) // blk == _iota((n, n), 1) // blk, 1.0, 0.0).astype(BF16)


def _row_tile(rows, cap):
    t = cap
    while rows % t:
        t //= 2
    return t


def _inproj_body(*refs, c_a, n_rwkv, c_c, q_scale, kv_transposed):
    h_ref, g_ref, w_ref, bf_ref = refs[0:4]
    qb_ref, k_ref, v_ref, kb_ref, vb_ref, lf_ref, z_ref, u_ref = refs[-8:]
    xn = _rms(h_ref[...], g_ref[...]).astype(BF16)
    q = _dot(xn, w_ref[:, 0:c_a])
    qb_ref[...] = (q * q_scale).astype(BF16)
    k = _dot(xn, w_ref[:, c_a:2 * c_a])
    k_ref[...] = k.T if kv_transposed else k
    kb_ref[...] = k.astype(BF16)
    v = _dot(xn, w_ref[:, 2 * c_a:3 * c_a])
    v_ref[...] = v.T if kv_transposed else v
    vb_ref[...] = v.astype(BF16)
    o = 3 * c_a
    f = _dot(xn, w_ref[:, o:o + LANES]) + bf_ref[...]
    lf_ref[...] = _log_sigmoid(f)
    o += LANES
    z_ref[...] = _dot(xn, w_ref[:, o:o + n_rwkv])
    o += n_rwkv
    u_ref[...] = _dot(xn, w_ref[:, o:o + c_c])


def _inproj(h2, g1, wp, bfp, c_a, n_rwkv, c_c, q_scale, kv_seq=None):
    rows, d = h2.shape
    row = lambda n: pl.BlockSpec((tm, n), lambda i: (i, 0))
    full = lambda a: pl.BlockSpec(a.shape, lambda i: (0,) * a.ndim)
    outs = [(c_a, BF16), (c_a, F32), (c_a, F32), (c_a, BF16), (c_a, BF16), (LANES, F32),
            (n_rwkv, F32), (c_c, F32)]
    args = [h2, g1, wp, bfp]
    aliases = {}
    if kv_seq is None:
        tm = _row_tile(rows, 512)
        out_specs = [row(n) for n, _ in outs]
        out_shape = [jax.ShapeDtypeStruct((rows, n), dt) for n, dt in outs]
        in_specs = [row(d), full(g1), full(wp), full(bfp)]
    else:
        layer, depth, bsz, seq_rows, kt, vt = kv_seq
        tm = _row_tile(seq_rows, 256)
        per_seq = seq_rows // tm
        out_specs = [row(n) for n, _ in outs]
        out_shape = [jax.ShapeDtypeStruct((rows, n), dt) for n, dt in outs]
        kv_spec = pl.BlockSpec((None, None, c_a, tm), lambda i: (layer, i // per_seq, 0, i % per_seq))
        kv_shape = jax.ShapeDtypeStruct((depth, bsz, c_a, seq_rows), F32)
        out_specs[1], out_specs[2] = kv_spec, kv_spec
        out_shape[1], out_shape[2] = kv_shape, kv_shape
        in_specs = [row(d), full(g1), full(wp), full(bfp)]
        if kt is not None:
            args += [kt, vt]
            in_specs += [pl.BlockSpec(memory_space=pl.ANY)] * 2
            aliases = {4: 1, 5: 2}
    return pl.pallas_call(
        functools.partial(_inproj_body, c_a=c_a, n_rwkv=n_rwkv, c_c=c_c, q_scale=q_scale,
                          kv_transposed=kv_seq is not None),
        grid=(rows // tm,),
        in_specs=in_specs,
        out_specs=out_specs,
        out_shape=out_shape,
        input_output_aliases=aliases,
        compiler_params=_cparams("parallel"),
        name="inproj",
    )(*args)


N_BIAS = 3
LOG2E = 1.4426950408889634


def _fox_prep_body(lf_ref, q_ref, k_ref, v_ref, qx_ref, kx_ref, vx_ref, carry_ref, *, tc, n_heads, n_valid):
    @pl.when(pl.program_id(1) == 0)
    def _():
        carry_ref[...] = jnp.zeros_like(carry_ref)

    @pl.when(pl.program_id(1) >= n_valid)
    def _():
        for ref in (qx_ref, kx_ref, vx_ref):
            ref[...] = jnp.zeros_like(ref)

    @pl.when(pl.program_id(1) < n_valid)
    def _():
        _fox_prep_tile(lf_ref, q_ref, k_ref, v_ref, qx_ref, kx_ref, vx_ref, carry_ref, tc, n_heads)


def _fox_prep_tile(lf_ref, q_ref, k_ref, v_ref, qx_ref, kx_ref, vx_ref, carry_ref, tc, n_heads):
    tri = jnp.where(_iota((tc, tc), 0) >= _iota((tc, tc), 1), 1.0, 0.0).astype(BF16)
    cs = _dot_r3(tri, lf_ref[0]) + carry_ref[...]
    carry_ref[...] = cs[tc - 1:tc]
    pieces = jnp.concatenate(_split3(cs * LOG2E), axis=1)

    nl = n_heads * LANES
    src = _iota((N_BIAS * LANES, nl), 0)
    dst = _iota((N_BIAS * LANES, nl), 1)
    base = jnp.where((dst // LANES) % 2 == 0, HEAD_A, 0)
    same_head = src % LANES == dst // LANES
    off = dst % LANES - base
    place_q = jnp.where(same_head & (off == src // LANES), 1.0, 0.0).astype(BF16)
    place_k = jnp.where(same_head & (off == N_BIAS + src // LANES), -1.0, 0.0).astype(BF16)
    off1 = off[0:1]
    qe = _dot(pieces, place_q) + jnp.where((off1 >= N_BIAS) & (off1 < 2 * N_BIAS), 1.0, 0.0)
    ke = _dot(pieces, place_k) + jnp.where((off1 >= 0) & (off1 < N_BIAS), 1.0, 0.0)
    lane_half = _iota((tc, LANES), 1) // HEAD_A
    ones = jnp.ones((tc, LANES), BF16)
    for h in range(n_heads):
        pair = slice((h // 2) * LANES, (h // 2 + 1) * LANES)
        tile = slice(h * LANES, (h + 1) * LANES)
        own = lane_half == h % 2
        qx_ref[0, h] = jnp.where(own, q_ref[0][:, pair], qe[:, tile].astype(BF16))
        kx_ref[0, h] = jnp.where(own, k_ref[0][:, pair], ke[:, tile].astype(BF16))
        vx_ref[0, h] = jnp.where(own, v_ref[0][:, pair], ones)


ATTN_TQ = 256
ATTN_TK = 1024


def _fox_prep(lf3, qb, kb, vb):
    b, lp, c_a = qb.shape
    n_heads = c_a // HEAD_A
    tc = 256
    n_valid = lp // tc
    lkv = -(-lp // ATTN_TK) * ATTN_TK
    row = lambda n: pl.BlockSpec((1, tc, n), lambda i, j: (i, jnp.minimum(j, n_valid - 1), 0))
    out = pl.BlockSpec((1, n_heads, tc, LANES), lambda i, j: (i, 0, j, 0))
    return pl.pallas_call(
        functools.partial(_fox_prep_body, tc=tc, n_heads=n_heads, n_valid=n_valid),
        grid=(b, lkv // tc),
        in_specs=[row(LANES), row(c_a), row(c_a), row(c_a)],
        out_specs=[out, out, out],
        out_shape=[jax.ShapeDtypeStruct((b, n_heads, lkv, LANES), BF16)] * 3,
        scratch_shapes=[pltpu.VMEM((1, LANES), F32)],
        compiler_params=_cparams("parallel", "arbitrary"),
        name="fox_prep",
    )(lf3, qb, kb, vb)


def _attn_body(qx_ref, kx_ref, vx_ref, o_ref, s_scr, *, tq, tk):
    row0 = pl.program_id(2) * tq
    n_full = row0 // tk
    visible = _iota((tq, tk), 1) <= _iota((tq, tk), 0) + (row0 - n_full * tk)
    q = [qx_ref[0, hh] for hh in range(2)]

    def lane_tile_max(s):
        m = s[:, 0:LANES]
        for t in range(1, tk // LANES):
            m = jnp.maximum(m, s[:, t * LANES:(t + 1) * LANES])
        return m

    def pass_max(j, ms, masked):
        start = pl.multiple_of(j * tk, tk)
        out = []
        for hh in range(2):
            s = _dot_nt(q[hh], kx_ref[0, hh, pl.ds(start, tk), :])
            if masked:
                s = jnp.where(visible, s, -jnp.inf)
            s_scr[hh, :, pl.ds(start, tk)] = s
            out.append(jnp.maximum(ms[hh], lane_tile_max(s)))
        return tuple(out)

    ms = tuple(jnp.full((tq, LANES), -jnp.inf, F32) for _ in range(2))
    ms = lax.fori_loop(0, n_full, lambda j, c: pass_max(j, c, False), ms)
    ms = pass_max(n_full, ms, True)
    mb = []
    for hh in range(2):
        m1 = jnp.broadcast_to(jnp.max(ms[hh], axis=1, keepdims=True), (tq, LANES))
        mb.append(jnp.concatenate([m1] * (tk // LANES), axis=1))

    def pass_acc(j, accs):
        start = pl.multiple_of(j * tk, tk)
        out = []
        for hh in range(2):
            p = jnp.exp2(s_scr[hh, :, pl.ds(start, tk)] - mb[hh]).astype(BF16)
            out.append(accs[hh] + _dot(p, vx_ref[0, hh, pl.ds(start, tk), :]))
        return tuple(out)

    accs = tuple(jnp.zeros((tq, LANES), F32) for _ in range(2))
    accs = lax.fori_loop(0, n_full + 1, pass_acc, accs)
    outs = [acc / pltpu.roll(acc, HEAD_A, 1) for acc in accs]
    o_ref[0] = jnp.where(_iota((tq, LANES), 1) < HEAD_A, outs[0], outs[1]).astype(BF16)


def _attn_prompt(qx, kx, vx, lp):
    b, n_heads, lkv, _ = qx.shape
    tq, tk = ATTN_TQ, ATTN_TK
    kv = pl.BlockSpec((1, 2, lkv, LANES), lambda bi, p, i: (bi, p, 0, 0))
    return pl.pallas_call(
        functools.partial(_attn_body, tq=tq, tk=tk),
        grid=(b, n_heads // 2, lp // tq),
        in_specs=[pl.BlockSpec((1, 2, tq, LANES), lambda bi, p, i: (bi, p, i, 0)), kv, kv],
        out_specs=pl.BlockSpec((1, tq, LANES), lambda bi, p, i: (bi, i, p)),
        out_shape=jax.ShapeDtypeStruct((b, lp, n_heads * HEAD_A), BF16),
        scratch_shapes=[pltpu.VMEM((2, tq, lkv), F32)],
        compiler_params=_cparams("parallel", "parallel", "arbitrary"),
        name="fox_prompt_attn",
    )(qx, kx, vx)


def _pool_suffix_body(x_ref, o_ref, *, n_h):
    n = x_ref.shape[1]
    later = jnp.where(_iota((n, n), 0) > _iota((n, n), 1), 1.0, 0.0).astype(BF16)
    whole = jnp.ones((n, n), BF16)
    hi, mid, lo = _split3(x_ref[...])
    pages = x_ref.shape[0] // n_h
    o_ref[:, 0:n_h, :] = (_dot(hi, later) + _dot(mid, later) + _dot(lo, later)).reshape(pages, n_h, n)
    o_ref[:, n_h:, :] = (_dot(hi, whole) + _dot(mid, whole) + _dot(lo, whole)).reshape(pages, n_h, n)


def _pool_suffix(ft2, n_h):
    rows, n = ft2.shape
    tm = _row_tile(rows, 1024)
    return pl.pallas_call(
        functools.partial(_pool_suffix_body, n_h=n_h),
        grid=(rows // tm,),
        in_specs=[pl.BlockSpec((tm, n), lambda i: (i, 0))],
        out_specs=pl.BlockSpec((tm // n_h, 2 * n_h, n), lambda i: (i, 0, 0)),
        out_shape=jax.ShapeDtypeStruct((rows // n_h, 2 * n_h, n), F32),
        compiler_params=_cparams("parallel"),
        name="fox_pool_suffix",
    )(ft2)


def _attn_sample_body(pt_ref, qbd_ref, kn_ref, vn_ref, lfn_ref, *rest, n_pages, t_new):
    kp = rest[0:n_pages]
    vp = rest[n_pages:2 * n_pages]
    lfp = rest[2 * n_pages:3 * n_pages]
    o_ref = rest[3 * n_pages]
    nr, c_a = qbd_ref.shape[1], qbd_ref.shape[2]
    n_h = nr // t_new
    qbd = qbd_ref[0]
    row = _iota((nr, LANES), 0)
    lane = _iota((nr, LANES), 1)

    lfn_pad = jnp.concatenate([lfn_ref[0], jnp.zeros((LANES - t_new, LANES), F32)], axis=0)
    tri = jnp.where(_iota((LANES, LANES), 0) >= _iota((LANES, LANES), 1), 1.0, 0.0).astype(BF16)
    cs = _dot_r3(tri, lfn_pad)
    rep = jnp.where(_iota((nr, LANES), 1) == row // n_h, 1.0, 0.0).astype(BF16)
    cs_rows = _dot_r3(rep, cs)
    cq = jnp.sum(jnp.where(lane == row % n_h, cs_rows, 0.0), axis=1, keepdims=True)
    cst = cs.T[0:n_h]

    tiles = [None] * (n_pages + 1)
    carry = jnp.zeros((n_h, LANES), F32)
    for j in reversed(range(n_pages)):
        kt = kp[j][...].reshape(c_a, kp[j].shape[2]).astype(BF16)
        bias = lfp[j][0:n_h] + carry
        tiles[j] = _dot(qbd, kt) + cq + jnp.concatenate([bias] * t_new, axis=0)
        carry = carry + lfp[j][n_h:]
    s = _dot_nt(qbd, kn_ref[0]) + cq - jnp.concatenate([cst] * t_new, axis=0)
    tiles[n_pages] = jnp.where((lane <= row // n_h) & (lane < t_new), s, -jnp.inf)

    m = tiles[0]
    for tl in tiles[1:]:
        m = jnp.maximum(m, tl)
    m = jnp.max(m, axis=1, keepdims=True)
    lsum = jnp.zeros((nr, LANES), F32)
    acc = jnp.zeros((nr, c_a), F32)
    for j in range(n_pages + 1):
        p = jnp.exp(tiles[j] - m)
        lsum = lsum + p
        if j == n_pages:
            acc = acc + _dot(p.astype(BF16), vn_ref[0])
        else:
            vt = vp[j][...].reshape(c_a, vp[j].shape[2]).astype(BF16)
            acc = acc + _dot_nt(p.astype(BF16), vt)
    o_full = acc / jnp.sum(lsum, axis=1, keepdims=True)
    keep = _iota((nr, c_a), 1) // HEAD_A == _iota((nr, c_a), 0) % n_h
    picked = jnp.where(keep, o_full, 0.0).astype(BF16)
    gather = jnp.where((_iota((nr, nr), 1) // n_h == _iota((nr, nr), 0)), 1.0, 0.0).astype(BF16)
    o_ref[0] = _dot(gather, picked)[0:t_new].astype(BF16)


def _attn_sample(page_table, layer, qbd, kn_pad, vn_pad, lfn, cache_kt, cache_vt, lf_pool):
    db, n_pages = page_table.shape
    nr, c_a = qbd.shape[1], qbd.shape[2]
    t_new = lfn.shape[1]
    n_h, hd, page = cache_kt.shape[2:]

    def seq_spec(a):
        return pl.BlockSpec((1,) + a.shape[1:], lambda b, pt: (b,) + (0,) * (a.ndim - 1))

    def pool_spec(j):
        return pl.BlockSpec((None, None, n_h, hd, page), lambda b, pt, j=j: (layer, pt[b, j], 0, 0, 0))

    def small_spec(j):
        return pl.BlockSpec((None, 2 * n_h, page), lambda b, pt, j=j: (pt[b, j], 0, 0))

    in_specs = ([seq_spec(qbd), seq_spec(kn_pad), seq_spec(vn_pad), seq_spec(lfn)]
                + [pool_spec(j) for j in range(n_pages)] * 2
                + [small_spec(j) for j in range(n_pages)])
    grid_spec = pltpu.PrefetchScalarGridSpec(
        num_scalar_prefetch=1,
        grid=(db,),
        in_specs=in_specs,
        out_specs=pl.BlockSpec((1, t_new, c_a), lambda b, pt: (b, 0, 0)),
    )
    args = ([qbd, kn_pad, vn_pad, lfn] + [cache_kt] * n_pages + [cache_vt] * n_pages
            + [lf_pool] * n_pages)
    return pl.pallas_call(
        functools.partial(_attn_sample_body, n_pages=n_pages, t_new=t_new),
        grid_spec=grid_spec,
        out_shape=jax.ShapeDtypeStruct((db, t_new, c_a), BF16),
        compiler_params=_cparams("parallel"),
        name="fox_paged_attn",
    )(page_table, *args)


BLK = SUBLANES


def _rwkv_body(z_ref, sp_ref, s0_ref, mu_ref, w0_ref, w2_ref, a0_ref, a2_ref, g2_ref, kkw_ref, ka_ref,
               rk_ref, lg_ref, lb_ref, o_ref, sout_ref,
               s_scr, carry_scr, at_s, rt_s, bt_s, kt_s, bh_s, kh_s, eg_s, v_s, g_s, bo_s, y_s,
               *, nb, tc, seq_len, c_b):
    c = pl.program_id(1)
    n_pairs = c_b // LANES

    @pl.when(c == 0)
    def _():
        s_scr[...] = s0_ref[...]
        carry_scr[...] = sp_ref[...]

    q_heads = _block_ones(c_b, HEAD_B)
    rowid = _iota((tc, z_ref.shape[2]), 0)
    lora0 = 3 * c_b
    for b in range(nb):
        z = z_ref[b]
        zprev = jnp.where(rowid == 0, carry_scr[b], pltpu.roll(z, 1, 0))
        carry_scr[b] = z[tc - 1:tc]
        zm = z + (zprev - z) * mu_ref[...]
        r = zm[:, 0:c_b]
        k = zm[:, c_b:2 * c_b]
        v = zm[:, 2 * c_b:3 * c_b]
        x_lora = zm[:, lora0:]
        w = _log_sigmoid(w0_ref[...] + _dot_x3(jnp.tanh(x_lora), w2_ref[...])) - 0.5
        a = _sigmoid(a0_ref[...] + _dot_x3(x_lora, a2_ref[...]))
        g = _dot_x3(_sigmoid(x_lora), g2_ref[...])
        kk = k * kkw_ref[...]
        kk = kk / jnp.maximum(jnp.sqrt(_dot_l2(kk * kk, q_heads)), 1e-12)
        k2 = k * (1.0 + (a - 1.0) * ka_ref[...])
        beta = kk * a
        g_s[b] = g
        bo_s[b] = _dot_l2(r * k2 * rk_ref[...], q_heads) * v
        lw = -jnp.exp(w)
        pos = _iota(lw.shape, 0) % BLK
        cl, sx = lw, lw
        shift = 1
        while shift < BLK:
            cl = cl + jnp.where(pos >= shift, pltpu.roll(cl, shift, 0), 0.0)
            sx = sx + jnp.where(pos < BLK - shift, pltpu.roll(sx, tc - shift, 0), 0.0)
            shift *= 2
        sx = sx - lw
        e_inv = jnp.exp(-cl)
        e_out = jnp.exp(sx)
        at_s[b] = -kk * jnp.exp(cl - lw)
        rt_s[b] = r * jnp.exp(cl)
        bt_s[b] = beta * e_inv
        kt_s[b] = k2 * e_inv
        bh_s[b] = beta * e_out
        kh_s[b] = k2 * e_out
        eg_s[b] = jnp.exp(cl + sx)
        v_s[b] = v

    q_pair = _block_ones(LANES, HEAD_B)
    lane = _iota((BLK, LANES), 1)
    sub = _iota((BLK, LANES), 0)
    head0 = lane < HEAD_B
    n_steps = jnp.clip(seq_len - c * tc, 0, tc)
    chains = [(b, p) for b in range(nb) for p in range(n_pairs)]
    bc = lambda x, j: jnp.broadcast_to(x[j:j + 1], x.shape)

    @pl.when(n_steps < tc)
    def _():
        y_s[...] = jnp.zeros_like(y_s)

    def block(gi, carry):
        rows = pl.ds(pl.multiple_of(gi * BLK, BLK), BLK)
        tiles, pair_hi, pair_lo, m1 = [], [], [], []
        for b, p in chains:
            cs = slice(p * LANES, (p + 1) * LANES)
            at, rt, bt, kt = at_s[b, rows, cs], rt_s[b, rows, cs], bt_s[b, rows, cs], kt_s[b, rows, cs]
            tiles.append((at, rt))
            fb, out = [], []
            for j in range(BLK):
                bj, kj = bc(bt, j), bc(kt, j)
                if j < BLK - 1:
                    at_j = jnp.where(sub > j, at, 0.0)
                    fb += [at_j * bj, at_j * kj]
                rt_j = jnp.where(sub >= j, rt, 0.0)
                out += [rt_j * bj, rt_j * kj]
            fb = jnp.concatenate(fb, axis=0)
            fb_hi = fb.astype(BF16)
            pair_hi += [fb_hi, jnp.concatenate(out, axis=0).astype(BF16)]
            pair_lo.append((fb - fb_hi.astype(F32)).astype(BF16))
            lhs = jnp.concatenate([jnp.where(head0, at, 0.0), jnp.where(head0, 0.0, at),
                                   jnp.where(head0, rt, 0.0), jnp.where(head0, 0.0, rt)], axis=0)
            lhs_hi = lhs.astype(BF16)
            lhs_lo = (lhs - lhs_hi.astype(F32)).astype(BF16)
            st_hi, st_lo = _split2(s_scr[b, p])
            z = _dot_nt(jnp.concatenate([lhs_hi, lhs_lo], axis=0), st_hi)
            z = z[0:4 * BLK] + z[4 * BLK:] + _dot_nt(lhs_hi, st_lo)
            m1.append(z)
        n_fb, n_out = 2 * (BLK - 1) * BLK, 2 * BLK * BLK
        dots_hi = _dot(jnp.concatenate(pair_hi, axis=0), q_pair)
        dots_lo = _dot(jnp.concatenate(pair_lo, axis=0), q_pair)
        for ci, (b, p) in enumerate(chains):
            cs = slice(p * LANES, (p + 1) * LANES)
            o = ci * (n_fb + n_out)
            fb = dots_hi[o:o + n_fb] + dots_lo[ci * n_fb:(ci + 1) * n_fb]
            out = dots_hi[o + n_fb:o + n_fb + n_out]
            z = m1[ci]
            vv = v_s[b, rows, cs]
            u = jnp.concatenate([z[0:BLK], z[BLK:2 * BLK]], axis=1)
            y = jnp.concatenate([z[2 * BLK:3 * BLK], z[3 * BLK:]], axis=1)
            for j in range(BLK - 1):
                u = u + fb[(2 * j + 1) * BLK:(2 * j + 2) * BLK] * bc(vv, j)
            for j in range(BLK - 1):
                u = u + fb[2 * j * BLK:(2 * j + 1) * BLK] * bc(u, j)
            for j in range(BLK):
                y = y + out[2 * j * BLK:(2 * j + 1) * BLK] * bc(u, j) + out[(2 * j + 1) * BLK:(2 * j + 2) * BLK] * bc(vv, j)
            y_s[b, rows, cs] = y
            u_hi = u.astype(BF16).astype(F32)
            v_hi = vv.astype(BF16).astype(F32)
            bh, kh = bh_s[b, rows, cs], kh_s[b, rows, cs]
            bh_hi = bh.astype(BF16).astype(F32)
            kh_hi = kh.astype(BF16).astype(F32)
            left = jnp.concatenate([u_hi, u - u_hi, u_hi, v_hi, vv - v_hi, v_hi], axis=0).astype(BF16)
            right = jnp.concatenate([bh_hi, bh_hi, bh - bh_hi, kh_hi, kh_hi, kh - kh_hi], axis=0).astype(BF16)
            upd = lax.dot_general(left, right, (((0,), (0,)), ((), ())), preferred_element_type=F32)
            upd = jnp.where(_iota((HEAD_B, LANES), 1) < HEAD_B, upd[0:HEAD_B], upd[HEAD_B:])
            s_scr[b, p] = s_scr[b, p] * eg_s[b, rows, cs][0:1] + upd
        return carry

    lax.fori_loop(0, n_steps // BLK, block, 0)

    inv_n = 1.0 / HEAD_B
    for b in range(nb):
        y = y_s[b]
        mu = _dot_l2(y, q_heads) * inv_n
        d = y - mu
        var = _dot_l2(d * d, q_heads) * inv_n
        yn = d * lax.rsqrt(var + GROUPNORM_EPS) * lg_ref[...] + lb_ref[...]
        o_ref[b] = ((yn + bo_s[b]) * g_s[b]).astype(BF16)
    sout_ref[...] = s_scr[...]


def _rwkv(z3, shift_prev, s0p, prm, seq_len, tc):
    bt, lp, n_rwkv = z3.shape
    nb = 4
    c_b = prm["w0"].shape[1]
    n_pairs = c_b // LANES
    full = lambda a: pl.BlockSpec(a.shape, lambda g, c: (0,) * a.ndim)
    names = ["mu", "w0", "w2", "a0", "a2", "g2", "kkw", "ka", "rk", "lg", "lb"]
    plist = [prm[n] for n in names]
    assert tc % BLK == 0 and seq_len % BLK == 0
    chunk = lambda: pltpu.VMEM((nb, tc, c_b), F32)
    return pl.pallas_call(
        functools.partial(_rwkv_body, nb=nb, tc=tc, seq_len=seq_len, c_b=c_b),
        grid=(bt // nb, lp // tc),
        in_specs=[pl.BlockSpec((nb, tc, n_rwkv), lambda g, c: (g, c, 0)),
                  pl.BlockSpec((nb, 1, n_rwkv), lambda g, c: (g, 0, 0)),
                  pl.BlockSpec((nb, n_pairs, HEAD_B, LANES), lambda g, c: (g, 0, 0, 0))]
                 + [full(a) for a in plist],
        out_specs=[pl.BlockSpec((nb, tc, c_b), lambda g, c: (g, c, 0)),
                   pl.BlockSpec((nb, n_pairs, HEAD_B, LANES), lambda g, c: (g, 0, 0, 0))],
        out_shape=[jax.ShapeDtypeStruct((bt, lp, c_b), BF16),
                   jax.ShapeDtypeStruct((bt, n_pairs, HEAD_B, LANES), F32)],
        scratch_shapes=[pltpu.VMEM((nb, n_pairs, HEAD_B, LANES), F32),
                        pltpu.VMEM((nb, 1, n_rwkv), F32)] + [chunk() for _ in range(11)],
        compiler_params=_cparams("parallel", "arbitrary"),
        name="rwkv7",
    )(z3, shift_prev, s0p, *plist)


def _s5_prep_body(are_ref, aim_ref, ldt_ref, bre_ref, bim_ref, abre_ref, abim_ref, bbre_ref, bbim_ref):
    a_re = are_ref[...]
    a_im = aim_ref[...]
    step = jnp.exp(ldt_ref[...])
    mag = jnp.exp(step * a_re)
    ab_re = mag * jnp.cos(step * a_im)
    ab_im = mag * jnp.sin(step * a_im)
    den = a_re * a_re + a_im * a_im
    cf_re = ((ab_re - 1.0) * a_re + ab_im * a_im) / den
    cf_im = (ab_im * a_re - (ab_re - 1.0) * a_im) / den
    abre_ref[...] = ab_re
    abim_ref[...] = ab_im
    bbre_ref[...] = cf_re * bre_ref[...] - cf_im * bim_ref[...]
    bbim_ref[...] = cf_re * bim_ref[...] + cf_im * bre_ref[...]


def _s5_prep(a_re, a_im, log_dt, b_re, b_im):
    g, p = a_re.shape
    grp = b_re.shape[2]
    r3 = lambda x: x.reshape(g, 1, p)
    ldt = jnp.broadcast_to(log_dt[:, None, None], (g, 1, p))
    bt = lambda x: jnp.transpose(x, (0, 2, 1))
    vec = jax.ShapeDtypeStruct((g, 1, p), F32)
    mat = jax.ShapeDtypeStruct((g, grp, p), F32)
    return pl.pallas_call(_s5_prep_body, out_shape=[vec, vec, mat, mat], name="s5_prep")(
        r3(a_re), r3(a_im), ldt, bt(b_re), bt(b_im))


def _gelu_tanh(x):
    return 0.5 * x * (1.0 + jnp.tanh(0.7978845608028654 * (x + 0.044715 * (x * x * x))))


def _s5_body(u_ref, re0_ref, im0_ref, bb_ref, ab_ref, cre_ref, cim_ref, d_ref, gw_ref, gb_ref,
             o_ref, reo_ref, imo_ref, xr_scr, xi_scr, xr_s, xi_s,
             *, nb, tc, seq_len, precise):
    c = pl.program_id(1)
    n_tiles = xr_s.shape[0]
    n_state = n_tiles * LANES
    lanes = lambda j: slice(j * LANES, (j + 1) * LANES)
    seq_rows = lambda b: pl.ds(b, tc, stride=SUBLANES)

    @pl.when(c == 0)
    def _():
        for j in range(n_tiles):
            xr_scr[j] = re0_ref[0][:, lanes(j)]
            xi_scr[j] = im0_ref[0][:, lanes(j)]
        if nb < SUBLANES:
            xr_s[...] = jnp.zeros_like(xr_s)
            xi_s[...] = jnp.zeros_like(xi_s)

    u = u_ref[...].reshape(nb * tc, u_ref.shape[2])
    for half, dst in enumerate((xr_s, xi_s)):
        cols = slice(half * n_state, (half + 1) * n_state)
        if precise:
            uh, ul = _split2(u)
            bu = _dot(uh, bb_ref[0, :, cols]) + _dot(ul, bb_ref[0, :, cols]) + _dot(uh, bb_ref[1, :, cols])
        else:
            bu = _dot(u.astype(BF16), bb_ref[0, :, cols])
        for b in range(nb):
            for j in range(n_tiles):
                dst[j, seq_rows(b), :] = bu[b * tc:(b + 1) * tc, lanes(j)]
    ar = ab_ref[0]
    ai = ab_ref[1]

    def step(t, carry):
        xr, xi = carry
        rows = pl.ds(pl.multiple_of(t * SUBLANES, SUBLANES), SUBLANES)
        nr = ar * xr - ai * xi + xr_s[:, rows, :]
        ni = ar * xi + ai * xr + xi_s[:, rows, :]
        xr_s[:, rows, :] = nr
        xi_s[:, rows, :] = ni
        return nr, ni

    n_steps = jnp.clip(seq_len - c * tc, 0, tc)
    xr, xi = lax.fori_loop(0, n_steps, step, (xr_scr[...], xi_scr[...]))
    xr_scr[...] = xr
    xi_scr[...] = xi
    reo_ref[0] = jnp.concatenate([xr[j] for j in range(n_tiles)], axis=1)
    imo_ref[0] = jnp.concatenate([xi[j] for j in range(n_tiles)], axis=1)

    def states(src):
        return jnp.concatenate([jnp.concatenate([src[j, seq_rows(b), :] for j in range(n_tiles)], axis=1)
                                for b in range(nb)], axis=0).astype(BF16)

    y = _dot(states(xr_s), cre_ref[...]) - _dot(states(xi_s), cim_ref[...]) + d_ref[...] * u
    hg = _gelu_tanh(y)
    out = hg * _sigmoid(_dot_x3(hg, gw_ref[...]) + gb_ref[...])
    o_ref[...] = out.astype(BF16).reshape(o_ref.shape)


def _s5(u3, re0, im0, prm, seq_len, nb, tc, precise):
    bt, lp, c_c = u3.shape
    n_state = re0.shape[2]
    full = lambda a: pl.BlockSpec(a.shape, lambda g, c: (0,) * a.ndim)
    plist = [prm[n] for n in ("bb", "ab", "cre", "cim", "d", "gw", "gb")]
    st = pl.BlockSpec((1, SUBLANES, n_state), lambda g, c: (g, 0, 0))
    st_shape = jax.ShapeDtypeStruct((bt // nb, SUBLANES, n_state), F32)
    rows = tc * SUBLANES
    n_tiles = n_state // LANES
    return pl.pallas_call(
        functools.partial(_s5_body, nb=nb, tc=tc, seq_len=seq_len, precise=precise),
        grid=(bt // nb, lp // tc),
        in_specs=[pl.BlockSpec((nb, tc, c_c), lambda g, c: (g, c, 0)), st, st] + [full(a) for a in plist],
        out_specs=[pl.BlockSpec((nb, tc, c_c), lambda g, c: (g, c, 0)), st, st],
        out_shape=[jax.ShapeDtypeStruct((bt, lp, c_c), BF16), st_shape, st_shape],
        scratch_shapes=[pltpu.VMEM((n_tiles, SUBLANES, LANES), F32), pltpu.VMEM((n_tiles, SUBLANES, LANES), F32),
                        pltpu.VMEM((n_tiles, rows, LANES), F32), pltpu.VMEM((n_tiles, rows, LANES), F32)],
        compiler_params=_cparams("parallel", "arbitrary"),
        name="s5",
    )(u3, re0, im0, *plist)


def _merge_body(h_ref, oa_ref, ob_ref, oc_ref, g_ref, wg_ref, pa_ref, pb_ref, pc_ref, wo_ref, o_ref, *, d):
    h = h_ref[...]
    xn = _rms(h, g_ref[...]).astype(BF16)
    merged = None
    for n, (br_ref, pj_ref) in enumerate(((oa_ref, pa_ref), (ob_ref, pb_ref), (oc_ref, pc_ref))):
        gate = _sigmoid(_dot(xn, wg_ref[:, n * d:(n + 1) * d]))
        term = gate * _dot(br_ref[...], pj_ref[...])
        merged = term if merged is None else merged + term
    o_ref[...] = h + _dot(merged.astype(BF16), wo_ref[...])


def _merge(h2, oa, ob, oc, g1, wg, pa, pb, pc, wo):
    rows, d = h2.shape
    tm = _row_tile(rows, 512)
    row = lambda a: pl.BlockSpec((tm, a.shape[1]), lambda i: (i, 0))
    full = lambda a: pl.BlockSpec(a.shape, lambda i: (0,) * a.ndim)
    return pl.pallas_call(
        functools.partial(_merge_body, d=d),
        grid=(rows // tm,),
        in_specs=[row(h2), row(oa), row(ob), row(oc)] + [full(a) for a in (g1, wg, pa, pb, pc, wo)],
        out_specs=row(h2),
        out_shape=jax.ShapeDtypeStruct(h2.shape, F32),
        compiler_params=_cparams("parallel"),
        name="merge_outproj",
    )(h2, oa, ob, oc, g1, wg, pa, pb, pc, wo)


PREV_ROWS = 16


def _ffn_body(*refs, tm, seq_rows, seq_valid, decode):
    if decode:
        (x_ref, pa_ref, pb_ref, g_ref, ua_ref, ub_ref, cwa_ref, cwb_ref, cba_ref, cbb_ref, dn_ref,
         o_ref, xn_scr) = refs
    else:
        (x_ref, xp_ref, g_ref, ua_ref, ub_ref, cwa_ref, cwb_ref, cba_ref, cbb_ref, dn_ref,
         o_ref, xn_scr) = refs
    i = pl.program_id(0)
    j = pl.program_id(1)

    @pl.when(j == 0)
    def _():
        o_ref[...] = x_ref[...]
        if decode:
            xn_scr[...] = _rms(x_ref[...], g_ref[...]).astype(BF16)
        else:
            d = x_ref.shape[1]
            pos_p = (i * tm - PREV_ROWS + _iota((PREV_ROWS, d), 0)) % seq_rows
            keep = (pos_p < seq_valid) & (i > 0)
            xn_scr[0:PREV_ROWS] = jnp.where(keep, _rms(xp_ref[...], g_ref[...]), 0.0).astype(BF16)
            pos = (i * tm + _iota((tm, d), 0)) % seq_rows
            xn_scr[PREV_ROWS:] = jnp.where(pos < seq_valid, _rms(x_ref[...], g_ref[...]), 0.0).astype(BF16)

    nc = ua_ref.shape[1]
    xn = xn_scr[...]

    def conv(up_ref, cw_ref, cb_ref, prev_ref):
        h_ext = _dot(xn, up_ref[...])
        if decode:
            hu = h_ext
            prev = prev_ref[...]
            pos = _iota((tm, nc), 0) % seq_rows
            sh1 = jnp.where(pos == 0, pltpu.roll(prev, tm - 1, 0), pltpu.roll(hu, 1, 0))
            sh2 = jnp.where(pos < 2, prev, pltpu.roll(hu, 2, 0))
        else:
            hu = h_ext[PREV_ROWS:]
            last = h_ext[PREV_ROWS - 1:PREV_ROWS]
            last2 = h_ext[PREV_ROWS - 2:PREV_ROWS - 1]
            r1 = pltpu.roll(hu, 1, 0)
            r2 = pltpu.roll(hu, 2, 0)
            row8 = _iota((SUBLANES, nc), 0)
            head1 = jnp.where(row8 == 0, last, r1[0:SUBLANES])
            head2 = jnp.where(row8 == 0, last2, jnp.where(row8 == 1, last, r2[0:SUBLANES]))
            sh1 = jnp.concatenate([head1, r1[SUBLANES:]], axis=0)
            sh2 = jnp.concatenate([head2, r2[SUBLANES:]], axis=0)
        return cw_ref[0:1, :] * sh2 + cw_ref[1:2, :] * sh1 + cw_ref[2:3, :] * hu + cb_ref[...]

    a = conv(ua_ref, cwa_ref, cba_ref, pa_ref if decode else None)
    bgate = conv(ub_ref, cwb_ref, cbb_ref, pb_ref if decode else None)
    act = (a * _sigmoid(a)) * bgate
    o_ref[...] += _dot(act.astype(BF16), dn_ref[...])


def _ffn(h2, g2, ua, ub, cwa, cwb, cba, cbb, dn, seq_rows, seq_valid, prev_pad=None):
    assert prev_pad is not None or seq_rows - seq_valid >= CONV_TAPS - 1
    rows, d = h2.shape
    d_ff = ua.shape[1]
    nc = 256
    n_chunks = d_ff // nc
    decode = prev_pad is not None
    tm = _row_tile(rows, 1024)
    xrow = pl.BlockSpec((tm, d), lambda i, j: (i, 0))
    col = lambda r: pl.BlockSpec((r, nc), lambda i, j: (0, j))
    w_specs = [pl.BlockSpec((1, d), lambda i, j: (0, 0)), col(d), col(d), col(CONV_TAPS), col(CONV_TAPS),
               col(1), col(1), pl.BlockSpec((nc, d), lambda i, j: (j, 0))]
    scratch = [pltpu.VMEM((tm if decode else tm + PREV_ROWS, d), BF16)]
    if decode:
        in_specs = [xrow, pl.BlockSpec((tm, nc), lambda i, j: (i, j)),
                    pl.BlockSpec((tm, nc), lambda i, j: (i, n_chunks + j))] + w_specs
        args = (h2, prev_pad, prev_pad, g2, ua, ub, cwa, cwb, cba, cbb, dn)
    else:
        per = tm // PREV_ROWS
        in_specs = [xrow, pl.BlockSpec((PREV_ROWS, d), lambda i, j: (jnp.maximum(i * per - 1, 0), 0))] + w_specs
        args = (h2, h2, g2, ua, ub, cwa, cwb, cba, cbb, dn)
    return pl.pallas_call(
        functools.partial(_ffn_body, tm=tm, seq_rows=seq_rows, seq_valid=seq_valid, decode=decode),
        grid=(rows // tm, n_chunks),
        in_specs=in_specs,
        out_specs=xrow,
        out_shape=jax.ShapeDtypeStruct(h2.shape, F32),
        scratch_shapes=scratch,
        compiler_params=_cparams("parallel", "arbitrary"),
        name="conv_ffn",
    )(*args)


def _up_rows_body(x_ref, g_ref, up_ref, o_ref):
    o_ref[...] = _dot(_rms(x_ref[...], g_ref[...]).astype(BF16), up_ref[...])


def _up_rows(x2, g2, up):
    rows, d = x2.shape
    n = up.shape[1]
    nc = 512
    return pl.pallas_call(
        _up_rows_body,
        grid=(n // nc,),
        in_specs=[pl.BlockSpec((rows, d), lambda j: (0, 0)), pl.BlockSpec((1, d), lambda j: (0, 0)),
                  pl.BlockSpec((d, nc), lambda j: (0, j))],
        out_specs=pl.BlockSpec((rows, nc), lambda j: (0, j)),
        out_shape=jax.ShapeDtypeStruct((rows, n), F32),
        compiler_params=_cparams("parallel"),
        name="conv_state_rows",
    )(x2, g2, up)


def _final_norm_body(x_ref, g_ref, o_ref):
    o_ref[...] = _rms(x_ref[...], g_ref[...])


def _final_norm(h2, g):
    rows, d = h2.shape
    tm = _row_tile(rows, 1024)
    spec = pl.BlockSpec((tm, d), lambda i: (i, 0))
    return pl.pallas_call(
        _final_norm_body,
        grid=(rows // tm,),
        in_specs=[spec, pl.BlockSpec((1, d), lambda i: (0, 0))],
        out_specs=spec,
        out_shape=jax.ShapeDtypeStruct(h2.shape, F32),
        compiler_params=_cparams("parallel"),
        name="final_norm",
    )(h2, g)


def _pack_state(s):
    b, h, v, k = s.shape
    return s.reshape(b, h // 2, 2, v, k).transpose(0, 1, 3, 2, 4).reshape(b, h // 2, v, 2 * k)


def _unpack_state(s):
    b, hp, v, k2 = s.shape
    return s.reshape(b, hp, v, 2, k2 // 2).transpose(0, 1, 3, 2, 4).reshape(b, 2 * hp, v, k2 // 2)


def _block_diag(x):
    g, a, c = x.shape
    eye = jnp.eye(g, dtype=bool)[:, None, :, None]
    return jnp.where(eye, x[:, :, None, :], 0.0).reshape(g * a, g * c)


def _pad_rows(x, n):
    return jnp.concatenate([x, jnp.zeros((n - x.shape[0],) + x.shape[1:], x.dtype)], axis=0)


def kernel(x_prompt, x_sample, cache_k, cache_v, cache_logf, page_table, state_rwkv_wkv, state_rwkv_shift, state_ssm_re, state_ssm_im, state_ffn_conv, meta_tokens, norm1_g, w_in, fox_bf, rwkv_mu, rwkv_w0, rwkv_w2, rwkv_a0, rwkv_a2, rwkv_g2, rwkv_kk, rwkv_ka, rwkv_rk, rwkv_lnx_g, rwkv_lnx_b, ssm_a_re, ssm_a_im, ssm_log_dt, ssm_b_re, ssm_b_im, ssm_c_re, ssm_c_im, ssm_d, ssm_glu_w, ssm_glu_b, proj_a, proj_b, proj_c, w_out, norm2_g, ffn_up, ffn_conv_w, ffn_conv_b, ffn_down, final_norm_g):
    bsz, seq, d = x_prompt.shape
    db, t_new, _ = x_sample.shape
    depth = norm1_g.shape[0]
    n_ha = fox_bf.shape[1]
    c_a = n_ha * HEAD_A
    n_rwkv = rwkv_mu.shape[1]
    c_b = rwkv_w0.shape[1]
    n_hb = c_b // HEAD_B
    g_c, p_c = ssm_a_re.shape[1], ssm_a_re.shape[2]
    c_c = g_c * SSM_GRP
    n_state = g_c * p_c
    d_ff = ffn_down.shape[1]
    n_pool, page = cache_k.shape[1], cache_k.shape[2]
    seq_len = N_META_TOK + seq
    lp = -(-seq_len // 256) * 256
    seq_tile = 256

    meta = jnp.broadcast_to(meta_tokens[None], (bsz, N_META_TOK, d))
    h_p = jnp.concatenate([meta, x_prompt, jnp.zeros((bsz, lp - seq_len, d), F32)], axis=1)
    h_p = h_p.reshape(bsz * lp, d)
    h_s = x_sample.reshape(db * t_new, d)

    cache_kt = jnp.transpose(cache_k, (0, 1, 3, 4, 2))
    cache_vt = jnp.transpose(cache_v, (0, 1, 3, 4, 2))
    eye_h = jnp.eye(n_ha, dtype=bool)

    zeros_shift = jnp.zeros((bsz, 1, n_rwkv), F32)
    zeros_wkv = jnp.zeros((bsz, n_hb // 2, HEAD_B, 2 * HEAD_B), F32)
    assert bsz <= SUBLANES and db % SUBLANES == 0
    zeros_ssm = jnp.zeros((1, SUBLANES, n_state), F32)

    outs_p, outs_s = [], []
    kt_p = vt_p = None
    for l in range(depth):
        w = w_in[l]
        o_f = 3 * c_a
        o_z = o_f + n_ha
        o_g = o_z + n_rwkv + c_c
        wp = jnp.concatenate([w[:, :o_f], w[:, o_f:o_z], jnp.zeros((d, LANES - n_ha), F32),
                              w[:, o_z:o_g]], axis=1).astype(BF16)
        wg = w[:, o_g:].astype(BF16)
        bfp = jnp.concatenate([fox_bf[l], jnp.zeros((LANES - n_ha,), F32)])[None]
        g1 = norm1_g[l][None]
        g2 = norm2_g[l][None]
        row1 = lambda x: x.reshape(1, -1)
        lora_pad = lambda x, o: jnp.zeros((n_rwkv - 3 * c_b, c_b), F32).at[o:o + x.shape[0]].set(x)
        rw = dict(mu=row1(rwkv_mu[l]), w0=row1(rwkv_w0[l]), w2=lora_pad(rwkv_w2[l], 0),
                  a0=row1(rwkv_a0[l]), a2=lora_pad(rwkv_a2[l], LORA_W_DIM),
                  g2=lora_pad(rwkv_g2[l], LORA_W_DIM + LORA_A_DIM), kkw=row1(rwkv_kk[l]),
                  ka=row1(rwkv_ka[l]), rk=row1(rwkv_rk[l]), lg=row1(rwkv_lnx_g[l]), lb=row1(rwkv_lnx_b[l]))
        ab_re, ab_im, bbt_re, bbt_im = _s5_prep(ssm_a_re[l], ssm_a_im[l], ssm_log_dt[l], ssm_b_re[l], ssm_b_im[l])
        bb = jnp.concatenate([_block_diag(bbt_re), _block_diag(bbt_im)], axis=1)
        bb_hi = bb.astype(BF16)
        bb_lo = (bb - bb_hi.astype(F32)).astype(BF16)
        s5p = dict(bb=jnp.stack([bb_hi, bb_lo]),
                   ab=jnp.stack([ab_re, ab_im]).reshape(2, n_state // LANES, 1, LANES),
                   cre=_block_diag(jnp.transpose(ssm_c_re[l], (0, 2, 1))).astype(BF16),
                   cim=_block_diag(jnp.transpose(ssm_c_im[l], (0, 2, 1))).astype(BF16),
                   d=row1(ssm_d[l]), gw=ssm_glu_w[l], gb=row1(ssm_glu_b[l]))
        pa, pb, pc, wo = (x[l].astype(BF16) for x in (proj_a, proj_b, proj_c, w_out))
        up = ffn_up[l].astype(BF16)
        ua, ub = up[:, :d_ff], up[:, d_ff:]
        cwa, cwb = ffn_conv_w[l][:, :d_ff], ffn_conv_w[l][:, d_ff:]
        cba, cbb = ffn_conv_b[l][None, :d_ff], ffn_conv_b[l][None, d_ff:]
        dn = ffn_down[l].astype(BF16)

        qb, kt_p, vt_p, kb, vb, lf, z, u = _inproj(h_p, g1, wp, bfp, c_a, n_rwkv, c_c, HEAD_A ** -0.5 * LOG2E,
                                                   kv_seq=(l, depth, bsz, lp, kt_p, vt_p))
        r3 = lambda x: x.reshape(bsz, lp, x.shape[1])
        o_a = _attn_prompt(*_fox_prep(r3(lf), r3(qb), r3(kb), r3(vb)), lp)
        o_b, wkv = _rwkv(r3(z), zeros_shift, zeros_wkv, rw, seq_len, seq_tile)
        o_c, re_n, im_n = _s5(r3(u), zeros_ssm, zeros_ssm, s5p, seq_len, bsz, seq_tile // 2, False)
        flat = lambda x: x.reshape(bsz * lp, x.shape[2])
        h_mid = _merge(h_p, flat(o_a), flat(o_b), flat(o_c), g1, wg, pa, pb, pc, wo)
        h_p = _ffn(h_mid, g2, ua, ub, cwa, cwb, cba, cbb, dn, lp, seq_len)
        tail = r3(h_mid)[:, seq_len - 2:seq_len].reshape(bsz * 2, d)
        conv = _up_rows(_pad_rows(tail, 16), g2, up)[:bsz * 2].reshape(bsz, 2, 2 * d_ff)
        outs_p.append((r3(lf)[:, :seq_len, :n_ha], _unpack_state(wkv), r3(z)[:, seq_len - 1],
                       re_n[0, :bsz].reshape(bsz, g_c, p_c), im_n[0, :bsz].reshape(bsz, g_c, p_c), conv))

        qb, k, v, kb, vb, lf, z, u = _inproj(h_s, g1, wp, bfp, c_a, n_rwkv, c_c, HEAD_A ** -0.5)
        s3 = lambda x: x.reshape(db, t_new, x.shape[1])
        q4 = s3(qb).reshape(db, t_new, n_ha, HEAD_A)
        qbd = jnp.where(eye_h[None, None, :, :, None], q4[:, :, :, None, :], jnp.zeros((), BF16))
        qbd = qbd.reshape(db, t_new * n_ha, c_a)
        pad_new = lambda x: jnp.concatenate([s3(x), jnp.zeros((db, LANES - t_new, c_a), BF16)], axis=1)
        lf_pool = _pool_suffix(jnp.transpose(cache_logf[l], (0, 2, 1)).reshape(n_pool * n_ha, page), n_ha)
        o_a = _attn_sample(page_table, l, qbd, pad_new(kb), pad_new(vb), s3(lf), cache_kt, cache_vt, lf_pool)
        o_b, wkv = _rwkv(s3(z), state_rwkv_shift[l][:, None], _pack_state(state_rwkv_wkv[l]), rw, t_new, t_new)
        o_c, re_n, im_n = _s5(s3(u), state_ssm_re[l].reshape(db // SUBLANES, SUBLANES, n_state),
                              state_ssm_im[l].reshape(db // SUBLANES, SUBLANES, n_state), s5p, t_new,
                              SUBLANES, t_new, True)
        flat = lambda x: x.reshape(db * t_new, x.shape[2])
        h_mid = _merge(h_s, flat(o_a), flat(o_b), flat(o_c), g1, wg, pa, pb, pc, wo)
        prev = state_ffn_conv[l]
        prev_pad = jnp.concatenate([prev, jnp.zeros((db, t_new - prev.shape[1], prev.shape[2]), F32)], axis=1)
        h_s = _ffn(h_mid, g2, ua, ub, cwa, cwb, cba, cbb, dn, t_new, t_new,
                   prev_pad.reshape(db * t_new, 2 * d_ff))
        tail = s3(h_mid)[:, t_new - 2:].reshape(db * 2, d)
        conv = _up_rows(tail, g2, up).reshape(db, 2, 2 * d_ff)
        outs_s.append((s3(k).reshape(db, t_new, n_ha, HEAD_A), s3(v).reshape(db, t_new, n_ha, HEAD_A),
                       s3(lf)[:, :, :n_ha], _unpack_state(wkv), s3(z)[:, t_new - 1],
                       re_n.reshape(db, g_c, p_c), im_n.reshape(db, g_c, p_c), conv))

    fg = final_norm_g[None]
    y_p = _final_norm(h_p, fg).reshape(bsz, lp, d)[:, N_META_TOK:seq_len]
    y_s = _final_norm(h_s, fg).reshape(db, t_new, d)
    stack = lambda outs: [jnp.stack(t) for t in zip(*outs)]
    seq_last = lambda x: jnp.transpose(x[..., :seq_len].reshape(depth, bsz, n_ha, HEAD_A, seq_len),
                                       (0, 1, 4, 2, 3))
    return tuple([y_p, y_s, seq_last(kt_p), seq_last(vt_p)] + stack(outs_p) + stack(outs_s))
```

```python
import functools

import jax
import jax.numpy as jnp
from jax import lax
from jax.experimental import pallas as pl
from jax.experimental.pallas import tpu as pltpu

F32 = jnp.float32
BF16 = jnp.bfloat16

N_META_TOK = 16
RMS_EPS = 1e-6
HEAD_A = 64
HEAD_B = 64
LORA_W_DIM = 64
LORA_A_DIM = 64
LORA_G_DIM = 128
GROUPNORM_EPS = 64e-5
SSM_GRP = 16
SSM_P = 64
CONV_TAPS = 3
LANES = 128
SUBLANES = 8
VMEM_LIMIT = 52 * 1024 * 1024


def _cparams(*sem):
    return pltpu.CompilerParams(dimension_semantics=sem, vmem_limit_bytes=VMEM_LIMIT)


def _dot(a, b):
    return jnp.dot(a, b, preferred_element_type=F32)


def _dot_nt(a, b):
    return lax.dot_general(a, b, (((1,), (1,)), ((), ())), preferred_element_type=F32)


def _split2(x):
    hi = x.astype(BF16)
    lo = (x - hi.astype(F32)).astype(BF16)
    return hi, lo


def _split3(x):
    hi = x.astype(BF16)
    r = x - hi.astype(F32)
    mid = r.astype(BF16)
    lo = (r - mid.astype(F32)).astype(BF16)
    return hi, mid, lo


def _dot_l2(x, w_bf):
    hi, lo = _split2(x)
    return _dot(hi, w_bf) + _dot(lo, w_bf)


def _dot_r3(w_bf, x):
    hi, mid, lo = _split3(x)
    return _dot(w_bf, hi) + _dot(w_bf, mid) + _dot(w_bf, lo)


def _dot_x3(a, b):
    ah, al = _split2(a)
    bh, bl = _split2(b)
    return _dot(ah, bh) + _dot(al, bh) + _dot(ah, bl)


def _sigmoid(x):
    return 1.0 / (1.0 + jnp.exp(-x))


def _log_sigmoid(x):
    return jnp.minimum(x, 0.0) - jnp.log1p(jnp.exp(-jnp.abs(x)))


def _rms(x, g):
    ms = jnp.mean(x * x, axis=-1, keepdims=True)
    return x * lax.rsqrt(ms + RMS_EPS) * g


def _iota(shape, dim):
    return lax.broadcasted_iota(jnp.int32, shape, dim)


def _block_ones(n, blk):
    return jnp.where(_iota((n, n), 0) // blk == _iota((n, n), 1) // blk, 1.0, 0.0).astype(BF16)


def _row_tile(rows, cap):
    t = cap
    while rows % t:
        t //= 2
    return t


def _inproj_body(*refs, c_a, n_rwkv, c_c, q_scale, kv_transposed):
    h_ref, g_ref, w_ref, bf_ref = refs[0:4]
    qb_ref, k_ref, v_ref, kb_ref, vb_ref, lf_ref, z_ref, u_ref = refs[-8:]
    xn = _rms(h_ref[...], g_ref[...]).astype(BF16)
    q = _dot(xn, w_ref[:, 0:c_a])
    qb_ref[...] = (q * q_scale).astype(BF16)
    k = _dot(xn, w_ref[:, c_a:2 * c_a])
    k_ref[...] = k.T if kv_transposed else k
    kb_ref[...] = k.astype(BF16)
    v = _dot(xn, w_ref[:, 2 * c_a:3 * c_a])
    v_ref[...] = v.T if kv_transposed else v
    vb_ref[...] = v.astype(BF16)
    o = 3 * c_a
    f = _dot(xn, w_ref[:, o:o + LANES]) + bf_ref[...]
    lf_ref[...] = _log_sigmoid(f)
    o += LANES
    z_ref[...] = _dot(xn, w_ref[:, o:o + n_rwkv])
    o += n_rwkv
    u_ref[...] = _dot(xn, w_ref[:, o:o + c_c])


def _inproj(h2, g1, wp, bfp, c_a, n_rwkv, c_c, q_scale, kv_seq=None):
    rows, d = h2.shape
    row = lambda n: pl.BlockSpec((tm, n), lambda i: (i, 0))
    full = lambda a: pl.BlockSpec(a.shape, lambda i: (0,) * a.ndim)
    outs = [(c_a, BF16), (c_a, F32), (c_a, F32), (c_a, BF16), (c_a, BF16), (LANES, F32),
            (n_rwkv, F32), (c_c, F32)]
    args = [h2, g1, wp, bfp]
    aliases = {}
    if kv_seq is None:
        tm = _row_tile(rows, 512)
        out_specs = [row(n) for n, _ in outs]
        out_shape = [jax.ShapeDtypeStruct((rows, n), dt) for n, dt in outs]
        in_specs = [row(d), full(g1), full(wp), full(bfp)]
    else:
        layer, depth, bsz, seq_rows, kt, vt = kv_seq
        tm = _row_tile(seq_rows, 256)
        per_seq = seq_rows // tm
        out_specs = [row(n) for n, _ in outs]
        out_shape = [jax.ShapeDtypeStruct((rows, n), dt) for n, dt in outs]
        kv_spec = pl.BlockSpec((None, None, c_a, tm), lambda i: (layer, i // per_seq, 0, i % per_seq))
        kv_shape = jax.ShapeDtypeStruct((depth, bsz, c_a, seq_rows), F32)
        out_specs[1], out_specs[2] = kv_spec, kv_spec
        out_shape[1], out_shape[2] = kv_shape, kv_shape
        in_specs = [row(d), full(g1), full(wp), full(bfp)]
        if kt is not None:
            args += [kt, vt]
            in_specs += [pl.BlockSpec(memory_space=pl.ANY)] * 2
            aliases = {4: 1, 5: 2}
    return pl.pallas_call(
        functools.partial(_inproj_body, c_a=c_a, n_rwkv=n_rwkv, c_c=c_c, q_scale=q_scale,
                          kv_transposed=kv_seq is not None),
        grid=(rows // tm,),
        in_specs=in_specs,
        out_specs=out_specs,
        out_shape=out_shape,
        input_output_aliases=aliases,
        compiler_params=_cparams("parallel"),
        name="inproj",
    )(*args)


N_BIAS = 3
LOG2E = 1.4426950408889634


def _fox_prep_body(lf_ref, q_ref, k_ref, v_ref, qx_ref, kx_ref, vx_ref, carry_ref, *, tc, n_heads, n_valid):
    @pl.when(pl.program_id(1) == 0)
    def _():
        carry_ref[...] = jnp.zeros_like(carry_ref)

    @pl.when(pl.program_id(1) >= n_valid)
    def _():
        for ref in (qx_ref, kx_ref, vx_ref):
            ref[...] = jnp.zeros_like(ref)

    @pl.when(pl.program_id(1) < n_valid)
    def _():
        _fox_prep_tile(lf_ref, q_ref, k_ref, v_ref, qx_ref, kx_ref, vx_ref, carry_ref, tc, n_heads)


def _fox_prep_tile(lf_ref, q_ref, k_ref, v_ref, qx_ref, kx_ref, vx_ref, carry_ref, tc, n_heads):
    tri = jnp.where(_iota((tc, tc), 0) >= _iota((tc, tc), 1), 1.0, 0.0).astype(BF16)
    cs = _dot_r3(tri, lf_ref[0]) + carry_ref[...]
    carry_ref[...] = cs[tc - 1:tc]
    pieces = jnp.concatenate(_split3(cs * LOG2E), axis=1)

    nl = n_heads * LANES
    src = _iota((N_BIAS * LANES, nl), 0)
    dst = _iota((N_BIAS * LANES, nl), 1)
    base = jnp.where((dst // LANES) % 2 == 0, HEAD_A, 0)
    same_head = src % LANES == dst // LANES
    off = dst % LANES - base
    place_q = jnp.where(same_head & (off == src // LANES), 1.0, 0.0).astype(BF16)
    place_k = jnp.where(same_head & (off == N_BIAS + src // LANES), -1.0, 0.0).astype(BF16)
    off1 = off[0:1]
    qe = _dot(pieces, place_q) + jnp.where((off1 >= N_BIAS) & (off1 < 2 * N_BIAS), 1.0, 0.0)
    ke = _dot(pieces, place_k) + jnp.where((off1 >= 0) & (off1 < N_BIAS), 1.0, 0.0)
    lane_half = _iota((tc, LANES), 1) // HEAD_A
    ones = jnp.ones((tc, LANES), BF16)
    for h in range(n_heads):
        pair = slice((h // 2) * LANES, (h // 2 + 1) * LANES)
        tile = slice(h * LANES, (h + 1) * LANES)
        own = lane_half == h % 2
        qx_ref[0, h] = jnp.where(own, q_ref[0][:, pair], qe[:, tile].astype(BF16))
        kx_ref[0, h] = jnp.where(own, k_ref[0][:, pair], ke[:, tile].astype(BF16))
        vx_ref[0, h] = jnp.where(own, v_ref[0][:, pair], ones)


ATTN_TQ = 256
ATTN_TK = 1024


def _fox_prep(lf3, qb, kb, vb):
    b, lp, c_a = qb.shape
    n_heads = c_a // HEAD_A
    tc = 256
    n_valid = lp // tc
    lkv = -(-lp // ATTN_TK) * ATTN_TK
    row = lambda n: pl.BlockSpec((1, tc, n), lambda i, j: (i, jnp.minimum(j, n_valid - 1), 0))
    out = pl.BlockSpec((1, n_heads, tc, LANES), lambda i, j: (i, 0, j, 0))
    return pl.pallas_call(
        functools.partial(_fox_prep_body, tc=tc, n_heads=n_heads, n_valid=n_valid),
        grid=(b, lkv // tc),
        in_specs=[row(LANES), row(c_a), row(c_a), row(c_a)],
        out_specs=[out, out, out],
        out_shape=[jax.ShapeDtypeStruct((b, n_heads, lkv, LANES), BF16)] * 3,
        scratch_shapes=[pltpu.VMEM((1, LANES), F32)],
        compiler_params=_cparams("parallel", "arbitrary"),
        name="fox_prep",
    )(lf3, qb, kb, vb)


def _attn_body(qx_ref, kx_ref, vx_ref, o_ref, s_scr, *, tq, tk):
    row0 = pl.program_id(2) * tq
    n_full = row0 // tk
    diag = (row0 - n_full * tk) // tq
    q = [qx_ref[0, hh] for hh in range(2)]

    def pass_max(j, ms, width, masked):
        start = pl.multiple_of(j * tk, tk)
        out = []
        for hh in range(2):
            s = _dot_nt(q[hh], kx_ref[0, hh, pl.ds(start, width), :])
            if masked:
                s = jnp.where(_iota((tq, width), 1) <= _iota((tq, width), 0) + (width - tq), s, -jnp.inf)
            s_scr[hh, :, pl.ds(start, width)] = s
            m = ms[hh]
            for t in range(width // LANES):
                m = jnp.maximum(m, s[:, t * LANES:(t + 1) * LANES])
            out.append(m)
        return tuple(out)

    def finish(ms, width):
        ms = pass_max(n_full, ms, width, True)
        m1 = [jnp.broadcast_to(jnp.max(m, axis=1, keepdims=True), (tq, LANES)) for m in ms]

        def pass_acc(j, accs, w):
            start = pl.multiple_of(j * tk, tk)
            out = []
            for hh in range(2):
                mb = jnp.concatenate([m1[hh]] * (w // LANES), axis=1)
                p = jnp.exp2(s_scr[hh, :, pl.ds(start, w)] - mb).astype(BF16)
                out.append(accs[hh] + _dot(p, vx_ref[0, hh, pl.ds(start, w), :]))
            return tuple(out)

        accs = tuple(jnp.zeros((tq, LANES), F32) for _ in range(2))
        accs = lax.fori_loop(0, n_full, lambda j, c: pass_acc(j, c, tk), accs)
        accs = pass_acc(n_full, accs, width)
        outs = [acc / pltpu.roll(acc, HEAD_A, 1) for acc in accs]
        o_ref[0] = jnp.where(_iota((tq, LANES), 1) < HEAD_A, outs[0], outs[1]).astype(BF16)

    ms = tuple(jnp.full((tq, LANES), -jnp.inf, F32) for _ in range(2))
    ms = lax.fori_loop(0, n_full, lambda j, c: pass_max(j, c, tk, False), ms)
    for r in range(tk // tq):
        pl.when(diag == r)(functools.partial(finish, ms, tq * (r + 1)))


def _attn_prompt(qx, kx, vx, lp):
    b, n_heads, lkv, _ = qx.shape
    tq, tk = ATTN_TQ, ATTN_TK
    kv = pl.BlockSpec((1, 2, lkv, LANES), lambda bi, p, i: (bi, p, 0, 0))
    return pl.pallas_call(
        functools.partial(_attn_body, tq=tq, tk=tk),
        grid=(b, n_heads // 2, lp // tq),
        in_specs=[pl.BlockSpec((1, 2, tq, LANES), lambda bi, p, i: (bi, p, i, 0)), kv, kv],
        out_specs=pl.BlockSpec((1, tq, LANES), lambda bi, p, i: (bi, i, p)),
        out_shape=jax.ShapeDtypeStruct((b, lp, n_heads * HEAD_A), BF16),
        scratch_shapes=[pltpu.VMEM((2, tq, lkv), F32)],
        compiler_params=_cparams("parallel", "parallel", "arbitrary"),
        name="fox_prompt_attn",
    )(qx, kx, vx)


def _pool_suffix_body(x_ref, o_ref, *, n_h):
    n = x_ref.shape[1]
    later = jnp.where(_iota((n, n), 0) > _iota((n, n), 1), 1.0, 0.0).astype(BF16)
    whole = jnp.ones((n, n), BF16)
    hi, mid, lo = _split3(x_ref[...])
    pages = x_ref.shape[0] // n_h
    o_ref[:, 0:n_h, :] = (_dot(hi, later) + _dot(mid, later) + _dot(lo, later)).reshape(pages, n_h, n)
    o_ref[:, n_h:, :] = (_dot(hi, whole) + _dot(mid, whole) + _dot(lo, whole)).reshape(pages, n_h, n)


def _pool_suffix(ft2, n_h):
    rows, n = ft2.shape
    tm = _row_tile(rows, 1024)
    return pl.pallas_call(
        functools.partial(_pool_suffix_body, n_h=n_h),
        grid=(rows // tm,),
        in_specs=[pl.BlockSpec((tm, n), lambda i: (i, 0))],
        out_specs=pl.BlockSpec((tm // n_h, 2 * n_h, n), lambda i: (i, 0, 0)),
        out_shape=jax.ShapeDtypeStruct((rows // n_h, 2 * n_h, n), F32),
        compiler_params=_cparams("parallel"),
        name="fox_pool_suffix",
    )(ft2)


def _attn_sample_body(pt_ref, qbd_ref, kn_ref, vn_ref, lfn_ref, *rest, n_pages, t_new):
    kp = rest[0:n_pages]
    vp = rest[n_pages:2 * n_pages]
    lfp = rest[2 * n_pages:3 * n_pages]
    o_ref = rest[3 * n_pages]
    nr, c_a = qbd_ref.shape[1], qbd_ref.shape[2]
    n_h = nr // t_new
    qbd = qbd_ref[0]
    row = _iota((nr, LANES), 0)
    lane = _iota((nr, LANES), 1)

    lfn_pad = jnp.concatenate([lfn_ref[0], jnp.zeros((LANES - t_new, LANES), F32)], axis=0)
    tri = jnp.where(_iota((LANES, LANES), 0) >= _iota((LANES, LANES), 1), 1.0, 0.0).astype(BF16)
    cs = _dot_r3(tri, lfn_pad)
    rep = jnp.where(_iota((nr, LANES), 1) == row // n_h, 1.0, 0.0).astype(BF16)
    cs_rows = _dot_r3(rep, cs)
    cq = jnp.sum(jnp.where(lane == row % n_h, cs_rows, 0.0), axis=1, keepdims=True)
    cst = cs.T[0:n_h]

    tiles = [None] * (n_pages + 1)
    carry = jnp.zeros((n_h, LANES), F32)
    for j in reversed(range(n_pages)):
        kt = kp[j][...].reshape(c_a, kp[j].shape[2]).astype(BF16)
        bias = lfp[j][0:n_h] + carry
        tiles[j] = _dot(qbd, kt) + cq + jnp.concatenate([bias] * t_new, axis=0)
        carry = carry + lfp[j][n_h:]
    s = _dot_nt(qbd, kn_ref[0]) + cq - jnp.concatenate([cst] * t_new, axis=0)
    tiles[n_pages] = jnp.where((lane <= row // n_h) & (lane < t_new), s, -jnp.inf)

    m = tiles[0]
    for tl in tiles[1:]:
        m = jnp.maximum(m, tl)
    m = jnp.max(m, axis=1, keepdims=True)
    lsum = jnp.zeros((nr, LANES), F32)
    acc = jnp.zeros((nr, c_a), F32)
    for j in range(n_pages + 1):
        p = jnp.exp(tiles[j] - m)
        lsum = lsum + p
        if j == n_pages:
            acc = acc + _dot(p.astype(BF16), vn_ref[0])
        else:
            vt = vp[j][...].reshape(c_a, vp[j].shape[2]).astype(BF16)
            acc = acc + _dot_nt(p.astype(BF16), vt)
    o_full = acc / jnp.sum(lsum, axis=1, keepdims=True)
    keep = _iota((nr, c_a), 1) // HEAD_A == _iota((nr, c_a), 0) % n_h
    picked = jnp.where(keep, o_full, 0.0).astype(BF16)
    gather = jnp.where((_iota((nr, nr), 1) // n_h == _iota((nr, nr), 0)), 1.0, 0.0).astype(BF16)
    o_ref[0] = _dot(gather, picked)[0:t_new].astype(BF16)


def _attn_sample(page_table, layer, qbd, kn_pad, vn_pad, lfn, cache_kt, cache_vt, lf_pool):
    db, n_pages = page_table.shape
    nr, c_a = qbd.shape[1], qbd.shape[2]
    t_new = lfn.shape[1]
    n_h, hd, page = cache_kt.shape[2:]

    def seq_spec(a):
        return pl.BlockSpec((1,) + a.shape[1:], lambda b, pt: (b,) + (0,) * (a.ndim - 1))

    def pool_spec(j):
        return pl.BlockSpec((None, None, n_h, hd, page), lambda b, pt, j=j: (layer, pt[b, j], 0, 0, 0))

    def small_spec(j):
        return pl.BlockSpec((None, 2 * n_h, page), lambda b, pt, j=j: (pt[b, j], 0, 0))

    in_specs = ([seq_spec(qbd), seq_spec(kn_pad), seq_spec(vn_pad), seq_spec(lfn)]
                + [pool_spec(j) for j in range(n_pages)] * 2
                + [small_spec(j) for j in range(n_pages)])
    grid_spec = pltpu.PrefetchScalarGridSpec(
        num_scalar_prefetch=1,
        grid=(db,),
        in_specs=in_specs,
        out_specs=pl.BlockSpec((1, t_new, c_a), lambda b, pt: (b, 0, 0)),
    )
    args = ([qbd, kn_pad, vn_pad, lfn] + [cache_kt] * n_pages + [cache_vt] * n_pages
            + [lf_pool] * n_pages)
    return pl.pallas_call(
        functools.partial(_attn_sample_body, n_pages=n_pages, t_new=t_new),
        grid_spec=grid_spec,
        out_shape=jax.ShapeDtypeStruct((db, t_new, c_a), BF16),
        compiler_params=_cparams("parallel"),
        name="fox_paged_attn",
    )(page_table, *args)


BLK = SUBLANES


def _rwkv_body(z_ref, sp_ref, s0_ref, mu_ref, w0_ref, w2_ref, a0_ref, a2_ref, g2_ref, kkw_ref, ka_ref,
               rk_ref, lg_ref, lb_ref, o_ref, sout_ref,
               s_scr, carry_scr, at_s, rt_s, bt_s, kt_s, bh_s, kh_s, eg_s, v_s, g_s, bo_s, y_s,
               *, nb, tc, seq_len, c_b):
    c = pl.program_id(1)
    n_pairs = c_b // LANES

    @pl.when(c == 0)
    def _():
        s_scr[...] = s0_ref[...]
        carry_scr[...] = sp_ref[...]

    q_heads = _block_ones(c_b, HEAD_B)
    rowid = _iota((tc, z_ref.shape[2]), 0)
    lora0 = 3 * c_b
    for b in range(nb):
        z = z_ref[b]
        zprev = jnp.where(rowid == 0, carry_scr[b], pltpu.roll(z, 1, 0))
        carry_scr[b] = z[tc - 1:tc]
        zm = z + (zprev - z) * mu_ref[...]
        r = zm[:, 0:c_b]
        k = zm[:, c_b:2 * c_b]
        v = zm[:, 2 * c_b:3 * c_b]
        x_lora = zm[:, lora0:]
        w = _log_sigmoid(w0_ref[...] + _dot_x3(jnp.tanh(x_lora), w2_ref[...])) - 0.5
        a = _sigmoid(a0_ref[...] + _dot_x3(x_lora, a2_ref[...]))
        g = _dot_x3(_sigmoid(x_lora), g2_ref[...])
        kk = k * kkw_ref[...]
        kk = kk / jnp.maximum(jnp.sqrt(_dot_l2(kk * kk, q_heads)), 1e-12)
        k2 = k * (1.0 + (a - 1.0) * ka_ref[...])
        beta = kk * a
        g_s[b] = g
        bo_s[b] = _dot_l2(r * k2 * rk_ref[...], q_heads) * v
        lw = -jnp.exp(w)
        pos = _iota(lw.shape, 0) % BLK
        cl, sx = lw, lw
        shift = 1
        while shift < BLK:
            cl = cl + jnp.where(pos >= shift, pltpu.roll(cl, shift, 0), 0.0)
            sx = sx + jnp.where(pos < BLK - shift, pltpu.roll(sx, tc - shift, 0), 0.0)
            shift *= 2
        sx = sx - lw
        e_inv = jnp.exp(-cl)
        e_out = jnp.exp(sx)
        at_s[b] = -kk * jnp.exp(cl - lw)
        rt_s[b] = r * jnp.exp(cl)
        bt_s[b] = beta * e_inv
        kt_s[b] = k2 * e_inv
        bh_s[b] = beta * e_out
        kh_s[b] = k2 * e_out
        eg_s[b] = jnp.exp(cl + sx)
        v_s[b] = v

    q_pair = _block_ones(LANES, HEAD_B)
    lane = _iota((BLK, LANES), 1)
    sub = _iota((BLK, LANES), 0)
    head0 = lane < HEAD_B
    n_steps = jnp.clip(seq_len - c * tc, 0, tc)
    chains = [(b, p) for b in range(nb) for p in range(n_pairs)]
    bc = lambda x, j: jnp.broadcast_to(x[j:j + 1], x.shape)

    @pl.when(n_steps < tc)
    def _():
        y_s[...] = jnp.zeros_like(y_s)

    def block(gi, carry):
        rows = pl.ds(pl.multiple_of(gi * BLK, BLK), BLK)
        tiles, pair_hi, pair_lo, m1 = [], [], [], []
        for b, p in chains:
            cs = slice(p * LANES, (p + 1) * LANES)
            at, rt, bt, kt = at_s[b, rows, cs], rt_s[b, rows, cs], bt_s[b, rows, cs], kt_s[b, rows, cs]
            tiles.append((at, rt))
            fb, out = [], []
            for j in range(BLK):
                bj, kj = bc(bt, j), bc(kt, j)
                if j < BLK - 1:
                    at_j = jnp.where(sub > j, at, 0.0)
                    fb += [at_j * bj, at_j * kj]
                rt_j = jnp.where(sub >= j, rt, 0.0)
                out += [rt_j * bj, rt_j * kj]
            fb = jnp.concatenate(fb, axis=0)
            fb_hi = fb.astype(BF16)
            pair_hi += [fb_hi, jnp.concatenate(out, axis=0).astype(BF16)]
            pair_lo.append((fb - fb_hi.astype(F32)).astype(BF16))
            lhs = jnp.concatenate([jnp.where(head0, at, 0.0), jnp.where(head0, 0.0, at),
                                   jnp.where(head0, rt, 0.0), jnp.where(head0, 0.0, rt)], axis=0)
            lhs_hi = lhs.astype(BF16)
            lhs_lo = (lhs - lhs_hi.astype(F32)).astype(BF16)
            st_hi, st_lo = _split2(s_scr[b, p])
            z = _dot_nt(jnp.concatenate([lhs_hi, lhs_lo], axis=0), st_hi)
            z = z[0:4 * BLK] + z[4 * BLK:] + _dot_nt(lhs_hi, st_lo)
            m1.append(z)
        n_fb, n_out = 2 * (BLK - 1) * BLK, 2 * BLK * BLK
        dots_hi = _dot(jnp.concatenate(pair_hi, axis=0), q_pair)
        dots_lo = _dot(jnp.concatenate(pair_lo, axis=0), q_pair)
        for ci, (b, p) in enumerate(chains):
            cs = slice(p * LANES, (p + 1) * LANES)
            o = ci * (n_fb + n_out)
            fb = dots_hi[o:o + n_fb] + dots_lo[ci * n_fb:(ci + 1) * n_fb]
            out = dots_hi[o + n_fb:o + n_fb + n_out]
            z = m1[ci]
            vv = v_s[b, rows, cs]
            u = jnp.concatenate([z[0:BLK], z[BLK:2 * BLK]], axis=1)
            y = jnp.concatenate([z[2 * BLK:3 * BLK], z[3 * BLK:]], axis=1)
            for j in range(BLK - 1):
                u = u + fb[(2 * j + 1) * BLK:(2 * j + 2) * BLK] * bc(vv, j)
            for j in range(BLK - 1):
                u = u + fb[2 * j * BLK:(2 * j + 1) * BLK] * bc(u, j)
            for j in range(BLK):
                y = y + out[2 * j * BLK:(2 * j + 1) * BLK] * bc(u, j) + out[(2 * j + 1) * BLK:(2 * j + 2) * BLK] * bc(vv, j)
            y_s[b, rows, cs] = y
            u_hi = u.astype(BF16).astype(F32)
            v_hi = vv.astype(BF16).astype(F32)
            bh, kh = bh_s[b, rows, cs], kh_s[b, rows, cs]
            bh_hi = bh.astype(BF16).astype(F32)
            kh_hi = kh.astype(BF16).astype(F32)
            left = jnp.concatenate([u_hi, u - u_hi, u_hi, v_hi, vv - v_hi, v_hi], axis=0).astype(BF16)
            right = jnp.concatenate([bh_hi, bh_hi, bh - bh_hi, kh_hi, kh_hi, kh - kh_hi], axis=0).astype(BF16)
            upd = lax.dot_general(left, right, (((0,), (0,)), ((), ())), preferred_element_type=F32)
            upd = jnp.where(_iota((HEAD_B, LANES), 1) < HEAD_B, upd[0:HEAD_B], upd[HEAD_B:])
            s_scr[b, p] = s_scr[b, p] * eg_s[b, rows, cs][0:1] + upd
        return carry

    lax.fori_loop(0, n_steps // BLK, block, 0)

    inv_n = 1.0 / HEAD_B
    for b in range(nb):
        y = y_s[b]
        mu = _dot_l2(y, q_heads) * inv_n
        d = y - mu
        var = _dot_l2(d * d, q_heads) * inv_n
        yn = d * lax.rsqrt(var + GROUPNORM_EPS) * lg_ref[...] + lb_ref[...]
        o_ref[b] = ((yn + bo_s[b]) * g_s[b]).astype(BF16)
    sout_ref[...] = s_scr[...]


def _rwkv(z3, shift_prev, s0p, prm, seq_len, tc):
    bt, lp, n_rwkv = z3.shape
    nb = 4
    c_b = prm["w0"].shape[1]
    n_pairs = c_b // LANES
    full = lambda a: pl.BlockSpec(a.shape, lambda g, c: (0,) * a.ndim)
    names = ["mu", "w0", "w2", "a0", "a2", "g2", "kkw", "ka", "rk", "lg", "lb"]
    plist = [prm[n] for n in names]
    assert tc % BLK == 0 and seq_len % BLK == 0
    chunk = lambda: pltpu.VMEM((nb, tc, c_b), F32)
    return pl.pallas_call(
        functools.partial(_rwkv_body, nb=nb, tc=tc, seq_len=seq_len, c_b=c_b),
        grid=(bt // nb, lp // tc),
        in_specs=[pl.BlockSpec((nb, tc, n_rwkv), lambda g, c: (g, c, 0)),
                  pl.BlockSpec((nb, 1, n_rwkv), lambda g, c: (g, 0, 0)),
                  pl.BlockSpec((nb, n_pairs, HEAD_B, LANES), lambda g, c: (g, 0, 0, 0))]
                 + [full(a) for a in plist],
        out_specs=[pl.BlockSpec((nb, tc, c_b), lambda g, c: (g, c, 0)),
                   pl.BlockSpec((nb, n_pairs, HEAD_B, LANES), lambda g, c: (g, 0, 0, 0))],
        out_shape=[jax.ShapeDtypeStruct((bt, lp, c_b), BF16),
                   jax.ShapeDtypeStruct((bt, n_pairs, HEAD_B, LANES), F32)],
        scratch_shapes=[pltpu.VMEM((nb, n_pairs, HEAD_B, LANES), F32),
                        pltpu.VMEM((nb, 1, n_rwkv), F32)] + [chunk() for _ in range(11)],
        compiler_params=_cparams("parallel", "arbitrary"),
        name="rwkv7",
    )(z3, shift_prev, s0p, *plist)


def _s5_prep_body(are_ref, aim_ref, ldt_ref, bre_ref, bim_ref, abre_ref, abim_ref, bbre_ref, bbim_ref):
    a_re = are_ref[...]
    a_im = aim_ref[...]
    step = jnp.exp(ldt_ref[...])
    mag = jnp.exp(step * a_re)
    ab_re = mag * jnp.cos(step * a_im)
    ab_im = mag * jnp.sin(step * a_im)
    den = a_re * a_re + a_im * a_im
    cf_re = ((ab_re - 1.0) * a_re + ab_im * a_im) / den
    cf_im = (ab_im * a_re - (ab_re - 1.0) * a_im) / den
    abre_ref[...] = ab_re
    abim_ref[...] = ab_im
    bbre_ref[...] = cf_re * bre_ref[...] - cf_im * bim_ref[...]
    bbim_ref[...] = cf_re * bim_ref[...] + cf_im * bre_ref[...]


def _s5_prep(a_re, a_im, log_dt, b_re, b_im):
    g, p = a_re.shape
    grp = b_re.shape[2]
    r3 = lambda x: x.reshape(g, 1, p)
    ldt = jnp.broadcast_to(log_dt[:, None, None], (g, 1, p))
    bt = lambda x: jnp.transpose(x, (0, 2, 1))
    vec = jax.ShapeDtypeStruct((g, 1, p), F32)
    mat = jax.ShapeDtypeStruct((g, grp, p), F32)
    return pl.pallas_call(_s5_prep_body, out_shape=[vec, vec, mat, mat], name="s5_prep")(
        r3(a_re), r3(a_im), ldt, bt(b_re), bt(b_im))


def _gelu_tanh(x):
    return 0.5 * x * (1.0 + jnp.tanh(0.7978845608028654 * (x + 0.044715 * (x * x * x))))


def _s5_body(u_ref, re0_ref, im0_ref, bb_ref, ab_ref, cre_ref, cim_ref, d_ref, gw_ref, gb_ref,
             o_ref, reo_ref, imo_ref, xr_scr, xi_scr, xr_s, xi_s,
             *, nb, tc, seq_len, precise):
    c = pl.program_id(1)
    n_tiles = xr_s.shape[0]
    n_state = n_tiles * LANES
    lanes = lambda j: slice(j * LANES, (j + 1) * LANES)
    seq_rows = lambda b: pl.ds(b, tc, stride=SUBLANES)

    @pl.when(c == 0)
    def _():
        for j in range(n_tiles):
            xr_scr[j] = re0_ref[0][:, lanes(j)]
            xi_scr[j] = im0_ref[0][:, lanes(j)]
        if nb < SUBLANES:
            xr_s[...] = jnp.zeros_like(xr_s)
            xi_s[...] = jnp.zeros_like(xi_s)

    u = u_ref[...].reshape(nb * tc, u_ref.shape[2])
    for half, dst in enumerate((xr_s, xi_s)):
        cols = slice(half * n_state, (half + 1) * n_state)
        if precise:
            uh, ul = _split2(u)
            bu = _dot(uh, bb_ref[0, :, cols]) + _dot(ul, bb_ref[0, :, cols]) + _dot(uh, bb_ref[1, :, cols])
        else:
            bu = _dot(u.astype(BF16), bb_ref[0, :, cols])
        for b in range(nb):
            for j in range(n_tiles):
                dst[j, seq_rows(b), :] = bu[b * tc:(b + 1) * tc, lanes(j)]
    ar = ab_ref[0]
    ai = ab_ref[1]

    def step(t, carry):
        xr, xi = carry
        rows = pl.ds(pl.multiple_of(t * SUBLANES, SUBLANES), SUBLANES)
        nr = ar * xr - ai * xi + xr_s[:, rows, :]
        ni = ar * xi + ai * xr + xi_s[:, rows, :]
        xr_s[:, rows, :] = nr
        xi_s[:, rows, :] = ni
        return nr, ni

    n_steps = jnp.clip(seq_len - c * tc, 0, tc)
    xr, xi = lax.fori_loop(0, n_steps, step, (xr_scr[...], xi_scr[...]))
    xr_scr[...] = xr
    xi_scr[...] = xi
    reo_ref[0] = jnp.concatenate([xr[j] for j in range(n_tiles)], axis=1)
    imo_ref[0] = jnp.concatenate([xi[j] for j in range(n_tiles)], axis=1)

    def states(src):
        return jnp.concatenate([jnp.concatenate([src[j, seq_rows(b), :] for j in range(n_tiles)], axis=1)
                                for b in range(nb)], axis=0).astype(BF16)

    y = _dot(states(xr_s), cre_ref[...]) - _dot(states(xi_s), cim_ref[...]) + d_ref[...] * u
    hg = _gelu_tanh(y)
    out = hg * _sigmoid(_dot_x3(hg, gw_ref[...]) + gb_ref[...])
    o_ref[...] = out.astype(BF16).reshape(o_ref.shape)


def _s5(u3, re0, im0, prm, seq_len, nb, tc, precise):
    bt, lp, c_c = u3.shape
    n_state = re0.shape[2]
    full = lambda a: pl.BlockSpec(a.shape, lambda g, c: (0,) * a.ndim)
    plist = [prm[n] for n in ("bb", "ab", "cre", "cim", "d", "gw", "gb")]
    st = pl.BlockSpec((1, SUBLANES, n_state), lambda g, c: (g, 0, 0))
    st_shape = jax.ShapeDtypeStruct((bt // nb, SUBLANES, n_state), F32)
    rows = tc * SUBLANES
    n_tiles = n_state // LANES
    return pl.pallas_call(
        functools.partial(_s5_body, nb=nb, tc=tc, seq_len=seq_len, precise=precise),
        grid=(bt // nb, lp // tc),
        in_specs=[pl.BlockSpec((nb, tc, c_c), lambda g, c: (g, c, 0)), st, st] + [full(a) for a in plist],
        out_specs=[pl.BlockSpec((nb, tc, c_c), lambda g, c: (g, c, 0)), st, st],
        out_shape=[jax.ShapeDtypeStruct((bt, lp, c_c), BF16), st_shape, st_shape],
        scratch_shapes=[pltpu.VMEM((n_tiles, SUBLANES, LANES), F32), pltpu.VMEM((n_tiles, SUBLANES, LANES), F32),
                        pltpu.VMEM((n_tiles, rows, LANES), F32), pltpu.VMEM((n_tiles, rows, LANES), F32)],
        compiler_params=_cparams("parallel", "arbitrary"),
        name="s5",
    )(u3, re0, im0, *plist)


def _merge_body(h_ref, oa_ref, ob_ref, oc_ref, g_ref, wg_ref, pa_ref, pb_ref, pc_ref, wo_ref, o_ref, *, d):
    h = h_ref[...]
    xn = _rms(h, g_ref[...]).astype(BF16)
    merged = None
    for n, (br_ref, pj_ref) in enumerate(((oa_ref, pa_ref), (ob_ref, pb_ref), (oc_ref, pc_ref))):
        gate = _sigmoid(_dot(xn, wg_ref[:, n * d:(n + 1) * d]))
        term = gate * _dot(br_ref[...], pj_ref[...])
        merged = term if merged is None else merged + term
    o_ref[...] = h + _dot(merged.astype(BF16), wo_ref[...])


def _merge(h2, oa, ob, oc, g1, wg, pa, pb, pc, wo):
    rows, d = h2.shape
    tm = _row_tile(rows, 512)
    row = lambda a: pl.BlockSpec((tm, a.shape[1]), lambda i: (i, 0))
    full = lambda a: pl.BlockSpec(a.shape, lambda i: (0,) * a.ndim)
    return pl.pallas_call(
        functools.partial(_merge_body, d=d),
        grid=(rows // tm,),
        in_specs=[row(h2), row(oa), row(ob), row(oc)] + [full(a) for a in (g1, wg, pa, pb, pc, wo)],
        out_specs=row(h2),
        out_shape=jax.ShapeDtypeStruct(h2.shape, F32),
        compiler_params=_cparams("parallel"),
        name="merge_outproj",
    )(h2, oa, ob, oc, g1, wg, pa, pb, pc, wo)


PREV_ROWS = 16


def _ffn_body(*refs, tm, seq_rows, seq_valid, decode):
    if decode:
        (x_ref, pa_ref, pb_ref, g_ref, ua_ref, ub_ref, cwa_ref, cwb_ref, cba_ref, cbb_ref, dn_ref,
         o_ref, xn_scr) = refs
    else:
        (x_ref, xp_ref, g_ref, ua_ref, ub_ref, cwa_ref, cwb_ref, cba_ref, cbb_ref, dn_ref,
         o_ref, xn_scr) = refs
    i = pl.program_id(0)
    j = pl.program_id(1)

    @pl.when(j == 0)
    def _():
        o_ref[...] = x_ref[...]
        if decode:
            xn_scr[...] = _rms(x_ref[...], g_ref[...]).astype(BF16)
        else:
            d = x_ref.shape[1]
            pos_p = (i * tm - PREV_ROWS + _iota((PREV_ROWS, d), 0)) % seq_rows
            keep = (pos_p < seq_valid) & (i > 0)
            xn_scr[0:PREV_ROWS] = jnp.where(keep, _rms(xp_ref[...], g_ref[...]), 0.0).astype(BF16)
            pos = (i * tm + _iota((tm, d), 0)) % seq_rows
            xn_scr[PREV_ROWS:] = jnp.where(pos < seq_valid, _rms(x_ref[...], g_ref[...]), 0.0).astype(BF16)

    nc = ua_ref.shape[1]
    xn = xn_scr[...]

    def conv(up_ref, cw_ref, cb_ref, prev_ref):
        h_ext = _dot(xn, up_ref[...])
        if decode:
            hu = h_ext
            prev = prev_ref[...]
            pos = _iota((tm, nc), 0) % seq_rows
            sh1 = jnp.where(pos == 0, pltpu.roll(prev, tm - 1, 0), pltpu.roll(hu, 1, 0))
            sh2 = jnp.where(pos < 2, prev, pltpu.roll(hu, 2, 0))
        else:
            hu = h_ext[PREV_ROWS:]
            last = h_ext[PREV_ROWS - 1:PREV_ROWS]
            last2 = h_ext[PREV_ROWS - 2:PREV_ROWS - 1]
            r1 = pltpu.roll(hu, 1, 0)
            r2 = pltpu.roll(hu, 2, 0)
            row8 = _iota((SUBLANES, nc), 0)
            head1 = jnp.where(row8 == 0, last, r1[0:SUBLANES])
            head2 = jnp.where(row8 == 0, last2, jnp.where(row8 == 1, last, r2[0:SUBLANES]))
            sh1 = jnp.concatenate([head1, r1[SUBLANES:]], axis=0)
            sh2 = jnp.concatenate([head2, r2[SUBLANES:]], axis=0)
        return cw_ref[0:1, :] * sh2 + cw_ref[1:2, :] * sh1 + cw_ref[2:3, :] * hu + cb_ref[...]

    a = conv(ua_ref, cwa_ref, cba_ref, pa_ref if decode else None)
    bgate = conv(ub_ref, cwb_ref, cbb_ref, pb_ref if decode else None)
    act = (a * _sigmoid(a)) * bgate
    o_ref[...] += _dot(act.astype(BF16), dn_ref[...])


def _ffn(h2, g2, ua, ub, cwa, cwb, cba, cbb, dn, seq_rows, seq_valid, prev_pad=None):
    assert prev_pad is not None or seq_rows - seq_valid >= CONV_TAPS - 1
    rows, d = h2.shape
    d_ff = ua.shape[1]
    nc = 256
    n_chunks = d_ff // nc
    decode = prev_pad is not None
    tm = _row_tile(rows, 1024)
    xrow = pl.BlockSpec((tm, d), lambda i, j: (i, 0))
    col = lambda r: pl.BlockSpec((r, nc), lambda i, j: (0, j))
    w_specs = [pl.BlockSpec((1, d), lambda i, j: (0, 0)), col(d), col(d), col(CONV_TAPS), col(CONV_TAPS),
               col(1), col(1), pl.BlockSpec((nc, d), lambda i, j: (j, 0))]
    scratch = [pltpu.VMEM((tm if decode else tm + PREV_ROWS, d), BF16)]
    if decode:
        in_specs = [xrow, pl.BlockSpec((tm, nc), lambda i, j: (i, j)),
                    pl.BlockSpec((tm, nc), lambda i, j: (i, n_chunks + j))] + w_specs
        args = (h2, prev_pad, prev_pad, g2, ua, ub, cwa, cwb, cba, cbb, dn)
    else:
        per = tm // PREV_ROWS
        in_specs = [xrow, pl.BlockSpec((PREV_ROWS, d), lambda i, j: (jnp.maximum(i * per - 1, 0), 0))] + w_specs
        args = (h2, h2, g2, ua, ub, cwa, cwb, cba, cbb, dn)
    return pl.pallas_call(
        functools.partial(_ffn_body, tm=tm, seq_rows=seq_rows, seq_valid=seq_valid, decode=decode),
        grid=(rows // tm, n_chunks),
        in_specs=in_specs,
        out_specs=xrow,
        out_shape=jax.ShapeDtypeStruct(h2.shape, F32),
        scratch_shapes=scratch,
        compiler_params=_cparams("parallel", "arbitrary"),
        name="conv_ffn",
    )(*args)


def _up_rows_body(x_ref, g_ref, up_ref, o_ref):
    o_ref[...] = _dot(_rms(x_ref[...], g_ref[...]).astype(BF16), up_ref[...])


def _up_rows(x2, g2, up):
    rows, d = x2.shape
    n = up.shape[1]
    nc = 512
    return pl.pallas_call(
        _up_rows_body,
        grid=(n // nc,),
        in_specs=[pl.BlockSpec((rows, d), lambda j: (0, 0)), pl.BlockSpec((1, d), lambda j: (0, 0)),
                  pl.BlockSpec((d, nc), lambda j: (0, j))],
        out_specs=pl.BlockSpec((rows, nc), lambda j: (0, j)),
        out_shape=jax.ShapeDtypeStruct((rows, n), F32),
        compiler_params=_cparams("parallel"),
        name="conv_state_rows",
    )(x2, g2, up)


def _final_norm_body(x_ref, g_ref, o_ref):
    o_ref[...] = _rms(x_ref[...], g_ref[...])


def _final_norm(h2, g):
    rows, d = h2.shape
    tm = _row_tile(rows, 1024)
    spec = pl.BlockSpec((tm, d), lambda i: (i, 0))
    return pl.pallas_call(
        _final_norm_body,
        grid=(rows // tm,),
        in_specs=[spec, pl.BlockSpec((1, d), lambda i: (0, 0))],
        out_specs=spec,
        out_shape=jax.ShapeDtypeStruct(h2.shape, F32),
        compiler_params=_cparams("parallel"),
        name="final_norm",
    )(h2, g)


def _pack_state(s):
    b, h, v, k = s.shape
    return s.reshape(b, h // 2, 2, v, k).transpose(0, 1, 3, 2, 4).reshape(b, h // 2, v, 2 * k)


def _unpack_state(s):
    b, hp, v, k2 = s.shape
    return s.reshape(b, hp, v, 2, k2 // 2).transpose(0, 1, 3, 2, 4).reshape(b, 2 * hp, v, k2 // 2)


def _block_diag(x):
    g, a, c = x.shape
    eye = jnp.eye(g, dtype=bool)[:, None, :, None]
    return jnp.where(eye, x[:, :, None, :], 0.0).reshape(g * a, g * c)


def _pad_rows(x, n):
    return jnp.concatenate([x, jnp.zeros((n - x.shape[0],) + x.shape[1:], x.dtype)], axis=0)


def kernel(x_prompt, x_sample, cache_k, cache_v, cache_logf, page_table, state_rwkv_wkv, state_rwkv_shift, state_ssm_re, state_ssm_im, state_ffn_conv, meta_tokens, norm1_g, w_in, fox_bf, rwkv_mu, rwkv_w0, rwkv_w2, rwkv_a0, rwkv_a2, rwkv_g2, rwkv_kk, rwkv_ka, rwkv_rk, rwkv_lnx_g, rwkv_lnx_b, ssm_a_re, ssm_a_im, ssm_log_dt, ssm_b_re, ssm_b_im, ssm_c_re, ssm_c_im, ssm_d, ssm_glu_w, ssm_glu_b, proj_a, proj_b, proj_c, w_out, norm2_g, ffn_up, ffn_conv_w, ffn_conv_b, ffn_down, final_norm_g):
    bsz, seq, d = x_prompt.shape
    db, t_new, _ = x_sample.shape
    depth = norm1_g.shape[0]
    n_ha = fox_bf.shape[1]
    c_a = n_ha * HEAD_A
    n_rwkv = rwkv_mu.shape[1]
    c_b = rwkv_w0.shape[1]
    n_hb = c_b // HEAD_B
    g_c, p_c = ssm_a_re.shape[1], ssm_a_re.shape[2]
    c_c = g_c * SSM_GRP
    n_state = g_c * p_c
    d_ff = ffn_down.shape[1]
    n_pool, page = cache_k.shape[1], cache_k.shape[2]
    seq_len = N_META_TOK + seq
    lp = -(-seq_len // 256) * 256
    seq_tile = 256

    meta = jnp.broadcast_to(meta_tokens[None], (bsz, N_META_TOK, d))
    h_p = jnp.concatenate([meta, x_prompt, jnp.zeros((bsz, lp - seq_len, d), F32)], axis=1)
    h_p = h_p.reshape(bsz * lp, d)
    h_s = x_sample.reshape(db * t_new, d)

    cache_kt = jnp.transpose(cache_k, (0, 1, 3, 4, 2))
    cache_vt = jnp.transpose(cache_v, (0, 1, 3, 4, 2))
    eye_h = jnp.eye(n_ha, dtype=bool)

    zeros_shift = jnp.zeros((bsz, 1, n_rwkv), F32)
    zeros_wkv = jnp.zeros((bsz, n_hb // 2, HEAD_B, 2 * HEAD_B), F32)
    assert bsz <= SUBLANES and db % SUBLANES == 0
    zeros_ssm = jnp.zeros((1, SUBLANES, n_state), F32)

    outs_p, outs_s = [], []
    kt_p = vt_p = None
    for l in range(depth):
        w = w_in[l]
        o_f = 3 * c_a
        o_z = o_f + n_ha
        o_g = o_z + n_rwkv + c_c
        wp = jnp.concatenate([w[:, :o_f], w[:, o_f:o_z], jnp.zeros((d, LANES - n_ha), F32),
                              w[:, o_z:o_g]], axis=1).astype(BF16)
        wg = w[:, o_g:].astype(BF16)
        bfp = jnp.concatenate([fox_bf[l], jnp.zeros((LANES - n_ha,), F32)])[None]
        g1 = norm1_g[l][None]
        g2 = norm2_g[l][None]
        row1 = lambda x: x.reshape(1, -1)
        lora_pad = lambda x, o: jnp.zeros((n_rwkv - 3 * c_b, c_b), F32).at[o:o + x.shape[0]].set(x)
        rw = dict(mu=row1(rwkv_mu[l]), w0=row1(rwkv_w0[l]), w2=lora_pad(rwkv_w2[l], 0),
                  a0=row1(rwkv_a0[l]), a2=lora_pad(rwkv_a2[l], LORA_W_DIM),
                  g2=lora_pad(rwkv_g2[l], LORA_W_DIM + LORA_A_DIM), kkw=row1(rwkv_kk[l]),
                  ka=row1(rwkv_ka[l]), rk=row1(rwkv_rk[l]), lg=row1(rwkv_lnx_g[l]), lb=row1(rwkv_lnx_b[l]))
        ab_re, ab_im, bbt_re, bbt_im = _s5_prep(ssm_a_re[l], ssm_a_im[l], ssm_log_dt[l], ssm_b_re[l], ssm_b_im[l])
        bb = jnp.concatenate([_block_diag(bbt_re), _block_diag(bbt_im)], axis=1)
        bb_hi = bb.astype(BF16)
        bb_lo = (bb - bb_hi.astype(F32)).astype(BF16)
        s5p = dict(bb=jnp.stack([bb_hi, bb_lo]),
                   ab=jnp.stack([ab_re, ab_im]).reshape(2, n_state // LANES, 1, LANES),
                   cre=_block_diag(jnp.transpose(ssm_c_re[l], (0, 2, 1))).astype(BF16),
                   cim=_block_diag(jnp.transpose(ssm_c_im[l], (0, 2, 1))).astype(BF16),
                   d=row1(ssm_d[l]), gw=ssm_glu_w[l], gb=row1(ssm_glu_b[l]))
        pa, pb, pc, wo = (x[l].astype(BF16) for x in (proj_a, proj_b, proj_c, w_out))
        up = ffn_up[l].astype(BF16)
        ua, ub = up[:, :d_ff], up[:, d_ff:]
        cwa, cwb = ffn_conv_w[l][:, :d_ff], ffn_conv_w[l][:, d_ff:]
        cba, cbb = ffn_conv_b[l][None, :d_ff], ffn_conv_b[l][None, d_ff:]
        dn = ffn_down[l].astype(BF16)

        qb, kt_p, vt_p, kb, vb, lf, z, u = _inproj(h_p, g1, wp, bfp, c_a, n_rwkv, c_c, HEAD_A ** -0.5 * LOG2E,
                                                   kv_seq=(l, depth, bsz, lp, kt_p, vt_p))
        r3 = lambda x: x.reshape(bsz, lp, x.shape[1])
        o_a = _attn_prompt(*_fox_prep(r3(lf), r3(qb), r3(kb), r3(vb)), lp)
        o_b, wkv = _rwkv(r3(z), zeros_shift, zeros_wkv, rw, seq_len, seq_tile)
        o_c, re_n, im_n = _s5(r3(u), zeros_ssm, zeros_ssm, s5p, seq_len, bsz, seq_tile // 2, False)
        flat = lambda x: x.reshape(bsz * lp, x.shape[2])
        h_mid = _merge(h_p, flat(o_a), flat(o_b), flat(o_c), g1, wg, pa, pb, pc, wo)
        h_p = _ffn(h_mid, g2, ua, ub, cwa, cwb, cba, cbb, dn, lp, seq_len)
        tail = r3(h_mid)[:, seq_len - 2:seq_len].reshape(bsz * 2, d)
        conv = _up_rows(_pad_rows(tail, 16), g2, up)[:bsz * 2].reshape(bsz, 2, 2 * d_ff)
        outs_p.append((r3(lf)[:, :seq_len, :n_ha], _unpack_state(wkv), r3(z)[:, seq_len - 1],
                       re_n[0, :bsz].reshape(bsz, g_c, p_c), im_n[0, :bsz].reshape(bsz, g_c, p_c), conv))

        qb, k, v, kb, vb, lf, z, u = _inproj(h_s, g1, wp, bfp, c_a, n_rwkv, c_c, HEAD_A ** -0.5)
        s3 = lambda x: x.reshape(db, t_new, x.shape[1])
        q4 = s3(qb).reshape(db, t_new, n_ha, HEAD_A)
        qbd = jnp.where(eye_h[None, None, :, :, None], q4[:, :, :, None, :], jnp.zeros((), BF16))
        qbd = qbd.reshape(db, t_new * n_ha, c_a)
        pad_new = lambda x: jnp.concatenate([s3(x), jnp.zeros((db, LANES - t_new, c_a), BF16)], axis=1)
        lf_pool = _pool_suffix(jnp.transpose(cache_logf[l], (0, 2, 1)).reshape(n_pool * n_ha, page), n_ha)
        o_a = _attn_sample(page_table, l, qbd, pad_new(kb), pad_new(vb), s3(lf), cache_kt, cache_vt, lf_pool)
        o_b, wkv = _rwkv(s3(z), state_rwkv_shift[l][:, None], _pack_state(state_rwkv_wkv[l]), rw, t_new, t_new)
        o_c, re_n, im_n = _s5(s3(u), state_ssm_re[l].reshape(db // SUBLANES, SUBLANES, n_state),
                              state_ssm_im[l].reshape(db // SUBLANES, SUBLANES, n_state), s5p, t_new,
                              SUBLANES, t_new, True)
        flat = lambda x: x.reshape(db * t_new, x.shape[2])
        h_mid = _merge(h_s, flat(o_a), flat(o_b), flat(o_c), g1, wg, pa, pb, pc, wo)
        prev = state_ffn_conv[l]
        prev_pad = jnp.concatenate([prev, jnp.zeros((db, t_new - prev.shape[1], prev.shape[2]), F32)], axis=1)
        h_s = _ffn(h_mid, g2, ua, ub, cwa, cwb, cba, cbb, dn, t_new, t_new,
                   prev_pad.reshape(db * t_new, 2 * d_ff))
        tail = s3(h_mid)[:, t_new - 2:].reshape(db * 2, d)
        conv = _up_rows(tail, g2, up).reshape(db, 2, 2 * d_ff)
        outs_s.append((s3(k).reshape(db, t_new, n_ha, HEAD_A), s3(v).reshape(db, t_new, n_ha, HEAD_A),
                       s3(lf)[:, :, :n_ha], _unpack_state(wkv), s3(z)[:, t_new - 1],
                       re_n.reshape(db, g_c, p_c), im_n.reshape(db, g_c, p_c), conv))

    fg = final_norm_g[None]
    y_p = _final_norm(h_p, fg).reshape(bsz, lp, d)[:, N_META_TOK:seq_len]
    y_s = _final_norm(h_s, fg).reshape(db, t_new, d)
    stack = lambda outs: [jnp.stack(t) for t in zip(*outs)]
    seq_last = lambda x: jnp.transpose(x[..., :seq_len].reshape(depth, bsz, n_ha, HEAD_A, seq_len),
                                       (0, 1, 4, 2, 3))
    return tuple([y_p, y_s, seq_last(kt_p), seq_last(vt_p)] + stack(outs_p) + stack(outs_s))
```

```python
import functools

import jax
import jax.numpy as jnp
from jax import lax
from jax.experimental import pallas as pl
from jax.experimental.pallas import tpu as pltpu

F32 = jnp.float32
BF16 = jnp.bfloat16

N_META_TOK = 16
RMS_EPS = 1e-6
HEAD_A = 64
HEAD_B = 64
LORA_W_DIM = 64
LORA_A_DIM = 64
LORA_G_DIM = 128
GROUPNORM_EPS = 64e-5
SSM_GRP = 16
SSM_P = 64
CONV_TAPS = 3
LANES = 128
SUBLANES = 8
VMEM_LIMIT = 52 * 1024 * 1024


def _cparams(*sem):
    return pltpu.CompilerParams(dimension_semantics=sem, vmem_limit_bytes=VMEM_LIMIT)


def _dot(a, b):
    return jnp.dot(a, b, preferred_element_type=F32)


def _dot_nt(a, b):
    return lax.dot_general(a, b, (((1,), (1,)), ((), ())), preferred_element_type=F32)


def _split2(x):
    hi = x.astype(BF16)
    lo = (x - hi.astype(F32)).astype(BF16)
    return hi, lo


def _split3(x):
    hi = x.astype(BF16)
    r = x - hi.astype(F32)
    mid = r.astype(BF16)
    lo = (r - mid.astype(F32)).astype(BF16)
    return hi, mid, lo


def _dot_l2(x, w_bf):
    hi, lo = _split2(x)
    return _dot(hi, w_bf) + _dot(lo, w_bf)


def _dot_r3(w_bf, x):
    hi, mid, lo = _split3(x)
    return _dot(w_bf, hi) + _dot(w_bf, mid) + _dot(w_bf, lo)


def _dot_x3(a, b):
    ah, al = _split2(a)
    bh, bl = _split2(b)
    return _dot(ah, bh) + _dot(al, bh) + _dot(ah, bl)


def _sigmoid(x):
    return 1.0 / (1.0 + jnp.exp(-x))


def _log_sigmoid(x):
    return jnp.minimum(x, 0.0) - jnp.log1p(jnp.exp(-jnp.abs(x)))


def _rms(x, g):
    ms = jnp.mean(x * x, axis=-1, keepdims=True)
    return x * lax.rsqrt(ms + RMS_EPS) * g


def _iota(shape, dim):
    return lax.broadcasted_iota(jnp.int32, shape, dim)


def _block_ones(n, blk):
    return jnp.where(_iota((n, n), 0) // blk == _iota((n, n), 1) // blk, 1.0, 0.0).astype(BF16)


def _row_tile(rows, cap):
    t = cap
    while rows % t:
        t //= 2
    return t


def _inproj_body(*refs, c_a, n_rwkv, c_c, q_scale, kv_transposed):
    h_ref, g_ref, w_ref, bf_ref = refs[0:4]
    qb_ref, k_ref, v_ref, kb_ref, vb_ref, lf_ref, z_ref, u_ref = refs[-8:]
    xn = _rms(h_ref[...], g_ref[...]).astype(BF16)
    q = _dot(xn, w_ref[:, 0:c_a])
    qb_ref[...] = (q * q_scale).astype(BF16)
    k = _dot(xn, w_ref[:, c_a:2 * c_a])
    k_ref[...] = k.T if kv_transposed else k
    kb_ref[...] = k.astype(BF16)
    v = _dot(xn, w_ref[:, 2 * c_a:3 * c_a])
    v_ref[...] = v.T if kv_transposed else v
    vb_ref[...] = v.astype(BF16)
    o = 3 * c_a
    f = _dot(xn, w_ref[:, o:o + LANES]) + bf_ref[...]
    lf_ref[...] = _log_sigmoid(f)
    o += LANES
    z_ref[...] = _dot(xn, w_ref[:, o:o + n_rwkv])
    o += n_rwkv
    u_ref[...] = _dot(xn, w_ref[:, o:o + c_c])


def _inproj(h2, g1, wp, bfp, c_a, n_rwkv, c_c, q_scale, kv_seq=None):
    rows, d = h2.shape
    row = lambda n: pl.BlockSpec((tm, n), lambda i: (i, 0))
    full = lambda a: pl.BlockSpec(a.shape, lambda i: (0,) * a.ndim)
    outs = [(c_a, BF16), (c_a, F32), (c_a, F32), (c_a, BF16), (c_a, BF16), (LANES, F32),
            (n_rwkv, F32), (c_c, F32)]
    args = [h2, g1, wp, bfp]
    aliases = {}
    if kv_seq is None:
        tm = _row_tile(rows, 512)
        out_specs = [row(n) for n, _ in outs]
        out_shape = [jax.ShapeDtypeStruct((rows, n), dt) for n, dt in outs]
        in_specs = [row(d), full(g1), full(wp), full(bfp)]
    else:
        layer, depth, bsz, seq_rows, kt, vt = kv_seq
        tm = _row_tile(seq_rows, 256)
        per_seq = seq_rows // tm
        out_specs = [row(n) for n, _ in outs]
        out_shape = [jax.ShapeDtypeStruct((rows, n), dt) for n, dt in outs]
        kv_spec = pl.BlockSpec((None, None, c_a, tm), lambda i: (layer, i // per_seq, 0, i % per_seq))
        kv_shape = jax.ShapeDtypeStruct((depth, bsz, c_a, seq_rows), F32)
        out_specs[1], out_specs[2] = kv_spec, kv_spec
        out_shape[1], out_shape[2] = kv_shape, kv_shape
        in_specs = [row(d), full(g1), full(wp), full(bfp)]
        if kt is not None:
            args += [kt, vt]
            in_specs += [pl.BlockSpec(memory_space=pl.ANY)] * 2
            aliases = {4: 1, 5: 2}
    return pl.pallas_call(
        functools.partial(_inproj_body, c_a=c_a, n_rwkv=n_rwkv, c_c=c_c, q_scale=q_scale,
                          kv_transposed=kv_seq is not None),
        grid=(rows // tm,),
        in_specs=in_specs,
        out_specs=out_specs,
        out_shape=out_shape,
        input_output_aliases=aliases,
        compiler_params=_cparams("parallel"),
        name="inproj",
    )(*args)


N_BIAS = 3
LOG2E = 1.4426950408889634


def _fox_prep_body(lf_ref, q_ref, k_ref, v_ref, qx_ref, kx_ref, vx_ref, carry_ref, *, tc, n_heads, n_valid):
    @pl.when(pl.program_id(1) == 0)
    def _():
        carry_ref[...] = jnp.zeros_like(carry_ref)

    @pl.when(pl.program_id(1) >= n_valid)
    def _():
        for ref in (qx_ref, kx_ref, vx_ref):
            ref[...] = jnp.zeros_like(ref)

    @pl.when(pl.program_id(1) < n_valid)
    def _():
        _fox_prep_tile(lf_ref, q_ref, k_ref, v_ref, qx_ref, kx_ref, vx_ref, carry_ref, tc, n_heads)


def _fox_prep_tile(lf_ref, q_ref, k_ref, v_ref, qx_ref, kx_ref, vx_ref, carry_ref, tc, n_heads):
    tri = jnp.where(_iota((tc, tc), 0) >= _iota((tc, tc), 1), 1.0, 0.0).astype(BF16)
    cs = _dot_r3(tri, lf_ref[0]) + carry_ref[...]
    carry_ref[...] = cs[tc - 1:tc]
    pieces = jnp.concatenate(_split3(cs * LOG2E), axis=1)

    nl = n_heads * LANES
    src = _iota((N_BIAS * LANES, nl), 0)
    dst = _iota((N_BIAS * LANES, nl), 1)
    base = jnp.where((dst // LANES) % 2 == 0, HEAD_A, 0)
    same_head = src % LANES == dst // LANES
    off = dst % LANES - base
    place_q = jnp.where(same_head & (off == src // LANES), 1.0, 0.0).astype(BF16)
    place_k = jnp.where(same_head & (off == N_BIAS + src // LANES), -1.0, 0.0).astype(BF16)
    off1 = off[0:1]
    qe = _dot(pieces, place_q) + jnp.where((off1 >= N_BIAS) & (off1 < 2 * N_BIAS), 1.0, 0.0)
    ke = _dot(pieces, place_k) + jnp.where((off1 >= 0) & (off1 < N_BIAS), 1.0, 0.0)
    lane_half = _iota((tc, LANES), 1) // HEAD_A
    ones = jnp.ones((tc, LANES), BF16)
    for h in range(n_heads):
        pair = slice((h // 2) * LANES, (h // 2 + 1) * LANES)
        tile = slice(h * LANES, (h + 1) * LANES)
        own = lane_half == h % 2
        qx_ref[0, h] = jnp.where(own, q_ref[0][:, pair], qe[:, tile].astype(BF16))
        kx_ref[0, h] = jnp.where(own, k_ref[0][:, pair], ke[:, tile].astype(BF16))
        vx_ref[0, h] = jnp.where(own, v_ref[0][:, pair], ones)


ATTN_TQ = 256
ATTN_TK = 1024


def _fox_prep(lf3, qb, kb, vb):
    b, lp, c_a = qb.shape
    n_heads = c_a // HEAD_A
    tc = 256
    n_valid = lp // tc
    lkv = -(-lp // ATTN_TK) * ATTN_TK
    row = lambda n: pl.BlockSpec((1, tc, n), lambda i, j: (i, jnp.minimum(j, n_valid - 1), 0))
    out = pl.BlockSpec((1, n_heads, tc, LANES), lambda i, j: (i, 0, j, 0))
    return pl.pallas_call(
        functools.partial(_fox_prep_body, tc=tc, n_heads=n_heads, n_valid=n_valid),
        grid=(b, lkv // tc),
        in_specs=[row(LANES), row(c_a), row(c_a), row(c_a)],
        out_specs=[out, out, out],
        out_shape=[jax.ShapeDtypeStruct((b, n_heads, lkv, LANES), BF16)] * 3,
        scratch_shapes=[pltpu.VMEM((1, LANES), F32)],
        compiler_params=_cparams("parallel", "arbitrary"),
        name="fox_prep",
    )(lf3, qb, kb, vb)


def _attn_body(qx_ref, kx_ref, vx_ref, o_ref, s_scr, *, tq, tk):
    row0 = pl.program_id(2) * tq
    n_full = row0 // tk
    diag = (row0 - n_full * tk) // tq
    q = [qx_ref[0, hh] for hh in range(2)]

    def pass_max(j, ms, width, masked):
        start = pl.multiple_of(j * tk, tk)
        out = []
        for hh in range(2):
            s = _dot_nt(q[hh], kx_ref[0, hh, pl.ds(start, width), :])
            if masked:
                s = jnp.where(_iota((tq, width), 1) <= _iota((tq, width), 0) + (width - tq), s, -jnp.inf)
            s_scr[hh, :, pl.ds(start, width)] = s
            m = ms[hh]
            for t in range(width // LANES):
                m = jnp.maximum(m, s[:, t * LANES:(t + 1) * LANES])
            out.append(m)
        return tuple(out)

    def finish(ms, width):
        ms = pass_max(n_full, ms, width, True)
        m1 = [jnp.broadcast_to(jnp.max(m, axis=1, keepdims=True), (tq, LANES)) for m in ms]

        def pass_acc(j, accs, w):
            start = pl.multiple_of(j * tk, tk)
            out = []
            for hh in range(2):
                mb = jnp.concatenate([m1[hh]] * (w // LANES), axis=1)
                p = jnp.exp2(s_scr[hh, :, pl.ds(start, w)] - mb).astype(BF16)
                out.append(accs[hh] + _dot(p, vx_ref[0, hh, pl.ds(start, w), :]))
            return tuple(out)

        accs = tuple(jnp.zeros((tq, LANES), F32) for _ in range(2))
        accs = lax.fori_loop(0, n_full, lambda j, c: pass_acc(j, c, tk), accs)
        accs = pass_acc(n_full, accs, width)
        outs = [acc / pltpu.roll(acc, HEAD_A, 1) for acc in accs]
        o_ref[0] = jnp.where(_iota((tq, LANES), 1) < HEAD_A, outs[0], outs[1]).astype(BF16)

    ms = tuple(jnp.full((tq, LANES), -jnp.inf, F32) for _ in range(2))
    ms = lax.fori_loop(0, n_full, lambda j, c: pass_max(j, c, tk, False), ms)
    for r in range(tk // tq):
        pl.when(diag == r)(functools.partial(finish, ms, tq * (r + 1)))


def _attn_prompt(qx, kx, vx, lp):
    b, n_heads, lkv, _ = qx.shape
    tq, tk = ATTN_TQ, ATTN_TK
    kv = pl.BlockSpec((1, 2, lkv, LANES), lambda bi, p, i: (bi, p, 0, 0))
    return pl.pallas_call(
        functools.partial(_attn_body, tq=tq, tk=tk),
        grid=(b, n_heads // 2, lp // tq),
        in_specs=[pl.BlockSpec((1, 2, tq, LANES), lambda bi, p, i: (bi, p, i, 0)), kv, kv],
        out_specs=pl.BlockSpec((1, tq, LANES), lambda bi, p, i: (bi, i, p)),
        out_shape=jax.ShapeDtypeStruct((b, lp, n_heads * HEAD_A), BF16),
        scratch_shapes=[pltpu.VMEM((2, tq, lkv), F32)],
        compiler_params=_cparams("parallel", "parallel", "arbitrary"),
        name="fox_prompt_attn",
    )(qx, kx, vx)


def _pool_suffix_body(x_ref, o_ref, *, n_h):
    n = x_ref.shape[1]
    later = jnp.where(_iota((n, n), 0) > _iota((n, n), 1), 1.0, 0.0).astype(BF16)
    whole = jnp.ones((n, n), BF16)
    hi, mid, lo = _split3(x_ref[...])
    pages = x_ref.shape[0] // n_h
    o_ref[:, 0:n_h, :] = (_dot(hi, later) + _dot(mid, later) + _dot(lo, later)).reshape(pages, n_h, n)
    o_ref[:, n_h:, :] = (_dot(hi, whole) + _dot(mid, whole) + _dot(lo, whole)).reshape(pages, n_h, n)


def _pool_suffix(ft2, n_h):
    rows, n = ft2.shape
    tm = _row_tile(rows, 1024)
    return pl.pallas_call(
        functools.partial(_pool_suffix_body, n_h=n_h),
        grid=(rows // tm,),
        in_specs=[pl.BlockSpec((tm, n), lambda i: (i, 0))],
        out_specs=pl.BlockSpec((tm // n_h, 2 * n_h, n), lambda i: (i, 0, 0)),
        out_shape=jax.ShapeDtypeStruct((rows // n_h, 2 * n_h, n), F32),
        compiler_params=_cparams("parallel"),
        name="fox_pool_suffix",
    )(ft2)


def _attn_sample_body(pt_ref, qbd_ref, kn_ref, vn_ref, lfn_ref, *rest, n_pages, t_new):
    kp = rest[0:n_pages]
    vp = rest[n_pages:2 * n_pages]
    lfp = rest[2 * n_pages:3 * n_pages]
    o_ref = rest[3 * n_pages]
    nr, c_a = qbd_ref.shape[1], qbd_ref.shape[2]
    n_h = nr // t_new
    qbd = qbd_ref[0]
    row = _iota((nr, LANES), 0)
    lane = _iota((nr, LANES), 1)

    lfn_pad = jnp.concatenate([lfn_ref[0], jnp.zeros((LANES - t_new, LANES), F32)], axis=0)
    tri = jnp.where(_iota((LANES, LANES), 0) >= _iota((LANES, LANES), 1), 1.0, 0.0).astype(BF16)
    cs = _dot_r3(tri, lfn_pad)
    rep = jnp.where(_iota((nr, LANES), 1) == row // n_h, 1.0, 0.0).astype(BF16)
    cs_rows = _dot_r3(rep, cs)
    cq = jnp.sum(jnp.where(lane == row % n_h, cs_rows, 0.0), axis=1, keepdims=True)
    cst = cs.T[0:n_h]

    tiles = [None] * (n_pages + 1)
    carry = jnp.zeros((n_h, LANES), F32)
    for j in reversed(range(n_pages)):
        kt = kp[j][...].reshape(c_a, kp[j].shape[2]).astype(BF16)
        bias = lfp[j][0:n_h] + carry
        tiles[j] = _dot(qbd, kt) + cq + jnp.concatenate([bias] * t_new, axis=0)
        carry = carry + lfp[j][n_h:]
    s = _dot_nt(qbd, kn_ref[0]) + cq - jnp.concatenate([cst] * t_new, axis=0)
    tiles[n_pages] = jnp.where((lane <= row // n_h) & (lane < t_new), s, -jnp.inf)

    m = tiles[0]
    for tl in tiles[1:]:
        m = jnp.maximum(m, tl)
    m = jnp.max(m, axis=1, keepdims=True)
    lsum = jnp.zeros((nr, LANES), F32)
    acc = jnp.zeros((nr, c_a), F32)
    for j in range(n_pages + 1):
        p = jnp.exp(tiles[j] - m)
        lsum = lsum + p
        if j == n_pages:
            acc = acc + _dot(p.astype(BF16), vn_ref[0])
        else:
            vt = vp[j][...].reshape(c_a, vp[j].shape[2]).astype(BF16)
            acc = acc + _dot_nt(p.astype(BF16), vt)
    o_full = acc / jnp.sum(lsum, axis=1, keepdims=True)
    keep = _iota((nr, c_a), 1) // HEAD_A == _iota((nr, c_a), 0) % n_h
    picked = jnp.where(keep, o_full, 0.0).astype(BF16)
    gather = jnp.where((_iota((nr, nr), 1) // n_h == _iota((nr, nr), 0)), 1.0, 0.0).astype(BF16)
    o_ref[0] = _dot(gather, picked)[0:t_new].astype(BF16)


def _attn_sample(page_table, layer, qbd, kn_pad, vn_pad, lfn, cache_kt, cache_vt, lf_pool):
    db, n_pages = page_table.shape
    nr, c_a = qbd.shape[1], qbd.shape[2]
    t_new = lfn.shape[1]
    n_h, hd, page = cache_kt.shape[2:]

    def seq_spec(a):
        return pl.BlockSpec((1,) + a.shape[1:], lambda b, pt: (b,) + (0,) * (a.ndim - 1))

    def pool_spec(j):
        return pl.BlockSpec((None, None, n_h, hd, page), lambda b, pt, j=j: (layer, pt[b, j], 0, 0, 0))

    def small_spec(j):
        return pl.BlockSpec((None, 2 * n_h, page), lambda b, pt, j=j: (pt[b, j], 0, 0))

    in_specs = ([seq_spec(qbd), seq_spec(kn_pad), seq_spec(vn_pad), seq_spec(lfn)]
                + [pool_spec(j) for j in range(n_pages)] * 2
                + [small_spec(j) for j in range(n_pages)])
    grid_spec = pltpu.PrefetchScalarGridSpec(
        num_scalar_prefetch=1,
        grid=(db,),
        in_specs=in_specs,
        out_specs=pl.BlockSpec((1, t_new, c_a), lambda b, pt: (b, 0, 0)),
    )
    args = ([qbd, kn_pad, vn_pad, lfn] + [cache_kt] * n_pages + [cache_vt] * n_pages
            + [lf_pool] * n_pages)
    return pl.pallas_call(
        functools.partial(_attn_sample_body, n_pages=n_pages, t_new=t_new),
        grid_spec=grid_spec,
        out_shape=jax.ShapeDtypeStruct((db, t_new, c_a), BF16),
        compiler_params=_cparams("parallel"),
        name="fox_paged_attn",
    )(page_table, *args)


BLK = SUBLANES


def _rwkv_body(z_ref, sp_ref, s0_ref, mu_ref, w0_ref, w2_ref, a0_ref, a2_ref, g2_ref, kkw_ref, ka_ref,
               rk_ref, lg_ref, lb_ref, o_ref, sout_ref,
               s_scr, carry_scr, at_s, rt_s, bt_s, kt_s, bh_s, kh_s, eg_s, v_s, g_s, bo_s, y_s,
               *, nb, tc, seq_len, c_b):
    c = pl.program_id(1)
    n_pairs = c_b // LANES

    @pl.when(c == 0)
    def _():
        s_scr[...] = s0_ref[...]
        carry_scr[...] = sp_ref[...]

    q_heads = _block_ones(c_b, HEAD_B)
    rowid = _iota((tc, z_ref.shape[2]), 0)
    lora0 = 3 * c_b
    for b in range(nb):
        z = z_ref[b]
        zprev = jnp.where(rowid == 0, carry_scr[b], pltpu.roll(z, 1, 0))
        carry_scr[b] = z[tc - 1:tc]
        zm = z + (zprev - z) * mu_ref[...]
        r = zm[:, 0:c_b]
        k = zm[:, c_b:2 * c_b]
        v = zm[:, 2 * c_b:3 * c_b]
        x_lora = zm[:, lora0:]
        w = _log_sigmoid(w0_ref[...] + _dot_x3(jnp.tanh(x_lora), w2_ref[...])) - 0.5
        a = _sigmoid(a0_ref[...] + _dot_x3(x_lora, a2_ref[...]))
        g = _dot_x3(_sigmoid(x_lora), g2_ref[...])
        kk = k * kkw_ref[...]
        kk = kk / jnp.maximum(jnp.sqrt(_dot_l2(kk * kk, q_heads)), 1e-12)
        k2 = k * (1.0 + (a - 1.0) * ka_ref[...])
        beta = kk * a
        g_s[b] = g
        bo_s[b] = _dot_l2(r * k2 * rk_ref[...], q_heads) * v
        lw = -jnp.exp(w)
        pos = _iota(lw.shape, 0) % BLK
        cl, sx = lw, lw
        shift = 1
        while shift < BLK:
            cl = cl + jnp.where(pos >= shift, pltpu.roll(cl, shift, 0), 0.0)
            sx = sx + jnp.where(pos < BLK - shift, pltpu.roll(sx, tc - shift, 0), 0.0)
            shift *= 2
        sx = sx - lw
        e_inv = jnp.exp(-cl)
        e_out = jnp.exp(sx)
        at_s[b] = -kk * jnp.exp(cl - lw)
        rt_s[b] = r * jnp.exp(cl)
        bt_s[b] = beta * e_inv
        kt_s[b] = k2 * e_inv
        bh_s[b] = beta * e_out
        kh_s[b] = k2 * e_out
        eg_s[b] = jnp.exp(cl + sx)
        v_s[b] = v

    q_pair = _block_ones(LANES, HEAD_B)
    lane = _iota((BLK, LANES), 1)
    sub = _iota((BLK, LANES), 0)
    head0 = lane < HEAD_B
    n_steps = jnp.clip(seq_len - c * tc, 0, tc)
    chains = [(b, p) for b in range(nb) for p in range(n_pairs)]
    bc = lambda x, j: jnp.broadcast_to(x[j:j + 1], x.shape)

    @pl.when(n_steps < tc)
    def _():
        y_s[...] = jnp.zeros_like(y_s)

    def block(gi, carry):
        rows = pl.ds(pl.multiple_of(gi * BLK, BLK), BLK)
        tiles, pair_hi, pair_lo, m1 = [], [], [], []
        for b, p in chains:
            cs = slice(p * LANES, (p + 1) * LANES)
            at, rt, bt, kt = at_s[b, rows, cs], rt_s[b, rows, cs], bt_s[b, rows, cs], kt_s[b, rows, cs]
            tiles.append((at, rt))
            fb, out = [], []
            for j in range(BLK):
                bj, kj = bc(bt, j), bc(kt, j)
                if j < BLK - 1:
                    at_j = jnp.where(sub > j, at, 0.0)
                    fb += [at_j * bj, at_j * kj]
                rt_j = jnp.where(sub >= j, rt, 0.0)
                out += [rt_j * bj, rt_j * kj]
            fb = jnp.concatenate(fb, axis=0)
            fb_hi = fb.astype(BF16)
            pair_hi += [fb_hi, jnp.concatenate(out, axis=0).astype(BF16)]
            pair_lo.append((fb - fb_hi.astype(F32)).astype(BF16))
            lhs = jnp.concatenate([jnp.where(head0, at, 0.0), jnp.where(head0, 0.0, at),
                                   jnp.where(head0, rt, 0.0), jnp.where(head0, 0.0, rt)], axis=0)
            lhs_hi = lhs.astype(BF16)
            lhs_lo = (lhs - lhs_hi.astype(F32)).astype(BF16)
            st_hi, st_lo = _split2(s_scr[b, p])
            z = _dot_nt(jnp.concatenate([lhs_hi, lhs_lo], axis=0), st_hi)
            z = z[0:4 * BLK] + z[4 * BLK:] + _dot_nt(lhs_hi, st_lo)
            m1.append(z)
        n_fb, n_out = 2 * (BLK - 1) * BLK, 2 * BLK * BLK
        dots_hi = _dot(jnp.concatenate(pair_hi, axis=0), q_pair)
        dots_lo = _dot(jnp.concatenate(pair_lo, axis=0), q_pair)
        for ci, (b, p) in enumerate(chains):
            cs = slice(p * LANES, (p + 1) * LANES)
            o = ci * (n_fb + n_out)
            fb = dots_hi[o:o + n_fb] + dots_lo[ci * n_fb:(ci + 1) * n_fb]
            out = dots_hi[o + n_fb:o + n_fb + n_out]
            z = m1[ci]
            vv = v_s[b, rows, cs]
            u = jnp.concatenate([z[0:BLK], z[BLK:2 * BLK]], axis=1)
            y = jnp.concatenate([z[2 * BLK:3 * BLK], z[3 * BLK:]], axis=1)
            for j in range(BLK - 1):
                u = u + fb[(2 * j + 1) * BLK:(2 * j + 2) * BLK] * bc(vv, j)
            for j in range(BLK - 1):
                u = u + fb[2 * j * BLK:(2 * j + 1) * BLK] * bc(u, j)
            for j in range(BLK):
                y = y + out[2 * j * BLK:(2 * j + 1) * BLK] * bc(u, j) + out[(2 * j + 1) * BLK:(2 * j + 2) * BLK] * bc(vv, j)
            y_s[b, rows, cs] = y
            u_hi = u.astype(BF16).astype(F32)
            v_hi = vv.astype(BF16).astype(F32)
            bh, kh = bh_s[b, rows, cs], kh_s[b, rows, cs]
            bh_hi = bh.astype(BF16).astype(F32)
            kh_hi = kh.astype(BF16).astype(F32)
            left = jnp.concatenate([u_hi, u - u_hi, u_hi, v_hi, vv - v_hi, v_hi], axis=0).astype(BF16)
            right = jnp.concatenate([bh_hi, bh_hi, bh - bh_hi, kh_hi, kh_hi, kh - kh_hi], axis=0).astype(BF16)
            upd = lax.dot_general(left, right, (((0,), (0,)), ((), ())), preferred_element_type=F32)
            upd = jnp.where(_iota((HEAD_B, LANES), 1) < HEAD_B, upd[0:HEAD_B], upd[HEAD_B:])
            s_scr[b, p] = s_scr[b, p] * eg_s[b, rows, cs][0:1] + upd
        return carry

    lax.fori_loop(0, n_steps // BLK, block, 0)

    inv_n = 1.0 / HEAD_B
    for b in range(nb):
        y = y_s[b]
        mu = _dot_l2(y, q_heads) * inv_n
        d = y - mu
        var = _dot_l2(d * d, q_heads) * inv_n
        yn = d * lax.rsqrt(var + GROUPNORM_EPS) * lg_ref[...] + lb_ref[...]
        o_ref[b] = ((yn + bo_s[b]) * g_s[b]).astype(BF16)
    sout_ref[...] = s_scr[...]


def _rwkv(z3, shift_prev, s0p, prm, seq_len, tc):
    bt, lp, n_rwkv = z3.shape
    nb = 4
    c_b = prm["w0"].shape[1]
    n_pairs = c_b // LANES
    full = lambda a: pl.BlockSpec(a.shape, lambda g, c: (0,) * a.ndim)
    names = ["mu", "w0", "w2", "a0", "a2", "g2", "kkw", "ka", "rk", "lg", "lb"]
    plist = [prm[n] for n in names]
    assert tc % BLK == 0 and seq_len % BLK == 0
    chunk = lambda: pltpu.VMEM((nb, tc, c_b), F32)
    return pl.pallas_call(
        functools.partial(_rwkv_body, nb=nb, tc=tc, seq_len=seq_len, c_b=c_b),
        grid=(bt // nb, lp // tc),
        in_specs=[pl.BlockSpec((nb, tc, n_rwkv), lambda g, c: (g, c, 0)),
                  pl.BlockSpec((nb, 1, n_rwkv), lambda g, c: (g, 0, 0)),
                  pl.BlockSpec((nb, n_pairs, HEAD_B, LANES), lambda g, c: (g, 0, 0, 0))]
                 + [full(a) for a in plist],
        out_specs=[pl.BlockSpec((nb, tc, c_b), lambda g, c: (g, c, 0)),
                   pl.BlockSpec((nb, n_pairs, HEAD_B, LANES), lambda g, c: (g, 0, 0, 0))],
        out_shape=[jax.ShapeDtypeStruct((bt, lp, c_b), BF16),
                   jax.ShapeDtypeStruct((bt, n_pairs, HEAD_B, LANES), F32)],
        scratch_shapes=[pltpu.VMEM((nb, n_pairs, HEAD_B, LANES), F32),
                        pltpu.VMEM((nb, 1, n_rwkv), F32)] + [chunk() for _ in range(11)],
        compiler_params=_cparams("parallel", "arbitrary"),
        name="rwkv7",
    )(z3, shift_prev, s0p, *plist)


def _s5_prep_body(are_ref, aim_ref, ldt_ref, bre_ref, bim_ref, abre_ref, abim_ref, bbre_ref, bbim_ref):
    a_re = are_ref[...]
    a_im = aim_ref[...]
    step = jnp.exp(ldt_ref[...])
    mag = jnp.exp(step * a_re)
    ab_re = mag * jnp.cos(step * a_im)
    ab_im = mag * jnp.sin(step * a_im)
    den = a_re * a_re + a_im * a_im
    cf_re = ((ab_re - 1.0) * a_re + ab_im * a_im) / den
    cf_im = (ab_im * a_re - (ab_re - 1.0) * a_im) / den
    abre_ref[...] = ab_re
    abim_ref[...] = ab_im
    bbre_ref[...] = cf_re * bre_ref[...] - cf_im * bim_ref[...]
    bbim_ref[...] = cf_re * bim_ref[...] + cf_im * bre_ref[...]


def _s5_prep(a_re, a_im, log_dt, b_re, b_im):
    g, p = a_re.shape
    grp = b_re.shape[2]
    r3 = lambda x: x.reshape(g, 1, p)
    ldt = jnp.broadcast_to(log_dt[:, None, None], (g, 1, p))
    bt = lambda x: jnp.transpose(x, (0, 2, 1))
    vec = jax.ShapeDtypeStruct((g, 1, p), F32)
    mat = jax.ShapeDtypeStruct((g, grp, p), F32)
    return pl.pallas_call(_s5_prep_body, out_shape=[vec, vec, mat, mat], name="s5_prep")(
        r3(a_re), r3(a_im), ldt, bt(b_re), bt(b_im))


def _gelu_tanh(x):
    return 0.5 * x * (1.0 + jnp.tanh(0.7978845608028654 * (x + 0.044715 * (x * x * x))))


def _s5_body(u_ref, re0_ref, im0_ref, bb_ref, ab_ref, cre_ref, cim_ref, d_ref, gw_ref, gb_ref,
             o_ref, reo_ref, imo_ref, xr_scr, xi_scr, xr_s, xi_s,
             *, nb, tc, seq_len, precise):
    c = pl.program_id(1)
    n_tiles = xr_s.shape[0]
    n_state = n_tiles * LANES
    lanes = lambda j: slice(j * LANES, (j + 1) * LANES)
    seq_rows = lambda b: pl.ds(b, tc, stride=SUBLANES)

    @pl.when(c == 0)
    def _():
        for j in range(n_tiles):
            xr_scr[j] = re0_ref[0][:, lanes(j)]
            xi_scr[j] = im0_ref[0][:, lanes(j)]
        if nb < SUBLANES:
            xr_s[...] = jnp.zeros_like(xr_s)
            xi_s[...] = jnp.zeros_like(xi_s)

    u = u_ref[...].reshape(nb * tc, u_ref.shape[2])
    for half, dst in enumerate((xr_s, xi_s)):
        cols = slice(half * n_state, (half + 1) * n_state)
        if precise:
            uh, ul = _split2(u)
            bu = _dot(uh, bb_ref[0, :, cols]) + _dot(ul, bb_ref[0, :, cols]) + _dot(uh, bb_ref[1, :, cols])
        else:
            bu = _dot(u.astype(BF16), bb_ref[0, :, cols])
        for b in range(nb):
            for j in range(n_tiles):
                dst[j, seq_rows(b), :] = bu[b * tc:(b + 1) * tc, lanes(j)]
    ar = ab_ref[0]
    ai = ab_ref[1]

    def step(t, carry):
        xr, xi = carry
        rows = pl.ds(pl.multiple_of(t * SUBLANES, SUBLANES), SUBLANES)
        nr = ar * xr - ai * xi + xr_s[:, rows, :]
        ni = ar * xi + ai * xr + xi_s[:, rows, :]
        xr_s[:, rows, :] = nr
        xi_s[:, rows, :] = ni
        return nr, ni

    n_steps = jnp.clip(seq_len - c * tc, 0, tc)
    xr, xi = lax.fori_loop(0, n_steps, step, (xr_scr[...], xi_scr[...]))
    xr_scr[...] = xr
    xi_scr[...] = xi
    reo_ref[0] = jnp.concatenate([xr[j] for j in range(n_tiles)], axis=1)
    imo_ref[0] = jnp.concatenate([xi[j] for j in range(n_tiles)], axis=1)

    def states(src):
        return jnp.concatenate([jnp.concatenate([src[j, seq_rows(b), :] for j in range(n_tiles)], axis=1)
                                for b in range(nb)], axis=0).astype(BF16)

    y = _dot(states(xr_s), cre_ref[...]) - _dot(states(xi_s), cim_ref[...]) + d_ref[...] * u
    hg = _gelu_tanh(y)
    out = hg * _sigmoid(_dot_x3(hg, gw_ref[...]) + gb_ref[...])
    o_ref[...] = out.astype(BF16).reshape(o_ref.shape)


def _s5(u3, re0, im0, prm, seq_len, nb, tc, precise):
    bt, lp, c_c = u3.shape
    n_state = re0.shape[2]
    full = lambda a: pl.BlockSpec(a.shape, lambda g, c: (0,) * a.ndim)
    plist = [prm[n] for n in ("bb", "ab", "cre", "cim", "d", "gw", "gb")]
    st = pl.BlockSpec((1, SUBLANES, n_state), lambda g, c: (g, 0, 0))
    st_shape = jax.ShapeDtypeStruct((bt // nb, SUBLANES, n_state), F32)
    rows = tc * SUBLANES
    n_tiles = n_state // LANES
    return pl.pallas_call(
        functools.partial(_s5_body, nb=nb, tc=tc, seq_len=seq_len, precise=precise),
        grid=(bt // nb, lp // tc),
        in_specs=[pl.BlockSpec((nb, tc, c_c), lambda g, c: (g, c, 0)), st, st] + [full(a) for a in plist],
        out_specs=[pl.BlockSpec((nb, tc, c_c), lambda g, c: (g, c, 0)), st, st],
        out_shape=[jax.ShapeDtypeStruct((bt, lp, c_c), BF16), st_shape, st_shape],
        scratch_shapes=[pltpu.VMEM((n_tiles, SUBLANES, LANES), F32), pltpu.VMEM((n_tiles, SUBLANES, LANES), F32),
                        pltpu.VMEM((n_tiles, rows, LANES), F32), pltpu.VMEM((n_tiles, rows, LANES), F32)],
        compiler_params=_cparams("parallel", "arbitrary"),
        name="s5",
    )(u3, re0, im0, *plist)


def _merge_body(h_ref, oa_ref, ob_ref, oc_ref, g_ref, wg_ref, pa_ref, pb_ref, pc_ref, wo_ref, o_ref, *, d):
    h = h_ref[...]
    xn = _rms(h, g_ref[...]).astype(BF16)
    merged = None
    for n, (br_ref, pj_ref) in enumerate(((oa_ref, pa_ref), (ob_ref, pb_ref), (oc_ref, pc_ref))):
        gate = _sigmoid(_dot(xn, wg_ref[:, n * d:(n + 1) * d]))
        term = gate * _dot(br_ref[...], pj_ref[...])
        merged = term if merged is None else merged + term
    o_ref[...] = h + _dot(merged.astype(BF16), wo_ref[...])


def _merge(h2, oa, ob, oc, g1, wg, pa, pb, pc, wo):
    rows, d = h2.shape
    tm = _row_tile(rows, 512)
    row = lambda a: pl.BlockSpec((tm, a.shape[1]), lambda i: (i, 0))
    full = lambda a: pl.BlockSpec(a.shape, lambda i: (0,) * a.ndim)
    return pl.pallas_call(
        functools.partial(_merge_body, d=d),
        grid=(rows // tm,),
        in_specs=[row(h2), row(oa), row(ob), row(oc)] + [full(a) for a in (g1, wg, pa, pb, pc, wo)],
        out_specs=row(h2),
        out_shape=jax.ShapeDtypeStruct(h2.shape, F32),
        compiler_params=_cparams("parallel"),
        name="merge_outproj",
    )(h2, oa, ob, oc, g1, wg, pa, pb, pc, wo)


PREV_ROWS = 16


def _ffn_body(*refs, tm, seq_rows, seq_valid, decode, final):
    o_ref, xn_scr = refs[-2:]
    fg_ref = refs[-3] if final else None
    ins = refs[:-3] if final else refs[:-2]
    if decode:
        x_ref, pa_ref, pb_ref, g_ref, ua_ref, ub_ref, cwa_ref, cwb_ref, cba_ref, cbb_ref, dn_ref = ins
    else:
        x_ref, xp_ref, g_ref, ua_ref, ub_ref, cwa_ref, cwb_ref, cba_ref, cbb_ref, dn_ref = ins
    i = pl.program_id(0)
    j = pl.program_id(1)

    @pl.when(j == 0)
    def _():
        o_ref[...] = x_ref[...]
        if decode:
            xn_scr[...] = _rms(x_ref[...], g_ref[...]).astype(BF16)
        else:
            d = x_ref.shape[1]
            pos_p = (i * tm - PREV_ROWS + _iota((PREV_ROWS, d), 0)) % seq_rows
            keep = (pos_p < seq_valid) & (i > 0)
            xn_scr[0:PREV_ROWS] = jnp.where(keep, _rms(xp_ref[...], g_ref[...]), 0.0).astype(BF16)
            pos = (i * tm + _iota((tm, d), 0)) % seq_rows
            xn_scr[PREV_ROWS:] = jnp.where(pos < seq_valid, _rms(x_ref[...], g_ref[...]), 0.0).astype(BF16)

    nc = ua_ref.shape[1]
    xn = xn_scr[...]

    def conv(up_ref, cw_ref, cb_ref, prev_ref):
        h_ext = _dot(xn, up_ref[...])
        if decode:
            hu = h_ext
            prev = prev_ref[...]
            pos = _iota((tm, nc), 0) % seq_rows
            sh1 = jnp.where(pos == 0, pltpu.roll(prev, tm - 1, 0), pltpu.roll(hu, 1, 0))
            sh2 = jnp.where(pos < 2, prev, pltpu.roll(hu, 2, 0))
        else:
            hu = h_ext[PREV_ROWS:]
            last = h_ext[PREV_ROWS - 1:PREV_ROWS]
            last2 = h_ext[PREV_ROWS - 2:PREV_ROWS - 1]
            r1 = pltpu.roll(hu, 1, 0)
            r2 = pltpu.roll(hu, 2, 0)
            row8 = _iota((SUBLANES, nc), 0)
            head1 = jnp.where(row8 == 0, last, r1[0:SUBLANES])
            head2 = jnp.where(row8 == 0, last2, jnp.where(row8 == 1, last, r2[0:SUBLANES]))
            sh1 = jnp.concatenate([head1, r1[SUBLANES:]], axis=0)
            sh2 = jnp.concatenate([head2, r2[SUBLANES:]], axis=0)
        return cw_ref[0:1, :] * sh2 + cw_ref[1:2, :] * sh1 + cw_ref[2:3, :] * hu + cb_ref[...]

    a = conv(ua_ref, cwa_ref, cba_ref, pa_ref if decode else None)
    bgate = conv(ub_ref, cwb_ref, cbb_ref, pb_ref if decode else None)
    act = (a * _sigmoid(a)) * bgate
    contrib = _dot(act.astype(BF16), dn_ref[...])
    if not final:
        o_ref[...] += contrib
        return
    last = pl.num_programs(1) - 1

    @pl.when(j < last)
    def _():
        o_ref[...] += contrib

    @pl.when(j == last)
    def _():
        o_ref[...] = _rms(o_ref[...] + contrib, fg_ref[...])


def _ffn(h2, g2, ua, ub, cwa, cwb, cba, cbb, dn, seq_rows, seq_valid, prev_pad=None, final_g=None):
    assert prev_pad is not None or seq_rows - seq_valid >= CONV_TAPS - 1
    rows, d = h2.shape
    d_ff = ua.shape[1]
    nc = 256
    n_chunks = d_ff // nc
    decode = prev_pad is not None
    tm = _row_tile(rows, 1024)
    xrow = pl.BlockSpec((tm, d), lambda i, j: (i, 0))
    col = lambda r: pl.BlockSpec((r, nc), lambda i, j: (0, j))
    w_specs = [pl.BlockSpec((1, d), lambda i, j: (0, 0)), col(d), col(d), col(CONV_TAPS), col(CONV_TAPS),
               col(1), col(1), pl.BlockSpec((nc, d), lambda i, j: (j, 0))]
    scratch = [pltpu.VMEM((tm if decode else tm + PREV_ROWS, d), BF16)]
    if decode:
        in_specs = [xrow, pl.BlockSpec((tm, nc), lambda i, j: (i, j)),
                    pl.BlockSpec((tm, nc), lambda i, j: (i, n_chunks + j))] + w_specs
        args = (h2, prev_pad, prev_pad, g2, ua, ub, cwa, cwb, cba, cbb, dn)
    else:
        per = tm // PREV_ROWS
        in_specs = [xrow, pl.BlockSpec((PREV_ROWS, d), lambda i, j: (jnp.maximum(i * per - 1, 0), 0))] + w_specs
        args = (h2, h2, g2, ua, ub, cwa, cwb, cba, cbb, dn)
    if final_g is not None:
        in_specs = in_specs + [pl.BlockSpec((1, d), lambda i, j: (0, 0))]
        args = args + (final_g,)
    return pl.pallas_call(
        functools.partial(_ffn_body, tm=tm, seq_rows=seq_rows, seq_valid=seq_valid, decode=decode,
                          final=final_g is not None),
        grid=(rows // tm, n_chunks),
        in_specs=in_specs,
        out_specs=xrow,
        out_shape=jax.ShapeDtypeStruct(h2.shape, F32),
        scratch_shapes=scratch,
        compiler_params=_cparams("parallel", "arbitrary"),
        name="conv_ffn",
    )(*args)


def _up_rows_body(x_ref, g_ref, up_ref, o_ref):
    o_ref[...] = _dot(_rms(x_ref[...], g_ref[...]).astype(BF16), up_ref[...])


def _up_rows(x2, g2, up):
    rows, d = x2.shape
    n = up.shape[1]
    nc = 512
    return pl.pallas_call(
        _up_rows_body,
        grid=(n // nc,),
        in_specs=[pl.BlockSpec((rows, d), lambda j: (0, 0)), pl.BlockSpec((1, d), lambda j: (0, 0)),
                  pl.BlockSpec((d, nc), lambda j: (0, j))],
        out_specs=pl.BlockSpec((rows, nc), lambda j: (0, j)),
        out_shape=jax.ShapeDtypeStruct((rows, n), F32),
        compiler_params=_cparams("parallel"),
        name="conv_state_rows",
    )(x2, g2, up)


def _pack_state(s):
    b, h, v, k = s.shape
    return s.reshape(b, h // 2, 2, v, k).transpose(0, 1, 3, 2, 4).reshape(b, h // 2, v, 2 * k)


def _unpack_state(s):
    b, hp, v, k2 = s.shape
    return s.reshape(b, hp, v, 2, k2 // 2).transpose(0, 1, 3, 2, 4).reshape(b, 2 * hp, v, k2 // 2)


def _block_diag(x):
    g, a, c = x.shape
    eye = jnp.eye(g, dtype=bool)[:, None, :, None]
    return jnp.where(eye, x[:, :, None, :], 0.0).reshape(g * a, g * c)


def _pad_rows(x, n):
    return jnp.concatenate([x, jnp.zeros((n - x.shape[0],) + x.shape[1:], x.dtype)], axis=0)


def kernel(x_prompt, x_sample, cache_k, cache_v, cache_logf, page_table, state_rwkv_wkv, state_rwkv_shift, state_ssm_re, state_ssm_im, state_ffn_conv, meta_tokens, norm1_g, w_in, fox_bf, rwkv_mu, rwkv_w0, rwkv_w2, rwkv_a0, rwkv_a2, rwkv_g2, rwkv_kk, rwkv_ka, rwkv_rk, rwkv_lnx_g, rwkv_lnx_b, ssm_a_re, ssm_a_im, ssm_log_dt, ssm_b_re, ssm_b_im, ssm_c_re, ssm_c_im, ssm_d, ssm_glu_w, ssm_glu_b, proj_a, proj_b, proj_c, w_out, norm2_g, ffn_up, ffn_conv_w, ffn_conv_b, ffn_down, final_norm_g):
    bsz, seq, d = x_prompt.shape
    db, t_new, _ = x_sample.shape
    depth = norm1_g.shape[0]
    n_ha = fox_bf.shape[1]
    c_a = n_ha * HEAD_A
    n_rwkv = rwkv_mu.shape[1]
    c_b = rwkv_w0.shape[1]
    n_hb = c_b // HEAD_B
    g_c, p_c = ssm_a_re.shape[1], ssm_a_re.shape[2]
    c_c = g_c * SSM_GRP
    n_state = g_c * p_c
    d_ff = ffn_down.shape[1]
    n_pool, page = cache_k.shape[1], cache_k.shape[2]
    seq_len = N_META_TOK + seq
    lp = -(-seq_len // 256) * 256
    seq_tile = 256

    meta = jnp.broadcast_to(meta_tokens[None], (bsz, N_META_TOK, d))
    h_p = jnp.concatenate([meta, x_prompt, jnp.zeros((bsz, lp - seq_len, d), F32)], axis=1)
    h_p = h_p.reshape(bsz * lp, d)
    h_s = x_sample.reshape(db * t_new, d)

    cache_kt = jnp.transpose(cache_k, (0, 1, 3, 4, 2))
    cache_vt = jnp.transpose(cache_v, (0, 1, 3, 4, 2))
    eye_h = jnp.eye(n_ha, dtype=bool)

    zeros_shift = jnp.zeros((bsz, 1, n_rwkv), F32)
    zeros_wkv = jnp.zeros((bsz, n_hb // 2, HEAD_B, 2 * HEAD_B), F32)
    assert bsz <= SUBLANES and db % SUBLANES == 0
    zeros_ssm = jnp.zeros((1, SUBLANES, n_state), F32)

    outs_p, outs_s = [], []
    kt_p = vt_p = None
    fg = final_norm_g[None]
    for l in range(depth):
        w = w_in[l]
        o_f = 3 * c_a
        o_z = o_f + n_ha
        o_g = o_z + n_rwkv + c_c
        wp = jnp.concatenate([w[:, :o_f], w[:, o_f:o_z], jnp.zeros((d, LANES - n_ha), F32),
                              w[:, o_z:o_g]], axis=1).astype(BF16)
        wg = w[:, o_g:].astype(BF16)
        bfp = jnp.concatenate([fox_bf[l], jnp.zeros((LANES - n_ha,), F32)])[None]
        g1 = norm1_g[l][None]
        g2 = norm2_g[l][None]
        row1 = lambda x: x.reshape(1, -1)
        lora_pad = lambda x, o: jnp.zeros((n_rwkv - 3 * c_b, c_b), F32).at[o:o + x.shape[0]].set(x)
        rw = dict(mu=row1(rwkv_mu[l]), w0=row1(rwkv_w0[l]), w2=lora_pad(rwkv_w2[l], 0),
                  a0=row1(rwkv_a0[l]), a2=lora_pad(rwkv_a2[l], LORA_W_DIM),
                  g2=lora_pad(rwkv_g2[l], LORA_W_DIM + LORA_A_DIM), kkw=row1(rwkv_kk[l]),
                  ka=row1(rwkv_ka[l]), rk=row1(rwkv_rk[l]), lg=row1(rwkv_lnx_g[l]), lb=row1(rwkv_lnx_b[l]))
        ab_re, ab_im, bbt_re, bbt_im = _s5_prep(ssm_a_re[l], ssm_a_im[l], ssm_log_dt[l], ssm_b_re[l], ssm_b_im[l])
        bb = jnp.concatenate([_block_diag(bbt_re), _block_diag(bbt_im)], axis=1)
        bb_hi = bb.astype(BF16)
        bb_lo = (bb - bb_hi.astype(F32)).astype(BF16)
        s5p = dict(bb=jnp.stack([bb_hi, bb_lo]),
                   ab=jnp.stack([ab_re, ab_im]).reshape(2, n_state // LANES, 1, LANES),
                   cre=_block_diag(jnp.transpose(ssm_c_re[l], (0, 2, 1))).astype(BF16),
                   cim=_block_diag(jnp.transpose(ssm_c_im[l], (0, 2, 1))).astype(BF16),
                   d=row1(ssm_d[l]), gw=ssm_glu_w[l], gb=row1(ssm_glu_b[l]))
        pa, pb, pc, wo = (x[l].astype(BF16) for x in (proj_a, proj_b, proj_c, w_out))
        up = ffn_up[l].astype(BF16)
        ua, ub = up[:, :d_ff], up[:, d_ff:]
        cwa, cwb = ffn_conv_w[l][:, :d_ff], ffn_conv_w[l][:, d_ff:]
        cba, cbb = ffn_conv_b[l][None, :d_ff], ffn_conv_b[l][None, d_ff:]
        dn = ffn_down[l].astype(BF16)

        qb, kt_p, vt_p, kb, vb, lf, z, u = _inproj(h_p, g1, wp, bfp, c_a, n_rwkv, c_c, HEAD_A ** -0.5 * LOG2E,
                                                   kv_seq=(l, depth, bsz, lp, kt_p, vt_p))
        r3 = lambda x: x.reshape(bsz, lp, x.shape[1])
        o_a = _attn_prompt(*_fox_prep(r3(lf), r3(qb), r3(kb), r3(vb)), lp)
        o_b, wkv = _rwkv(r3(z), zeros_shift, zeros_wkv, rw, seq_len, seq_tile)
        o_c, re_n, im_n = _s5(r3(u), zeros_ssm, zeros_ssm, s5p, seq_len, bsz, seq_tile // 2, False)
        flat = lambda x: x.reshape(bsz * lp, x.shape[2])
        h_mid = _merge(h_p, flat(o_a), flat(o_b), flat(o_c), g1, wg, pa, pb, pc, wo)
        h_p = _ffn(h_mid, g2, ua, ub, cwa, cwb, cba, cbb, dn, lp, seq_len, final_g=fg if l == depth - 1 else None)
        tail = r3(h_mid)[:, seq_len - 2:seq_len].reshape(bsz * 2, d)
        conv = _up_rows(_pad_rows(tail, 16), g2, up)[:bsz * 2].reshape(bsz, 2, 2 * d_ff)
        outs_p.append((r3(lf)[:, :seq_len, :n_ha], _unpack_state(wkv), r3(z)[:, seq_len - 1],
                       re_n[0, :bsz].reshape(bsz, g_c, p_c), im_n[0, :bsz].reshape(bsz, g_c, p_c), conv))

        qb, k, v, kb, vb, lf, z, u = _inproj(h_s, g1, wp, bfp, c_a, n_rwkv, c_c, HEAD_A ** -0.5)
        s3 = lambda x: x.reshape(db, t_new, x.shape[1])
        q4 = s3(qb).reshape(db, t_new, n_ha, HEAD_A)
        qbd = jnp.where(eye_h[None, None, :, :, None], q4[:, :, :, None, :], jnp.zeros((), BF16))
        qbd = qbd.reshape(db, t_new * n_ha, c_a)
        pad_new = lambda x: jnp.concatenate([s3(x), jnp.zeros((db, LANES - t_new, c_a), BF16)], axis=1)
        lf_pool = _pool_suffix(jnp.transpose(cache_logf[l], (0, 2, 1)).reshape(n_pool * n_ha, page), n_ha)
        o_a = _attn_sample(page_table, l, qbd, pad_new(kb), pad_new(vb), s3(lf), cache_kt, cache_vt, lf_pool)
        o_b, wkv = _rwkv(s3(z), state_rwkv_shift[l][:, None], _pack_state(state_rwkv_wkv[l]), rw, t_new, t_new)
        o_c, re_n, im_n = _s5(s3(u), state_ssm_re[l].reshape(db // SUBLANES, SUBLANES, n_state),
                              state_ssm_im[l].reshape(db // SUBLANES, SUBLANES, n_state), s5p, t_new,
                              SUBLANES, t_new, True)
        flat = lambda x: x.reshape(db * t_new, x.shape[2])
        h_mid = _merge(h_s, flat(o_a), flat(o_b), flat(o_c), g1, wg, pa, pb, pc, wo)
        prev = state_ffn_conv[l]
        prev_pad = jnp.concatenate([prev, jnp.zeros((db, t_new - prev.shape[1], prev.shape[2]), F32)], axis=1)
        h_s = _ffn(h_mid, g2, ua, ub, cwa, cwb, cba, cbb, dn, t_new, t_new,
                   prev_pad.reshape(db * t_new, 2 * d_ff), final_g=fg if l == depth - 1 else None)
        tail = s3(h_mid)[:, t_new - 2:].reshape(db * 2, d)
        conv = _up_rows(tail, g2, up).reshape(db, 2, 2 * d_ff)
        outs_s.append((s3(k).reshape(db, t_new, n_ha, HEAD_A), s3(v).reshape(db, t_new, n_ha, HEAD_A),
                       s3(lf)[:, :, :n_ha], _unpack_state(wkv), s3(z)[:, t_new - 1],
                       re_n.reshape(db, g_c, p_c), im_n.reshape(db, g_c, p_c), conv))

    y_p = h_p.reshape(bsz, lp, d)[:, N_META_TOK:seq_len]
    y_s = h_s.reshape(db, t_new, d)
    stack = lambda outs: [jnp.stack(t) for t in zip(*outs)]
    seq_last = lambda x: jnp.transpose(x[..., :seq_len].reshape(depth, bsz, n_ha, HEAD_A, seq_len),
                                       (0, 1, 4, 2, 3))
    return tuple([y_p, y_s, seq_last(kt_p), seq_last(vt_p)] + stack(outs_p) + stack(outs_s))
```
